```python
import jax, jax.numpy as jnp
from jax import lax
import numpy as np

D_MODEL = 2048
BATCH = 4
SEQ = 2048
DEPTH = 2

N_META = 16
CONF_W = 1024
CONF_K = 31
SC_W = 1024
SC_K = 3
AB_SIZES = (CONF_W, CONF_W, SC_W, SC_W, SC_W)
AB_IN = sum(AB_SIZES)
GLA_HEADS = 4
GLA_DK = 128
GLA_DV = 256
GLA_QK = GLA_HEADS * GLA_DK
GLA_V = GLA_HEADS * GLA_DV
GLA_RANK = 16
GLA_GATE_NORM = 16.0
GLA_CHUNK = 64
GLA_SIZES = (GLA_QK, GLA_QK, GLA_V, GLA_V, GLA_RANK)
RW_HEADS = 16
RW_N = 64
RW_W = RW_HEADS * RW_N
RW_DECAY_RANK = 64
RW_A_RANK = 64
RW_G_RANK = 128
RW_SIZES = (RW_W, RW_W, RW_W, RW_DECAY_RANK, RW_A_RANK, RW_G_RANK)
GLA_COLS = sum(GLA_SIZES)
RW_COLS = sum(RW_SIZES)
CD_IN = GLA_COLS + RW_COLS
RW_GN_EPS = 64e-5
D_FF = 5632
FFN_K = 3
EPS = 1e-6
LN_EPS = 1e-5
N_EVEN = (DEPTH + 1) // 2
N_ODD = DEPTH // 2

kernel_name = "hybrid_conv_gla_rwkv7_meta"


def _splits(sizes):
    return [int(s) for s in np.cumsum(sizes)[:-1]]


def rmsnorm(x, g):
    x32 = x.astype(jnp.float32)
    y = x32 * lax.rsqrt(jnp.mean(x32 * x32, axis=-1, keepdims=True) + EPS)
    return (y * g.astype(jnp.float32)).astype(x.dtype)


def layernorm(x, g, b):
    x32 = x.astype(jnp.float32)
    mu = jnp.mean(x32, axis=-1, keepdims=True)
    var = jnp.mean(jnp.square(x32 - mu), axis=-1, keepdims=True)
    y = (x32 - mu) * lax.rsqrt(var + LN_EPS) * g.astype(jnp.float32) + b.astype(jnp.float32)
    return y.astype(x.dtype)


def causal_dwconv(x, w):
    K = w.shape[0]
    L = x.shape[1]
    xp = jnp.pad(x, ((0, 0), (K - 1, 0), (0, 0)))
    out = xp[:, 0:L] * w[0]
    for t in range(1, K):
        out = out + xp[:, t:t + L] * w[t]
    return out


def token_shift(z):
    return jnp.pad(z, ((0, 0), (1, 0), (0, 0)))[:, :-1]


def mixer_ab(h, w_in, conf_dw, conf_dw_b, conf_ln_g, conf_ln_b, sc_dw, w_out):
    z = h @ w_in
    a_val, a_gate, s_b, s_c, s_x = jnp.split(z, _splits(AB_SIZES), axis=-1)
    a = a_val * jax.nn.sigmoid(a_gate)
    a = causal_dwconv(a, conf_dw) + conf_dw_b
    a = jax.nn.silu(layernorm(a, conf_ln_g, conf_ln_b))
    s = s_b * causal_dwconv(s_c * s_x, sc_dw)
    return jnp.concatenate([a, s], axis=-1) @ w_out


def gla_chunked(q, k, v, log_a):
    Bs, T, H, dk = q.shape
    dv = v.shape[-1]
    C = GLA_CHUNK
    nc = T // C

    def blk(t):
        return t.reshape(Bs, nc, C, H, t.shape[-1]).transpose(0, 3, 1, 2, 4)

    q, k, v, g = blk(q), blk(k), blk(v), blk(log_a)
    cum = jnp.cumsum(g, axis=3)
    last = cum[:, :, :, -1:, :]
    q_dec = q * jnp.exp(cum)
    k_dec = k * jnp.exp(-cum)
    mask = jnp.tril(jnp.ones((C, C), dtype=bool))
    scores = jnp.where(mask, jnp.einsum('bhncd,bhnsd->bhncs', q_dec, k_dec), 0.0)
    o = jnp.einsum('bhncs,bhnsv->bhncv', scores, v)
    chunk_state = jnp.einsum('bhncd,bhncv->bhndv', k * jnp.exp(last - cum), v)
    chunk_decay = jnp.exp(last[:, :, :, 0, :])

    def step(S, inp):
        dec, cs = inp
        return S * dec[..., None] + cs, S

    _, S_prev = lax.scan(step, jnp.zeros((Bs, H, dk, dv), q.dtype),
                         (jnp.moveaxis(chunk_decay, 2, 0), jnp.moveaxis(chunk_state, 2, 0)))
    S_prev = jnp.moveaxis(S_prev, 0, 2)
    o = o + jnp.einsum('bhncd,bhndv->bhncv', q_dec, S_prev)
    return o.transpose(0, 2, 3, 1, 4).reshape(Bs, T, H, dv)


def rwkv7_scan(r, w, k, v, kk, a):
    Bs, L, H, N = r.shape

    def step(S, inp):
        r_t, w_t, k_t, v_t, kk_t, a_t = inp
        sa = jnp.einsum('bhvk,bhk->bhv', S, -kk_t)
        S = S * w_t[:, :, None, :] + sa[..., None] * (kk_t * a_t)[:, :, None, :] \
            + v_t[..., None] * k_t[:, :, None, :]
        return S, jnp.einsum('bhvk,bhk->bhv', S, r_t)

    xs = tuple(jnp.moveaxis(t, 1, 0) for t in (r, w, k, v, kk, a))
    _, y = lax.scan(step, jnp.zeros((Bs, H, N, N), r.dtype), xs)
    return jnp.moveaxis(y, 0, 1)


def mixer_cd(h, w_in, gla_w2, gla_b, gla_norm_g, rw_mu, rw_w0, rw_w2, rw_a0, rw_a2, rw_g2,
             rw_kk, rw_ka, rw_rk, rw_ln_g, rw_ln_b, w_out):
    dt = h.dtype
    Bs, L, _ = h.shape
    z = h @ w_in
    z_gla, z_rw = z[..., :GLA_COLS], z[..., GLA_COLS:]

    q, k, v, go, glr = jnp.split(z_gla.astype(jnp.float32), _splits(GLA_SIZES), axis=-1)
    log_a = jax.nn.log_sigmoid(glr @ gla_w2.astype(jnp.float32) + gla_b) / GLA_GATE_NORM
    q = q * (GLA_DK ** -0.5)
    heads = lambda t, d: t.reshape(Bs, L, GLA_HEADS, d)
    q, k, log_a, v = heads(q, GLA_DK), heads(k, GLA_DK), heads(log_a, GLA_DK), heads(v, GLA_DV)
    pad_front = (-N_META) % GLA_CHUNK
    pad_back = (-(pad_front + L)) % GLA_CHUNK
    pw = ((0, 0), (pad_front, pad_back), (0, 0), (0, 0))
    o = gla_chunked(jnp.pad(q, pw), jnp.pad(k, pw), jnp.pad(v, pw), jnp.pad(log_a, pw))
    o = o[:, pad_front:pad_front + L]
    o = o * lax.rsqrt(jnp.mean(o * o, axis=-1, keepdims=True) + EPS)
    o = o.reshape(Bs, L, GLA_V) * gla_norm_g * jax.nn.silu(go)

    zr = z_rw.astype(jnp.float32)
    zr = zr + (token_shift(zr) - zr) * rw_mu
    r, kr, vr, xw, xa, xg = jnp.split(zr, _splits(RW_SIZES), axis=-1)
    w_log = -jax.nn.softplus(-(rw_w0 + jnp.tanh(xw) @ rw_w2.astype(jnp.float32))) - 0.5
    decay = jnp.exp(-jnp.exp(w_log))
    a = jax.nn.sigmoid(rw_a0 + xa @ rw_a2.astype(jnp.float32))
    g = jax.nn.sigmoid(xg) @ rw_g2.astype(jnp.float32)
    rh = lambda t: t.reshape(Bs, L, RW_HEADS, RW_N)
    r, kr, vr, decay, a = rh(r), rh(kr), rh(vr), rh(decay), rh(a)
    kk = kr * rw_kk.reshape(RW_HEADS, RW_N)
    kk = kk / jnp.maximum(jnp.sqrt(jnp.sum(kk * kk, axis=-1, keepdims=True)), 1e-12)
    kr = kr * (1.0 + (a - 1.0) * rw_ka.reshape(RW_HEADS, RW_N))
    y = rwkv7_scan(r, decay, kr, vr, kk, a)
    mu = jnp.mean(y, axis=-1, keepdims=True)
    var = jnp.mean(jnp.square(y - mu), axis=-1, keepdims=True)
    y = (y - mu) * lax.rsqrt(var + RW_GN_EPS) * rw_ln_g.reshape(RW_HEADS, RW_N) \
        + rw_ln_b.reshape(RW_HEADS, RW_N)
    y = y + jnp.sum(r * kr * rw_rk.reshape(RW_HEADS, RW_N), axis=-1, keepdims=True) * vr
    y = y.reshape(Bs, L, RW_W) * g

    return jnp.concatenate([o, y], axis=-1).astype(dt) @ w_out


def conv_ffn(h, w_up, dw, w_down):
    u = causal_dwconv(h @ w_up, dw)
    val, gate = u[..., :D_FF], u[..., D_FF:]
    return (jax.nn.silu(gate) * val) @ w_down


def setup_inputs(seed: int = 0) -> dict:
    key = jax.random.key(seed)
    ks = iter(jax.random.split(key, 48))
    nrm = lambda shape, scale: jax.random.normal(next(ks), shape, jnp.float32) * scale
    uni = lambda shape, lo, hi: jax.random.uniform(next(ks), shape, jnp.float32, lo, hi)
    gain = lambda shape: 1.0 + nrm(shape, 0.02)
    D = D_MODEL
    NE, NO = N_EVEN, N_ODD
    return {
        "x": nrm((BATCH, SEQ, D), 1.0),
        "meta": nrm((N_META, D), 1.0),
        "ab_w_in": nrm((NE, D, AB_IN), D ** -0.5),
        "ab_conf_dw": nrm((NE, CONF_K, CONF_W), CONF_K ** -0.5),
        "ab_conf_dw_b": nrm((NE, CONF_W), 0.02),
        "ab_conf_ln_g": gain((NE, CONF_W)),
        "ab_conf_ln_b": nrm((NE, CONF_W), 0.02),
        "ab_sc_dw": nrm((NE, SC_K, SC_W), SC_K ** -0.5),
        "ab_w_out": nrm((NE, CONF_W + SC_W, D), (CONF_W + SC_W) ** -0.5),
        "cd_w_in": nrm((NO, D, CD_IN), D ** -0.5),
        "cd_gla_w2": nrm((NO, GLA_RANK, GLA_QK), GLA_RANK ** -0.5),
        "cd_gla_b": nrm((NO, GLA_QK), 0.1),
        "cd_gla_norm_g": gain((NO, GLA_V)),
        "cd_rw_mu": uni((NO, RW_COLS), 0.0, 1.0),
        "cd_rw_w0": uni((NO, RW_W), -6.0, -1.0),
        "cd_rw_w2": nrm((NO, RW_DECAY_RANK, RW_W), RW_DECAY_RANK ** -0.5),
        "cd_rw_a0": nrm((NO, RW_W), 0.1),
        "cd_rw_a2": nrm((NO, RW_A_RANK, RW_W), RW_A_RANK ** -0.5),
        "cd_rw_g2": nrm((NO, RW_G_RANK, RW_W), RW_G_RANK ** -0.5),
        "cd_rw_kk": 0.85 + nrm((NO, RW_W), 0.05),
        "cd_rw_ka": 1.0 + nrm((NO, RW_W), 0.05),
        "cd_rw_rk": nrm((NO, RW_W), 0.1),
        "cd_rw_ln_g": gain((NO, RW_W)),
        "cd_rw_ln_b": nrm((NO, RW_W), 0.02),
        "cd_w_out": nrm((NO, GLA_V + RW_W, D), (GLA_V + RW_W) ** -0.5),
        "norm_mix": gain((DEPTH, D)),
        "norm_ffn": gain((DEPTH, D)),
        "ffn_w_up": nrm((DEPTH, D, 2 * D_FF), D ** -0.5),
        "ffn_dw": nrm((DEPTH, FFN_K, 2 * D_FF), FFN_K ** -0.5),
        "ffn_w_down": nrm((DEPTH, D_FF, D), D_FF ** -0.5),
        "norm_final": gain((D,)),
    }


def reference(x, meta, ab_w_in, ab_conf_dw, ab_conf_dw_b, ab_conf_ln_g, ab_conf_ln_b, ab_sc_dw,
              ab_w_out, cd_w_in, cd_gla_w2, cd_gla_b, cd_gla_norm_g, cd_rw_mu, cd_rw_w0, cd_rw_w2,
              cd_rw_a0, cd_rw_a2, cd_rw_g2, cd_rw_kk, cd_rw_ka, cd_rw_rk, cd_rw_ln_g, cd_rw_ln_b,
              cd_w_out, norm_mix, norm_ffn, ffn_w_up, ffn_dw, ffn_w_down, norm_final):
    Bs = x.shape[0]
    h = jnp.concatenate([jnp.broadcast_to(meta[None].astype(x.dtype), (Bs, N_META, D_MODEL)), x],
                        axis=1)
    for i in range(DEPTH):
        hn = rmsnorm(h, norm_mix[i])
        j = i // 2
        if i % 2 == 0:
            mix = mixer_ab(hn, ab_w_in[j], ab_conf_dw[j], ab_conf_dw_b[j], ab_conf_ln_g[j],
                           ab_conf_ln_b[j], ab_sc_dw[j], ab_w_out[j])
        else:
            mix = mixer_cd(hn, cd_w_in[j], cd_gla_w2[j], cd_gla_b[j], cd_gla_norm_g[j], cd_rw_mu[j],
                           cd_rw_w0[j], cd_rw_w2[j], cd_rw_a0[j], cd_rw_a2[j], cd_rw_g2[j],
                           cd_rw_kk[j], cd_rw_ka[j], cd_rw_rk[j], cd_rw_ln_g[j], cd_rw_ln_b[j],
                           cd_w_out[j])
        h = h + mix.astype(h.dtype)
        h = h + conv_ffn(rmsnorm(h, norm_ffn[i]), ffn_w_up[i], ffn_dw[i], ffn_w_down[i]).astype(h.dtype)
    return rmsnorm(h, norm_final)[:, N_META:]
```

```python
import functools

import jax
import jax.numpy as jnp
from jax import lax
from jax.experimental import pallas as pl
from jax.experimental.pallas import tpu as pltpu

D_MODEL = 2048
BATCH = 4
SEQ = 2048
N_META = 16
CONF_W = 1024
CONF_K = 31
SC_W = 1024
SC_K = 3
AB_IN = 5 * 1024
GLA_HEADS = 4
GLA_DK = 128
GLA_DV = 256
GLA_QK = GLA_HEADS * GLA_DK
GLA_V = GLA_HEADS * GLA_DV
GLA_RANK = 16
GLA_GATE_NORM = 16.0
GLA_CHUNK = 64
GLA_COLS = 2 * GLA_QK + 2 * GLA_V + GLA_RANK
RW_HEADS = 16
RW_N = 64
RW_W = RW_HEADS * RW_N
RW_LORA = 64 + 64 + 128
RW_COLS = 3 * RW_W + RW_LORA
RW_GN_EPS = 64e-5
D_FF = 5632
EPS = 1e-6
LN_EPS = 1e-5

PAD = (-N_META) % GLA_CHUNK
LP = PAD + N_META + SEQ
MT = BATCH * LP
FIRST = PAD
TM = LP // 3
TS = LP // 6
LANES = 128
GLR_PAD = LANES
CD_COLS = RW_COLS + 2 * GLA_QK + 2 * GLA_V + GLR_PAD
TN_AB = 1280
TN_CD = 2176
TN_FF = 512
HALO_FF = 16
HALO_CONF = 32
HALO_SC = 8
RW_TB = 64
VMEM_LIMIT = 56 * 1024 * 1024

F32 = jnp.float32
BF16 = jnp.bfloat16


def _params(sem):
    return pltpu.CompilerParams(dimension_semantics=sem, vmem_limit_bytes=VMEM_LIMIT)


def _sigmoid(x):
    return 1.0 / (1.0 + jnp.exp(-x))


def _silu(x):
    return x * _sigmoid(x)


def _softplus(x):
    return jnp.maximum(x, 0.0) + jnp.log(1.0 + jnp.exp(-jnp.abs(x)))


def _rms_rows(x, g):
    ms = jnp.mean(x * x, axis=-1, keepdims=True)
    return x * lax.rsqrt(ms + EPS) * g


NORM_CHUNK = 176


def _norm_rows_into(dst_ref, dst_row0, src_ref, g_ref):
    def body(c, carry):
        r = pl.multiple_of(c * NORM_CHUNK, 16)
        x = src_ref[pl.ds(r, NORM_CHUNK), :]
        dst_ref[pl.ds(pl.multiple_of(dst_row0 + r, 16), NORM_CHUNK), :] = _rms_rows(x, g_ref[...]).astype(BF16)
        return carry

    lax.fori_loop(0, src_ref.shape[0] // NORM_CHUNK, body, 0)


def _norm_matmul_kernel(h_ref, g_ref, w_ref, o_ref, hn_ref):
    @pl.when(pl.program_id(1) == 0)
    def _():
        _norm_rows_into(hn_ref, 0, h_ref, g_ref)

    o_ref[...] = jnp.dot(hn_ref[...], w_ref[...], preferred_element_type=F32)


def norm_matmul(h, g, w, tn):
    n = w.shape[1]
    return pl.pallas_call(
        _norm_matmul_kernel,
        grid=(MT // TM, n // tn),
        in_specs=[
            pl.BlockSpec((TM, D_MODEL), lambda i, j: (i, 0)),
            pl.BlockSpec((1, D_MODEL), lambda i, j: (0, 0)),
            pl.BlockSpec((D_MODEL, tn), lambda i, j: (0, j)),
        ],
        out_specs=pl.BlockSpec((TM, tn), lambda i, j: (i, j)),
        out_shape=jax.ShapeDtypeStruct((MT, n), F32),
        scratch_shapes=[pltpu.VMEM((TM, D_MODEL), BF16)],
        compiler_params=_params(("parallel", "arbitrary")),
        name="norm_matmul",
    )(h, g, w)


def _zero_pad_rows(o_ref, tiles_per_batch):
    @pl.when(pl.program_id(0) % tiles_per_batch == 0)
    def _():
        o_ref[0:FIRST, :] = jnp.zeros((FIRST, o_ref.shape[1]), o_ref.dtype)


def _out_proj_kernel(h_ref, a1_ref, a2_ref, w1_ref, w2_ref, o_ref):
    acc = jnp.dot(a1_ref[...], w1_ref[...], preferred_element_type=F32)
    acc = acc + jnp.dot(a2_ref[...], w2_ref[...], preferred_element_type=F32)
    o_ref[...] = h_ref[...] + acc
    _zero_pad_rows(o_ref, LP // TS)


def out_proj(h, a1, a2, w1, w2):
    k1, k2 = a1.shape[1], a2.shape[1]
    return pl.pallas_call(
        _out_proj_kernel,
        grid=(MT // TS,),
        in_specs=[
            pl.BlockSpec((TS, D_MODEL), lambda i: (i, 0)),
            pl.BlockSpec((TS, k1), lambda i: (i, 0)),
            pl.BlockSpec((TS, k2), lambda i: (i, 0)),
            pl.BlockSpec((k1, D_MODEL), lambda i: (0, 0)),
            pl.BlockSpec((k2, D_MODEL), lambda i: (0, 0)),
        ],
        out_specs=pl.BlockSpec((TS, D_MODEL), lambda i: (i, 0)),
        out_shape=jax.ShapeDtypeStruct((MT, D_MODEL), F32),
        compiler_params=_params(("parallel",)),
        name="out_proj",
    )(h, a1, a2, w1, w2)


def _ffn_kernel(h_ref, halo_ref, g_ref, wv_ref, wg_ref, dwv_ref, dwg_ref, wd_ref, gf_ref,
                o_ref, hn_ref, *, final_norm):
    j = pl.program_id(1)

    @pl.when(j == 0)
    def _():
        hn_ref[0:HALO_FF, :] = _rms_rows(halo_ref[...], g_ref[...]).astype(BF16)
        _norm_rows_into(hn_ref, HALO_FF, h_ref, g_ref)
        o_ref[...] = h_ref[...]

    hn = hn_ref[...]

    def conv(w_ref, dw_ref):
        u = jnp.dot(hn, w_ref[...], preferred_element_type=F32)
        u1 = pltpu.roll(u, 1, 0)
        u2 = pltpu.roll(u, 2, 0)
        c = u * dw_ref[2:3, :] + u1 * dw_ref[1:2, :] + u2 * dw_ref[0:1, :]
        return c[HALO_FF:, :]

    act = (_silu(conv(wg_ref, dwg_ref)) * conv(wv_ref, dwv_ref)).astype(BF16)
    for n in range(D_MODEL // TN_FF):
        cols = slice(n * TN_FF, (n + 1) * TN_FF)
        o_ref[:, cols] += jnp.dot(act, wd_ref[:, cols], preferred_element_type=F32)

    @pl.when(j == pl.num_programs(1) - 1)
    def _():
        if final_norm:
            def body(c, carry):
                rows = pl.ds(pl.multiple_of(c * NORM_CHUNK, 16), NORM_CHUNK)
                o_ref[rows, :] = _rms_rows(o_ref[rows, :], gf_ref[...])
                return carry

            lax.fori_loop(0, TM // NORM_CHUNK, body, 0)
        _zero_pad_rows(o_ref, LP // TM)


def conv_ffn(h, g, w_up, dw, w_down, g_final, final_norm):
    nj = D_FF // TN_FF
    halo_blocks = TM // HALO_FF
    return pl.pallas_call(
        functools.partial(_ffn_kernel, final_norm=final_norm),
        grid=(MT // TM, nj),
        in_specs=[
            pl.BlockSpec((TM, D_MODEL), lambda i, j: (i, 0)),
            pl.BlockSpec((HALO_FF, D_MODEL), lambda i, j: (jnp.maximum(i * halo_blocks - 1, 0), 0)),
            pl.BlockSpec((1, D_MODEL), lambda i, j: (0, 0)),
            pl.BlockSpec((D_MODEL, TN_FF), lambda i, j: (0, j)),
            pl.BlockSpec((D_MODEL, TN_FF), lambda i, j: (0, nj + j)),
            pl.BlockSpec((3, TN_FF), lambda i, j: (0, j)),
            pl.BlockSpec((3, TN_FF), lambda i, j: (0, nj + j)),
            pl.BlockSpec((TN_FF, D_MODEL), lambda i, j: (j, 0)),
            pl.BlockSpec((1, D_MODEL), lambda i, j: (0, 0)),
        ],
        out_specs=pl.BlockSpec((TM, D_MODEL), lambda i, j: (i, 0)),
        out_shape=jax.ShapeDtypeStruct((MT, D_MODEL), F32),
        scratch_shapes=[pltpu.VMEM((HALO_FF + TM, D_MODEL), BF16)],
        compiler_params=_params(("parallel", "arbitrary")),
        name="conv_ffn",
    )(h, h, g, w_up, w_up, dw, dw, w_down, g_final)


CONF_RB = 88


def _mix_ab_kernel(av_ref, ag_ref, sb_ref, sc_ref, sx_ref, hav_ref, hag_ref, hsc_ref, hsx_ref,
                   cw_ref, cb_ref, lg_ref, lb_ref, sw_ref, a_ref, s_ref, xs_ref, ac_ref, cx_ref):
    xs_ref[0:HALO_CONF, :] = hav_ref[...] * _sigmoid(hag_ref[...])
    xs_ref[HALO_CONF:, :] = av_ref[...] * _sigmoid(ag_ref[...])
    base = HALO_CONF - (CONF_K - 1)

    def lane_chunk(c, carry):
        cols = pl.ds(pl.multiple_of(c * LANES, LANES), LANES)
        for rb in range(TS // CONF_RB):
            r0 = rb * CONF_RB
            acc = jnp.broadcast_to(cb_ref[:, cols], (CONF_RB, LANES))
            for t in range(CONF_K):
                acc = acc + xs_ref[pl.ds(r0 + base + t, CONF_RB), cols] * cw_ref[pl.ds(t, 1), cols]
            ac_ref[pl.ds(r0, CONF_RB), cols] = acc
        return carry

    lax.fori_loop(0, CONF_W // LANES, lane_chunk, 0)
    a = ac_ref[...]
    mu = jnp.mean(a, axis=-1, keepdims=True)
    d = a - mu
    var = jnp.mean(d * d, axis=-1, keepdims=True)
    y = d * lax.rsqrt(var + LN_EPS) * lg_ref[...] + lb_ref[...]
    a_ref[...] = _silu(y).astype(BF16)

    cx_ref[0:HALO_SC, :] = hsc_ref[...] * hsx_ref[...]
    cx_ref[HALO_SC:, :] = sc_ref[...] * sx_ref[...]
    cv = cx_ref[pl.ds(HALO_SC - 2, TS), :] * sw_ref[0:1, :]
    cv = cv + cx_ref[pl.ds(HALO_SC - 1, TS), :] * sw_ref[1:2, :]
    cv = cv + cx_ref[pl.ds(HALO_SC, TS), :] * sw_ref[2:3, :]
    s_ref[...] = (sb_ref[...] * cv).astype(BF16)


def mix_ab(z, conf_dw, conf_b, ln_g, ln_b, sc_dw):
    w = CONF_W
    hc, hs = TS // HALO_CONF, TS // HALO_SC

    def col(k):
        return pl.BlockSpec((TS, w), lambda i, k=k: (i, k))

    def halo(rows, per_tile, k):
        return pl.BlockSpec((rows, w), lambda i, k=k: (jnp.maximum(i * per_tile - 1, 0), k))

    def full(r):
        return pl.BlockSpec((r, w), lambda i: (0, 0))

    return pl.pallas_call(
        _mix_ab_kernel,
        grid=(MT // TS,),
        in_specs=[col(0), col(1), col(2), col(3), col(4),
                  halo(HALO_CONF, hc, 0), halo(HALO_CONF, hc, 1),
                  halo(HALO_SC, hs, 3), halo(HALO_SC, hs, 4),
                  full(CONF_K), full(1), full(1), full(1), full(SC_K)],
        out_specs=[pl.BlockSpec((TS, w), lambda i: (i, 0)), pl.BlockSpec((TS, w), lambda i: (i, 0))],
        out_shape=[jax.ShapeDtypeStruct((MT, w), BF16), jax.ShapeDtypeStruct((MT, w), BF16)],
        scratch_shapes=[pltpu.VMEM((HALO_CONF + TS, w), F32), pltpu.VMEM((TS, w), F32),
                        pltpu.VMEM((HALO_SC + TS, w), F32)],
        compiler_params=_params(("parallel",)),
        name="mix_ab",
    )(z, z, z, z, z, z, z, z, z, conf_dw, conf_b, ln_g, ln_b, sc_dw)


def _gla_kernel(q_ref, k_ref, v_ref, go_ref, glr_ref, w2_ref, b_ref, gn_ref, o_ref, st_ref):
    c_len = GLA_CHUNK
    st_ref[...] = jnp.zeros(st_ref.shape, F32)
    ri = lax.broadcasted_iota(jnp.int32, (c_len, c_len), 0)
    ci = lax.broadcasted_iota(jnp.int32, (c_len, c_len), 1)
    tril = ri >= ci
    tril_f = tril.astype(F32)
    row = lax.broadcasted_iota(jnp.int32, (c_len, 1), 0)

    def chunk(c, carry):
        rows = pl.ds(pl.multiple_of(c * c_len, c_len), c_len)
        q = q_ref[rows, :] * (GLA_DK ** -0.5)
        k = k_ref[rows, :]
        v = v_ref[rows, :]
        lin = jnp.dot(glr_ref[rows, :], w2_ref[...], preferred_element_type=F32) + b_ref[...]
        log_a = -_softplus(-lin) / GLA_GATE_NORM
        log_a = jnp.where(c * c_len + row >= FIRST, log_a, 0.0)
        cum = jnp.dot(tril_f, log_a, preferred_element_type=F32, precision=lax.Precision.HIGHEST)
        last = cum[c_len - 1:c_len, :]
        q_dec = q * jnp.exp(cum)
        k_dec = k * jnp.exp(-cum)
        k_st = k * jnp.exp(last - cum)
        scores = lax.dot_general(q_dec, k_dec, (((1,), (1,)), ((), ())), preferred_element_type=F32)
        scores = jnp.where(tril, scores, 0.0)
        st = st_ref[...]
        o = jnp.dot(scores, v, preferred_element_type=F32)
        o = o + lax.dot_general(q_dec, st, (((1,), (1,)), ((), ())), preferred_element_type=F32)
        st_ref[...] = st * jnp.exp(last) + jnp.dot(v.T, k_st, preferred_element_type=F32)
        o = o * lax.rsqrt(jnp.mean(o * o, axis=-1, keepdims=True) + EPS)
        o_ref[rows, :] = (o * gn_ref[...] * _silu(go_ref[rows, :])).astype(BF16)
        return carry

    lax.fori_loop(0, LP // c_len, chunk, 0)


def gla(z, w2p, b, gn):
    qk0 = RW_COLS // GLA_DK
    v0 = (RW_COLS + 2 * GLA_QK) // GLA_DV
    go0 = v0 + GLA_HEADS
    glr0 = (RW_COLS + 2 * GLA_QK + 2 * GLA_V) // GLR_PAD
    z3 = z.reshape(BATCH, LP, CD_COLS)
    return pl.pallas_call(
        _gla_kernel,
        grid=(BATCH, GLA_HEADS),
        in_specs=[
            pl.BlockSpec((None, LP, GLA_DK), lambda b_, h: (b_, 0, qk0 + h)),
            pl.BlockSpec((None, LP, GLA_DK), lambda b_, h: (b_, 0, qk0 + GLA_HEADS + h)),
            pl.BlockSpec((None, LP, GLA_DV), lambda b_, h: (b_, 0, v0 + h)),
            pl.BlockSpec((None, LP, GLA_DV), lambda b_, h: (b_, 0, go0 + h)),
            pl.BlockSpec((None, LP, GLR_PAD), lambda b_, h: (b_, 0, glr0)),
            pl.BlockSpec((GLR_PAD, GLA_DK), lambda b_, h: (0, h)),
            pl.BlockSpec((1, GLA_DK), lambda b_, h: (0, h)),
            pl.BlockSpec((1, GLA_DV), lambda b_, h: (0, h)),
        ],
        out_specs=pl.BlockSpec((None, LP, GLA_DV), lambda b_, h: (b_, 0, h)),
        out_shape=jax.ShapeDtypeStruct((BATCH, LP, GLA_V), BF16),
        scratch_shapes=[pltpu.VMEM((GLA_DV, GLA_DK), F32)],
        compiler_params=_params(("parallel", "parallel")),
        name="gla",
    )(z3, z3, z3, z3, z3, w2p, b, gn).reshape(MT, GLA_V)


def _head_sum(x, ones_blk):
    parts = [jnp.dot(x[:, c * LANES:(c + 1) * LANES], ones_blk, preferred_element_type=F32,
                     precision=lax.Precision.HIGHEST) for c in range(RW_W // LANES)]
    return jnp.concatenate(parts, axis=-1)


def _head_ones():
    r = lax.broadcasted_iota(jnp.int32, (LANES, LANES), 0) // RW_N
    c = lax.broadcasted_iota(jnp.int32, (LANES, LANES), 1) // RW_N
    return (r == c).astype(F32)


def _rw_prep_kernel(r_ref, k_ref, v_ref, lo_ref, hr_ref, hk_ref, hv_ref, hlo_ref, mu_ref, mulo_ref,
                    w0_ref, w2_ref, a0_ref, a2_ref, g2_ref, kkp_ref, kap_ref,
                    ro_ref, wo_ref, ko_ref, vo_ref, kko_ref, bo_ref, go_ref):
    row = lax.broadcasted_iota(jnp.int32, (TS, 1), 0)

    def shift_mix(x_ref, halo_ref, mu):
        x = x_ref[...]
        prev = jnp.where(row == 0, halo_ref[HALO_SC - 1:HALO_SC, :], pltpu.roll(x, 1, 0))
        return x + (prev - x) * mu

    r = shift_mix(r_ref, hr_ref, mu_ref[0:1, :])
    k = shift_mix(k_ref, hk_ref, mu_ref[1:2, :])
    v = shift_mix(v_ref, hv_ref, mu_ref[2:3, :])
    lo = shift_mix(lo_ref, hlo_ref, mulo_ref[...])
    xw, xa, xg = lo[:, 0:64], lo[:, 64:128], lo[:, 128:256]
    w_log = -_softplus(-(w0_ref[...] + jnp.dot(jnp.tanh(xw), w2_ref[...], preferred_element_type=F32))) - 0.5
    decay = jnp.exp(-jnp.exp(w_log))
    a = _sigmoid(a0_ref[...] + jnp.dot(xa, a2_ref[...], preferred_element_type=F32))
    g = jnp.dot(_sigmoid(xg), g2_ref[...], preferred_element_type=F32)
    real = jnp.logical_or(pl.program_id(0) % (LP // TS) != 0, row >= FIRST)
    k = jnp.where(real, k, 0.0)
    v = jnp.where(real, v, 0.0)
    kk = k * kkp_ref[...]
    kk = kk / jnp.maximum(jnp.sqrt(_head_sum(kk * kk, _head_ones())), 1e-12)
    ro_ref[...] = r
    wo_ref[...] = decay
    ko_ref[...] = k * (1.0 + (a - 1.0) * kap_ref[...])
    vo_ref[...] = v
    kko_ref[...] = kk
    bo_ref[...] = kk * a
    go_ref[...] = g


def rw_prep(z, mu, w0, w2, a0, a2, g2, kkp, kap):
    w = RW_W
    hs = TS // HALO_SC
    lo0 = 3 * w // RW_LORA

    def col(k):
        return pl.BlockSpec((TS, w), lambda i, k=k: (i, k))

    def halo(k):
        return pl.BlockSpec((HALO_SC, w), lambda i, k=k: (jnp.maximum(i * hs - 1, 0), k))

    def full(shape):
        return pl.BlockSpec(shape, lambda i: (0, 0))

    out = pl.BlockSpec((TS, w), lambda i: (i, 0))
    return pl.pallas_call(
        _rw_prep_kernel,
        grid=(MT // TS,),
        in_specs=[col(0), col(1), col(2),
                  pl.BlockSpec((TS, RW_LORA), lambda i: (i, lo0)),
                  halo(0), halo(1), halo(2),
                  pl.BlockSpec((HALO_SC, RW_LORA), lambda i: (jnp.maximum(i * hs - 1, 0), lo0)),
                  full((3, w)), full((1, RW_LORA)),
                  full((1, w)), full((64, w)), full((1, w)), full((64, w)), full((128, w)),
                  full((1, w)), full((1, w))],
        out_specs=[out] * 7,
        out_shape=[jax.ShapeDtypeStruct((MT, w), F32)] * 7,
        compiler_params=_params(("parallel",)),
        name="rw_prep",
    )(z, z, z, z, z, z, z, z, mu[:, :3 * w].reshape(3, w), mu[:, 3 * w:], w0, w2, a0, a2, g2, kkp, kap)


RW_PAIRS = RW_HEADS // 2
RW_UNROLL = 8


def _rw_scan_kernel(r_ref, w_ref, k_ref, v_ref, kk_ref, b_ref, y_ref, s_ref, yt_ref):
    @pl.when(pl.program_id(0) == 0)
    def _():
        s_ref[...] = jnp.zeros(s_ref.shape, F32)

    yt_ref[...] = jnp.zeros(yt_ref.shape, F32)
    ones_blk = _head_ones().astype(BF16)
    vi = lax.broadcasted_iota(jnp.int32, (RW_N, LANES), 0)
    li = lax.broadcasted_iota(jnp.int32, (RW_N, LANES), 1)
    diag = (li % RW_N == vi).astype(F32)
    lane_t = li % RW_N

    def bsum(x):
        return jnp.dot(x.astype(BF16), ones_blk, preferred_element_type=F32)

    def steps(g, carry):
        t0 = pl.multiple_of(g * RW_UNROLL, RW_UNROLL)
        for j in range(RW_UNROLL):
            hit = lane_t == t0 + j
            for b in range(BATCH):
                for p in range(RW_PAIRS):
                    cols = slice(p * LANES, (p + 1) * LANES)

                    def rowv(ref):
                        return ref[b, pl.ds(t0, RW_UNROLL), cols][j:j + 1, :]

                    s = s_ref[b, p]
                    sa = -bsum(s * rowv(kk_ref))
                    vb = bsum(diag * rowv(v_ref))
                    s = s * rowv(w_ref) + sa * rowv(b_ref) + vb * rowv(k_ref)
                    s_ref[b, p] = s
                    yb = bsum(s * rowv(r_ref))
                    yt_ref[b, p] = jnp.where(hit, yb, yt_ref[b, p])
        return carry

    lax.fori_loop(0, RW_TB // RW_UNROLL, steps, 0)

    zeros = jnp.zeros((LANES - RW_N, LANES), F32)
    for b in range(BATCH):
        for p in range(RW_PAIRS):
            yt = jnp.concatenate([yt_ref[b, p], zeros], axis=0).T
            lo = yt[0:RW_TB, :]
            hi = pltpu.roll(yt[RW_N:RW_N + RW_TB, :], RW_N, 1)
            lane = lax.broadcasted_iota(jnp.int32, (RW_TB, LANES), 1)
            y_ref[b, :, p * LANES:(p + 1) * LANES] = jnp.where(lane < RW_N, lo, hi)


def rw_scan(r, w, k, v, kk, bb):
    spec = pl.BlockSpec((BATCH, RW_TB, RW_W), lambda t: (0, t, 0))
    args = [x.reshape(BATCH, LP, RW_W) for x in (r, w, k, v, kk, bb)]
    return pl.pallas_call(
        _rw_scan_kernel,
        grid=(LP // RW_TB,),
        in_specs=[spec] * 6,
        out_specs=spec,
        out_shape=jax.ShapeDtypeStruct((BATCH, LP, RW_W), F32),
        scratch_shapes=[pltpu.VMEM((BATCH, RW_PAIRS, RW_N, LANES), F32),
                        pltpu.VMEM((BATCH, RW_PAIRS, RW_N, LANES), F32)],
        compiler_params=_params(("arbitrary",)),
        name="rw_scan",
    )(*args).reshape(MT, RW_W)


def _rw_post_kernel(y_ref, r_ref, k_ref, v_ref, g_ref, lg_ref, lb_ref, rk_ref, o_ref):
    ones_blk = _head_ones()
    y = y_ref[...]
    mu = _head_sum(y, ones_blk) * (1.0 / RW_N)
    d = y - mu
    var = _head_sum(d * d, ones_blk) * (1.0 / RW_N)
    yn = d * lax.rsqrt(var + RW_GN_EPS) * lg_ref[...] + lb_ref[...]
    bonus = _head_sum(r_ref[...] * k_ref[...] * rk_ref[...], ones_blk) * v_ref[...]
    o_ref[...] = ((yn + bonus) * g_ref[...]).astype(BF16)


def rw_post(y, r, k, v, g, lg, lb, rk):
    blk = pl.BlockSpec((TS, RW_W), lambda i: (i, 0))
    par = pl.BlockSpec((1, RW_W), lambda i: (0, 0))
    return pl.pallas_call(
        _rw_post_kernel,
        grid=(MT // TS,),
        in_specs=[blk] * 5 + [par] * 3,
        out_specs=blk,
        out_shape=jax.ShapeDtypeStruct((MT, RW_W), BF16),
        compiler_params=_params(("parallel",)),
        name="rw_post",
    )(y, r, k, v, g, lg, lb, rk)


def kernel(x, meta, ab_w_in, ab_conf_dw, ab_conf_dw_b, ab_conf_ln_g, ab_conf_ln_b, ab_sc_dw, ab_w_out,
           cd_w_in, cd_gla_w2, cd_gla_b, cd_gla_norm_g, cd_rw_mu, cd_rw_w0, cd_rw_w2, cd_rw_a0, cd_rw_a2,
           cd_rw_g2, cd_rw_kk, cd_rw_ka, cd_rw_rk, cd_rw_ln_g, cd_rw_ln_b, cd_w_out, norm_mix, norm_ffn,
           ffn_w_up, ffn_dw, ffn_w_down, norm_final):
    bs = x.shape[0]
    h = jnp.concatenate([jnp.zeros((bs, PAD, D_MODEL), x.dtype),
                         jnp.broadcast_to(meta[None].astype(x.dtype), (bs, N_META, D_MODEL)), x], axis=1)
    h = h.reshape(MT, D_MODEL)
    row2 = lambda p: p.reshape(1, -1)
    g_final = row2(norm_final)

    z = norm_matmul(h, row2(norm_mix[0]), ab_w_in[0].astype(BF16), TN_AB)
    a_act, s_act = mix_ab(z, ab_conf_dw[0], row2(ab_conf_dw_b[0]), row2(ab_conf_ln_g[0]),
                          row2(ab_conf_ln_b[0]), ab_sc_dw[0])
    w_out = ab_w_out[0].astype(BF16)
    h = out_proj(h, a_act, s_act, w_out[:CONF_W], w_out[CONF_W:])
    h = conv_ffn(h, row2(norm_ffn[0]), ffn_w_up[0].astype(BF16), ffn_dw[0], ffn_w_down[0].astype(BF16),
                 g_final, False)

    w_in = cd_w_in[0]
    w_in = jnp.concatenate([w_in[:, GLA_COLS:], w_in[:, :GLA_COLS],
                            jnp.zeros((D_MODEL, GLR_PAD - GLA_RANK), w_in.dtype)], axis=1).astype(BF16)
    z = norm_matmul(h, row2(norm_mix[1]), w_in, TN_CD)
    w2p = jnp.concatenate([cd_gla_w2[0], jnp.zeros((GLR_PAD - GLA_RANK, GLA_QK), F32)], axis=0)
    o_act = gla(z, w2p, row2(cd_gla_b[0]), row2(cd_gla_norm_g[0]))
    r, w, k, v, kk, bb, g = rw_prep(z, row2(cd_rw_mu[0]), row2(cd_rw_w0[0]), cd_rw_w2[0], row2(cd_rw_a0[0]),
                                    cd_rw_a2[0], cd_rw_g2[0], row2(cd_rw_kk[0]), row2(cd_rw_ka[0]))
    y = rw_scan(r, w, k, v, kk, bb)
    y_act = rw_post(y, r, k, v, g, row2(cd_rw_ln_g[0]), row2(cd_rw_ln_b[0]), row2(cd_rw_rk[0]))
    w_out = cd_w_out[0].astype(BF16)
    h = out_proj(h, o_act, y_act, w_out[:GLA_V], w_out[GLA_V:])
    h = conv_ffn(h, row2(norm_ffn[1]), ffn_w_up[1].astype(BF16), ffn_dw[1], ffn_w_down[1].astype(BF16),
                 g_final, True)
    return h.reshape(bs, LP, D_MODEL)[:, PAD + N_META:]
```

```python
import functools

import jax
import jax.numpy as jnp
from jax import lax
from jax.experimental import pallas as pl
from jax.experimental.pallas import tpu as pltpu

D_MODEL = 2048
BATCH = 4
SEQ = 2048
N_META = 16
CONF_W = 1024
CONF_K = 31
SC_W = 1024
SC_K = 3
AB_IN = 5 * 1024
GLA_HEADS = 4
GLA_DK = 128
GLA_DV = 256
GLA_QK = GLA_HEADS * GLA_DK
GLA_V = GLA_HEADS * GLA_DV
GLA_RANK = 16
GLA_GATE_NORM = 16.0
GLA_CHUNK = 64
GLA_COLS = 2 * GLA_QK + 2 * GLA_V + GLA_RANK
RW_HEADS = 16
RW_N = 64
RW_W = RW_HEADS * RW_N
RW_LORA = 64 + 64 + 128
RW_COLS = 3 * RW_W + RW_LORA
RW_GN_EPS = 64e-5
D_FF = 5632
EPS = 1e-6
LN_EPS = 1e-5

PAD = (-N_META) % GLA_CHUNK
LP = PAD + N_META + SEQ
MT = BATCH * LP
FIRST = PAD
TM = LP // 3
TS = LP // 6
LANES = 128
GLR_PAD = LANES
CD_COLS = RW_COLS + 2 * GLA_QK + 2 * GLA_V + GLR_PAD
TN_AB = 1280
TN_CD = 2176
TN_FF = 512
HALO_FF = 16
HALO_CONF = 32
HALO_SC = 8
RW_TB = 64
VMEM_LIMIT = 56 * 1024 * 1024

F32 = jnp.float32
BF16 = jnp.bfloat16


def _params(sem):
    return pltpu.CompilerParams(dimension_semantics=sem, vmem_limit_bytes=VMEM_LIMIT)


def _sigmoid(x):
    return 1.0 / (1.0 + jnp.exp(-x))


def _silu(x):
    return x * _sigmoid(x)


def _softplus(x):
    return jnp.maximum(x, 0.0) + jnp.log(1.0 + jnp.exp(-jnp.abs(x)))


def _rms_rows(x, g):
    ms = jnp.mean(x * x, axis=-1, keepdims=True)
    return x * lax.rsqrt(ms + EPS) * g


NORM_CHUNK = 176


def _norm_rows_into(dst_ref, dst_row0, src_ref, g_ref):
    def body(c, carry):
        r = pl.multiple_of(c * NORM_CHUNK, 16)
        x = src_ref[pl.ds(r, NORM_CHUNK), :]
        dst_ref[pl.ds(pl.multiple_of(dst_row0 + r, 16), NORM_CHUNK), :] = _rms_rows(x, g_ref[...]).astype(BF16)
        return carry

    lax.fori_loop(0, src_ref.shape[0] // NORM_CHUNK, body, 0)


def _norm_matmul_kernel(h_ref, g_ref, w_ref, o_ref, hn_ref):
    @pl.when(pl.program_id(1) == 0)
    def _():
        _norm_rows_into(hn_ref, 0, h_ref, g_ref)

    o_ref[...] = jnp.dot(hn_ref[...], w_ref[...], preferred_element_type=F32)


def norm_matmul(h, g, w, tn):
    n = w.shape[1]
    return pl.pallas_call(
        _norm_matmul_kernel,
        grid=(MT // TM, n // tn),
        in_specs=[
            pl.BlockSpec((TM, D_MODEL), lambda i, j: (i, 0)),
            pl.BlockSpec((1, D_MODEL), lambda i, j: (0, 0)),
            pl.BlockSpec((D_MODEL, tn), lambda i, j: (0, j)),
        ],
        out_specs=pl.BlockSpec((TM, tn), lambda i, j: (i, j)),
        out_shape=jax.ShapeDtypeStruct((MT, n), F32),
        scratch_shapes=[pltpu.VMEM((TM, D_MODEL), BF16)],
        compiler_params=_params(("parallel", "arbitrary")),
        name="norm_matmul",
    )(h, g, w)


def _zero_pad_rows(o_ref, tiles_per_batch):
    @pl.when(pl.program_id(0) % tiles_per_batch == 0)
    def _():
        o_ref[0:FIRST, :] = jnp.zeros((FIRST, o_ref.shape[1]), o_ref.dtype)


def _out_proj_kernel(h_ref, a1_ref, a2_ref, w1_ref, w2_ref, o_ref):
    acc = jnp.dot(a1_ref[...], w1_ref[...], preferred_element_type=F32)
    acc = acc + jnp.dot(a2_ref[...], w2_ref[...], preferred_element_type=F32)
    o_ref[...] = h_ref[...] + acc
    _zero_pad_rows(o_ref, LP // TS)


def out_proj(h, a1, a2, w):
    k1, k2 = a1.shape[1], a2.shape[1]
    assert k1 == k2 and w.shape[0] == k1 + k2
    w1 = w2 = w
    return pl.pallas_call(
        _out_proj_kernel,
        grid=(MT // TS,),
        in_specs=[
            pl.BlockSpec((TS, D_MODEL), lambda i: (i, 0)),
            pl.BlockSpec((TS, k1), lambda i: (i, 0)),
            pl.BlockSpec((TS, k2), lambda i: (i, 0)),
            pl.BlockSpec((k1, D_MODEL), lambda i: (0, 0)),
            pl.BlockSpec((k2, D_MODEL), lambda i: (1, 0)),
        ],
        out_specs=pl.BlockSpec((TS, D_MODEL), lambda i: (i, 0)),
        out_shape=jax.ShapeDtypeStruct((MT, D_MODEL), F32),
        compiler_params=_params(("parallel",)),
        name="out_proj",
    )(h, a1, a2, w1, w2)


def _ffn_kernel(h_ref, halo_ref, g_ref, wv_ref, wg_ref, dwv_ref, dwg_ref, wd_ref, gf_ref,
                o_ref, hn_ref, *, final_norm):
    j = pl.program_id(1)

    @pl.when(j == 0)
    def _():
        hn_ref[0:HALO_FF, :] = _rms_rows(halo_ref[...], g_ref[...]).astype(BF16)
        _norm_rows_into(hn_ref, HALO_FF, h_ref, g_ref)
        o_ref[...] = h_ref[...]

    hn = hn_ref[...]

    def conv(w_ref, dw_ref):
        u = jnp.dot(hn, w_ref[...], preferred_element_type=F32)
        u1 = pltpu.roll(u, 1, 0)
        u2 = pltpu.roll(u, 2, 0)
        c = u * dw_ref[2:3, :] + u1 * dw_ref[1:2, :] + u2 * dw_ref[0:1, :]
        return c[HALO_FF:, :]

    act = (_silu(conv(wg_ref, dwg_ref)) * conv(wv_ref, dwv_ref)).astype(BF16)
    for n in range(D_MODEL // TN_FF):
        cols = slice(n * TN_FF, (n + 1) * TN_FF)
        o_ref[:, cols] += jnp.dot(act, wd_ref[:, cols], preferred_element_type=F32)

    @pl.when(j == pl.num_programs(1) - 1)
    def _():
        if final_norm:
            def body(c, carry):
                rows = pl.ds(pl.multiple_of(c * NORM_CHUNK, 16), NORM_CHUNK)
                o_ref[rows, :] = _rms_rows(o_ref[rows, :], gf_ref[...])
                return carry

            lax.fori_loop(0, TM // NORM_CHUNK, body, 0)
        _zero_pad_rows(o_ref, LP // TM)


def conv_ffn(h, g, w_up, dw, w_down, g_final, final_norm):
    nj = D_FF // TN_FF
    halo_blocks = TM // HALO_FF
    return pl.pallas_call(
        functools.partial(_ffn_kernel, final_norm=final_norm),
        grid=(MT // TM, nj),
        in_specs=[
            pl.BlockSpec((TM, D_MODEL), lambda i, j: (i, 0)),
            pl.BlockSpec((HALO_FF, D_MODEL), lambda i, j: (jnp.maximum(i * halo_blocks - 1, 0), 0)),
            pl.BlockSpec((1, D_MODEL), lambda i, j: (0, 0)),
            pl.BlockSpec((D_MODEL, TN_FF), lambda i, j: (0, j)),
            pl.BlockSpec((D_MODEL, TN_FF), lambda i, j: (0, nj + j)),
            pl.BlockSpec((3, TN_FF), lambda i, j: (0, j)),
            pl.BlockSpec((3, TN_FF), lambda i, j: (0, nj + j)),
            pl.BlockSpec((TN_FF, D_MODEL), lambda i, j: (j, 0)),
            pl.BlockSpec((1, D_MODEL), lambda i, j: (0, 0)),
        ],
        out_specs=pl.BlockSpec((TM, D_MODEL), lambda i, j: (i, 0)),
        out_shape=jax.ShapeDtypeStruct((MT, D_MODEL), F32),
        scratch_shapes=[pltpu.VMEM((HALO_FF + TM, D_MODEL), BF16)],
        compiler_params=_params(("parallel", "arbitrary")),
        name="conv_ffn",
    )(h, h, g, w_up, w_up, dw, dw, w_down, g_final)


CONF_RB = 88


def _mix_ab_kernel(av_ref, ag_ref, sb_ref, sc_ref, sx_ref, hav_ref, hag_ref, hsc_ref, hsx_ref,
                   cw_ref, cb_ref, lg_ref, lb_ref, sw_ref, a_ref, s_ref, xs_ref, ac_ref, cx_ref):
    xs_ref[0:HALO_CONF, :] = hav_ref[...] * _sigmoid(hag_ref[...])
    xs_ref[HALO_CONF:, :] = av_ref[...] * _sigmoid(ag_ref[...])
    base = HALO_CONF - (CONF_K - 1)

    def lane_chunk(c, carry):
        cols = pl.ds(pl.multiple_of(c * LANES, LANES), LANES)
        for rb in range(TS // CONF_RB):
            r0 = rb * CONF_RB
            acc = jnp.broadcast_to(cb_ref[:, cols], (CONF_RB, LANES))
            for t in range(CONF_K):
                acc = acc + xs_ref[pl.ds(r0 + base + t, CONF_RB), cols] * cw_ref[pl.ds(t, 1), cols]
            ac_ref[pl.ds(r0, CONF_RB), cols] = acc
        return carry

    lax.fori_loop(0, CONF_W // LANES, lane_chunk, 0)
    a = ac_ref[...]
    mu = jnp.mean(a, axis=-1, keepdims=True)
    d = a - mu
    var = jnp.mean(d * d, axis=-1, keepdims=True)
    y = d * lax.rsqrt(var + LN_EPS) * lg_ref[...] + lb_ref[...]
    a_ref[...] = _silu(y).astype(BF16)

    cx_ref[0:HALO_SC, :] = hsc_ref[...] * hsx_ref[...]
    cx_ref[HALO_SC:, :] = sc_ref[...] * sx_ref[...]
    cv = cx_ref[pl.ds(HALO_SC - 2, TS), :] * sw_ref[0:1, :]
    cv = cv + cx_ref[pl.ds(HALO_SC - 1, TS), :] * sw_ref[1:2, :]
    cv = cv + cx_ref[pl.ds(HALO_SC, TS), :] * sw_ref[2:3, :]
    s_ref[...] = (sb_ref[...] * cv).astype(BF16)


def mix_ab(z, conf_dw, conf_b, ln_g, ln_b, sc_dw):
    w = CONF_W
    hc, hs = TS // HALO_CONF, TS // HALO_SC

    def col(k):
        return pl.BlockSpec((TS, w), lambda i, k=k: (i, k))

    def halo(rows, per_tile, k):
        return pl.BlockSpec((rows, w), lambda i, k=k: (jnp.maximum(i * per_tile - 1, 0), k))

    def full(r):
        return pl.BlockSpec((r, w), lambda i: (0, 0))

    return pl.pallas_call(
        _mix_ab_kernel,
        grid=(MT // TS,),
        in_specs=[col(0), col(1), col(2), col(3), col(4),
                  halo(HALO_CONF, hc, 0), halo(HALO_CONF, hc, 1),
                  halo(HALO_SC, hs, 3), halo(HALO_SC, hs, 4),
                  full(CONF_K), full(1), full(1), full(1), full(SC_K)],
        out_specs=[pl.BlockSpec((TS, w), lambda i: (i, 0)), pl.BlockSpec((TS, w), lambda i: (i, 0))],
        out_shape=[jax.ShapeDtypeStruct((MT, w), BF16), jax.ShapeDtypeStruct((MT, w), BF16)],
        scratch_shapes=[pltpu.VMEM((HALO_CONF + TS, w), F32), pltpu.VMEM((TS, w), F32),
                        pltpu.VMEM((HALO_SC + TS, w), F32)],
        compiler_params=_params(("parallel",)),
        name="mix_ab",
    )(z, z, z, z, z, z, z, z, z, conf_dw, conf_b, ln_g, ln_b, sc_dw)


def _gla_kernel(q_ref, k_ref, v_ref, go_ref, glr_ref, w2_ref, b_ref, gn_ref, o_ref, st_ref):
    c_len = GLA_CHUNK
    st_ref[...] = jnp.zeros(st_ref.shape, F32)
    ri = lax.broadcasted_iota(jnp.int32, (c_len, c_len), 0)
    ci = lax.broadcasted_iota(jnp.int32, (c_len, c_len), 1)
    tril = ri >= ci
    tril_f = tril.astype(F32)
    row = lax.broadcasted_iota(jnp.int32, (c_len, 1), 0)

    def chunk(c, carry):
        rows = pl.ds(pl.multiple_of(c * c_len, c_len), c_len)
        q = q_ref[rows, :] * (GLA_DK ** -0.5)
        k = k_ref[rows, :]
        v = v_ref[rows, :]
        lin = jnp.dot(glr_ref[rows, :], w2_ref[...], preferred_element_type=F32) + b_ref[...]
        log_a = -_softplus(-lin) / GLA_GATE_NORM
        log_a = jnp.where(c * c_len + row >= FIRST, log_a, 0.0)
        cum = jnp.dot(tril_f, log_a, preferred_element_type=F32, precision=lax.Precision.HIGHEST)
        last = cum[c_len - 1:c_len, :]
        q_dec = q * jnp.exp(cum)
        k_dec = k * jnp.exp(-cum)
        k_st = k * jnp.exp(last - cum)
        scores = lax.dot_general(q_dec, k_dec, (((1,), (1,)), ((), ())), preferred_element_type=F32)
        scores = jnp.where(tril, scores, 0.0)
        st = st_ref[...]
        o = jnp.dot(scores, v, preferred_element_type=F32)
        o = o + lax.dot_general(q_dec, st, (((1,), (1,)), ((), ())), preferred_element_type=F32)
        st_ref[...] = st * jnp.exp(last) + jnp.dot(v.T, k_st, preferred_element_type=F32)
        o = o * lax.rsqrt(jnp.mean(o * o, axis=-1, keepdims=True) + EPS)
        o_ref[rows, :] = (o * gn_ref[...] * _silu(go_ref[rows, :])).astype(BF16)
        return carry

    lax.fori_loop(0, LP // c_len, chunk, 0)


def gla(z, w2p, b, gn):
    qk0 = RW_COLS // GLA_DK
    v0 = (RW_COLS + 2 * GLA_QK) // GLA_DV
    go0 = v0 + GLA_HEADS
    glr0 = (RW_COLS + 2 * GLA_QK + 2 * GLA_V) // GLR_PAD
    z3 = z.reshape(BATCH, LP, CD_COLS)
    return pl.pallas_call(
        _gla_kernel,
        grid=(BATCH, GLA_HEADS),
        in_specs=[
            pl.BlockSpec((None, LP, GLA_DK), lambda b_, h: (b_, 0, qk0 + h)),
            pl.BlockSpec((None, LP, GLA_DK), lambda b_, h: (b_, 0, qk0 + GLA_HEADS + h)),
            pl.BlockSpec((None, LP, GLA_DV), lambda b_, h: (b_, 0, v0 + h)),
            pl.BlockSpec((None, LP, GLA_DV), lambda b_, h: (b_, 0, go0 + h)),
            pl.BlockSpec((None, LP, GLR_PAD), lambda b_, h: (b_, 0, glr0)),
            pl.BlockSpec((GLR_PAD, GLA_DK), lambda b_, h: (0, h)),
            pl.BlockSpec((1, GLA_DK), lambda b_, h: (0, h)),
            pl.BlockSpec((1, GLA_DV), lambda b_, h: (0, h)),
        ],
        out_specs=pl.BlockSpec((None, LP, GLA_DV), lambda b_, h: (b_, 0, h)),
        out_shape=jax.ShapeDtypeStruct((BATCH, LP, GLA_V), BF16),
        scratch_shapes=[pltpu.VMEM((GLA_DV, GLA_DK), F32)],
        compiler_params=_params(("parallel", "parallel")),
        name="gla",
    )(z3, z3, z3, z3, z3, w2p, b, gn).reshape(MT, GLA_V)


def _head_sum(x, ones_blk):
    parts = [jnp.dot(x[:, c * LANES:(c + 1) * LANES], ones_blk, preferred_element_type=F32,
                     precision=lax.Precision.HIGHEST) for c in range(RW_W // LANES)]
    return jnp.concatenate(parts, axis=-1)


def _head_ones():
    r = lax.broadcasted_iota(jnp.int32, (LANES, LANES), 0) // RW_N
    c = lax.broadcasted_iota(jnp.int32, (LANES, LANES), 1) // RW_N
    return (r == c).astype(F32)


def _rw_prep_kernel(r_ref, k_ref, v_ref, lo_ref, hr_ref, hk_ref, hv_ref, hlo_ref, mu_ref, mulo_ref,
                    w0_ref, w2_ref, a0_ref, a2_ref, g2_ref, kkp_ref, kap_ref,
                    ro_ref, wo_ref, ko_ref, vo_ref, kko_ref, bo_ref, go_ref):
    row = lax.broadcasted_iota(jnp.int32, (TS, 1), 0)

    def shift_mix(x_ref, halo_ref, mu):
        x = x_ref[...]
        prev = jnp.where(row == 0, halo_ref[HALO_SC - 1:HALO_SC, :], pltpu.roll(x, 1, 0))
        return x + (prev - x) * mu

    r = shift_mix(r_ref, hr_ref, mu_ref[0:1, :])
    k = shift_mix(k_ref, hk_ref, mu_ref[1:2, :])
    v = shift_mix(v_ref, hv_ref, mu_ref[2:3, :])
    lo = shift_mix(lo_ref, hlo_ref, mulo_ref[...])
    xw, xa, xg = lo[:, 0:64], lo[:, 64:128], lo[:, 128:256]
    w_log = -_softplus(-(w0_ref[...] + jnp.dot(jnp.tanh(xw), w2_ref[...], preferred_element_type=F32))) - 0.5
    decay = jnp.exp(-jnp.exp(w_log))
    a = _sigmoid(a0_ref[...] + jnp.dot(xa, a2_ref[...], preferred_element_type=F32))
    g = jnp.dot(_sigmoid(xg), g2_ref[...], preferred_element_type=F32)
    real = jnp.logical_or(pl.program_id(0) % (LP // TS) != 0, row >= FIRST)
    k = jnp.where(real, k, 0.0)
    v = jnp.where(real, v, 0.0)
    kk = k * kkp_ref[...]
    kk = kk / jnp.maximum(jnp.sqrt(_head_sum(kk * kk, _head_ones())), 1e-12)
    ro_ref[...] = r
    wo_ref[...] = decay
    ko_ref[...] = k * (1.0 + (a - 1.0) * kap_ref[...])
    vo_ref[...] = v
    kko_ref[...] = kk
    bo_ref[...] = kk * a
    go_ref[...] = g


def rw_prep(z, mu, w0, w2, a0, a2, g2, kkp, kap):
    w = RW_W
    hs = TS // HALO_SC
    lo0 = 3 * w // RW_LORA

    def col(k):
        return pl.BlockSpec((TS, w), lambda i, k=k: (i, k))

    def halo(k):
        return pl.BlockSpec((HALO_SC, w), lambda i, k=k: (jnp.maximum(i * hs - 1, 0), k))

    def full(shape):
        return pl.BlockSpec(shape, lambda i: (0, 0))

    out = pl.BlockSpec((TS, w), lambda i: (i, 0))
    return pl.pallas_call(
        _rw_prep_kernel,
        grid=(MT // TS,),
        in_specs=[col(0), col(1), col(2),
                  pl.BlockSpec((TS, RW_LORA), lambda i: (i, lo0)),
                  halo(0), halo(1), halo(2),
                  pl.BlockSpec((HALO_SC, RW_LORA), lambda i: (jnp.maximum(i * hs - 1, 0), lo0)),
                  full((3, w)), full((1, RW_LORA)),
                  full((1, w)), full((64, w)), full((1, w)), full((64, w)), full((128, w)),
                  full((1, w)), full((1, w))],
        out_specs=[out] * 7,
        out_shape=[jax.ShapeDtypeStruct((MT, w), F32)] * 7,
        compiler_params=_params(("parallel",)),
        name="rw_prep",
    )(z, z, z, z, z, z, z, z, mu[:, :3 * w].reshape(3, w), mu[:, 3 * w:], w0, w2, a0, a2, g2, kkp, kap)


RW_PAIRS = RW_HEADS // 2
RW_UNROLL = 8


def _rw_scan_kernel(r_ref, w_ref, k_ref, v_ref, kk_ref, b_ref, y_ref, s_ref, yt_ref):
    @pl.when(pl.program_id(0) == 0)
    def _():
        s_ref[...] = jnp.zeros(s_ref.shape, F32)

    yt_ref[...] = jnp.zeros(yt_ref.shape, F32)
    ones_blk = _head_ones().astype(BF16)
    vi = lax.broadcasted_iota(jnp.int32, (RW_N, LANES), 0)
    li = lax.broadcasted_iota(jnp.int32, (RW_N, LANES), 1)
    diag = (li % RW_N == vi).astype(F32)
    lane_t = li % RW_N

    pairs = [(b, p) for b in range(BATCH) for p in range(RW_PAIRS)]

    def bsum(parts):
        out = jnp.dot(jnp.concatenate(parts, axis=0), ones_blk, preferred_element_type=F32)
        return [out[i * RW_N:(i + 1) * RW_N, :] for i in range(len(parts))]

    def steps(g, carry):
        t0 = pl.multiple_of(g * RW_UNROLL, RW_UNROLL)
        for j in range(RW_UNROLL):
            hit = lane_t == t0 + j

            def rowv(ref, b, p):
                return ref[b, pl.ds(t0, RW_UNROLL), p * LANES:(p + 1) * LANES][j:j + 1, :]

            sa = bsum([(s_ref[b, p] * rowv(kk_ref, b, p)).astype(BF16) for b, p in pairs])
            vb = bsum([(diag * rowv(v_ref, b, p)).astype(BF16) for b, p in pairs])
            sr = []
            for i, (b, p) in enumerate(pairs):
                s = s_ref[b, p] * rowv(w_ref, b, p) - sa[i] * rowv(b_ref, b, p) + vb[i] * rowv(k_ref, b, p)
                s_ref[b, p] = s
                sr.append((s * rowv(r_ref, b, p)).astype(BF16))
            yb = bsum(sr)
            for i, (b, p) in enumerate(pairs):
                yt_ref[b, p] = jnp.where(hit, yb[i], yt_ref[b, p])
        return carry

    lax.fori_loop(0, RW_TB // RW_UNROLL, steps, 0)

    zeros = jnp.zeros((LANES - RW_N, LANES), F32)
    for b in range(BATCH):
        for p in range(RW_PAIRS):
            yt = jnp.concatenate([yt_ref[b, p], zeros], axis=0).T
            lo = yt[0:RW_TB, :]
            hi = pltpu.roll(yt[RW_N:RW_N + RW_TB, :], RW_N, 1)
            lane = lax.broadcasted_iota(jnp.int32, (RW_TB, LANES), 1)
            y_ref[b, :, p * LANES:(p + 1) * LANES] = jnp.where(lane < RW_N, lo, hi)


def rw_scan(r, w, k, v, kk, bb):
    spec = pl.BlockSpec((BATCH, RW_TB, RW_W), lambda t: (0, t, 0))
    args = [x.reshape(BATCH, LP, RW_W) for x in (r, w, k, v, kk, bb)]
    return pl.pallas_call(
        _rw_scan_kernel,
        grid=(LP // RW_TB,),
        in_specs=[spec] * 6,
        out_specs=spec,
        out_shape=jax.ShapeDtypeStruct((BATCH, LP, RW_W), F32),
        scratch_shapes=[pltpu.VMEM((BATCH, RW_PAIRS, RW_N, LANES), F32),
                        pltpu.VMEM((BATCH, RW_PAIRS, RW_N, LANES), F32)],
        compiler_params=_params(("arbitrary",)),
        name="rw_scan",
    )(*args).reshape(MT, RW_W)


def _rw_post_kernel(y_ref, r_ref, k_ref, v_ref, g_ref, lg_ref, lb_ref, rk_ref, o_ref):
    ones_blk = _head_ones()
    y = y_ref[...]
    mu = _head_sum(y, ones_blk) * (1.0 / RW_N)
    d = y - mu
    var = _head_sum(d * d, ones_blk) * (1.0 / RW_N)
    yn = d * lax.rsqrt(var + RW_GN_EPS) * lg_ref[...] + lb_ref[...]
    bonus = _head_sum(r_ref[...] * k_ref[...] * rk_ref[...], ones_blk) * v_ref[...]
    o_ref[...] = ((yn + bonus) * g_ref[...]).astype(BF16)


def rw_post(y, r, k, v, g, lg, lb, rk):
    blk = pl.BlockSpec((TS, RW_W), lambda i: (i, 0))
    par = pl.BlockSpec((1, RW_W), lambda i: (0, 0))
    return pl.pallas_call(
        _rw_post_kernel,
        grid=(MT // TS,),
        in_specs=[blk] * 5 + [par] * 3,
        out_specs=blk,
        out_shape=jax.ShapeDtypeStruct((MT, RW_W), BF16),
        compiler_params=_params(("parallel",)),
        name="rw_post",
    )(y, r, k, v, g, lg, lb, rk)


def kernel(x, meta, ab_w_in, ab_conf_dw, ab_conf_dw_b, ab_conf_ln_g, ab_conf_ln_b, ab_sc_dw, ab_w_out,
           cd_w_in, cd_gla_w2, cd_gla_b, cd_gla_norm_g, cd_rw_mu, cd_rw_w0, cd_rw_w2, cd_rw_a0, cd_rw_a2,
           cd_rw_g2, cd_rw_kk, cd_rw_ka, cd_rw_rk, cd_rw_ln_g, cd_rw_ln_b, cd_w_out, norm_mix, norm_ffn,
           ffn_w_up, ffn_dw, ffn_w_down, norm_final):
    bs = x.shape[0]
    h = jnp.concatenate([jnp.zeros((bs, PAD, D_MODEL), x.dtype),
                         jnp.broadcast_to(meta[None].astype(x.dtype), (bs, N_META, D_MODEL)), x], axis=1)
    h = h.reshape(MT, D_MODEL)
    row2 = lambda p: p.reshape(1, -1)
    g_final = row2(norm_final)

    z = norm_matmul(h, row2(norm_mix[0]), ab_w_in[0].astype(BF16), TN_AB)
    a_act, s_act = mix_ab(z, ab_conf_dw[0], row2(ab_conf_dw_b[0]), row2(ab_conf_ln_g[0]),
                          row2(ab_conf_ln_b[0]), ab_sc_dw[0])
    h = out_proj(h, a_act, s_act, ab_w_out[0].astype(BF16))
    h = conv_ffn(h, row2(norm_ffn[0]), ffn_w_up[0].astype(BF16), ffn_dw[0], ffn_w_down[0].astype(BF16),
                 g_final, False)

    w_in = cd_w_in[0]
    w_in = jnp.concatenate([w_in[:, GLA_COLS:], w_in[:, :GLA_COLS],
                            jnp.zeros((D_MODEL, GLR_PAD - GLA_RANK), w_in.dtype)], axis=1).astype(BF16)
    z = norm_matmul(h, row2(norm_mix[1]), w_in, TN_CD)
    w2p = jnp.concatenate([cd_gla_w2[0], jnp.zeros((GLR_PAD - GLA_RANK, GLA_QK), F32)], axis=0)
    o_act = gla(z, w2p, row2(cd_gla_b[0]), row2(cd_gla_norm_g[0]))
    r, w, k, v, kk, bb, g = rw_prep(z, row2(cd_rw_mu[0]), row2(cd_rw_w0[0]), cd_rw_w2[0], row2(cd_rw_a0[0]),
                                    cd_rw_a2[0], cd_rw_g2[0], row2(cd_rw_kk[0]), row2(cd_rw_ka[0]))
    y = rw_scan(r, w, k, v, kk, bb)
    y_act = rw_post(y, r, k, v, g, row2(cd_rw_ln_g[0]), row2(cd_rw_ln_b[0]), row2(cd_rw_rk[0]))
    h = out_proj(h, o_act, y_act, cd_w_out[0].astype(BF16))
    h = conv_ffn(h, row2(norm_ffn[1]), ffn_w_up[1].astype(BF16), ffn_dw[1], ffn_w_down[1].astype(BF16),
                 g_final, True)
    return h.reshape(bs, LP, D_MODEL)[:, PAD + N_META:]
```

```python
import functools

import jax
import jax.numpy as jnp
from jax import lax
from jax.experimental import pallas as pl
from jax.experimental.pallas import tpu as pltpu

D_MODEL = 2048
BATCH = 4
SEQ = 2048
N_META = 16
CONF_W = 1024
CONF_K = 31
SC_W = 1024
SC_K = 3
AB_IN = 5 * 1024
GLA_HEADS = 4
GLA_DK = 128
GLA_DV = 256
GLA_QK = GLA_HEADS * GLA_DK
GLA_V = GLA_HEADS * GLA_DV
GLA_RANK = 16
GLA_GATE_NORM = 16.0
GLA_CHUNK = 64
GLA_COLS = 2 * GLA_QK + 2 * GLA_V + GLA_RANK
RW_HEADS = 16
RW_N = 64
RW_W = RW_HEADS * RW_N
RW_LORA = 64 + 64 + 128
RW_COLS = 3 * RW_W + RW_LORA
RW_GN_EPS = 64e-5
D_FF = 5632
EPS = 1e-6
LN_EPS = 1e-5

PAD = (-N_META) % GLA_CHUNK
LP = PAD + N_META + SEQ
MT = BATCH * LP
FIRST = PAD
TM = LP // 3
TS = LP // 6
LANES = 128
GLR_PAD = LANES
GLA_MAIN = 2 * GLA_QK + 2 * GLA_V
CD_COLS = GLA_MAIN + RW_COLS + GLR_PAD
TN_AB = 1280
TN_CD = 2176
TN_FF = 512
HALO_FF = 16
HALO_CONF = 32
HALO_SC = 8
RW_TB = 64
GLA_TB = TM
VMEM_LIMIT = 56 * 1024 * 1024

F32 = jnp.float32
BF16 = jnp.bfloat16


def _params(sem):
    return pltpu.CompilerParams(dimension_semantics=sem, vmem_limit_bytes=VMEM_LIMIT)


def _sigmoid(x):
    return 1.0 / (1.0 + jnp.exp(-x))


def _silu(x):
    return x * _sigmoid(x)


def _softplus(x):
    return jnp.maximum(x, 0.0) + jnp.log(1.0 + jnp.exp(-jnp.abs(x)))


def _rms_rows(x, g):
    ms = jnp.mean(x * x, axis=-1, keepdims=True)
    return x * lax.rsqrt(ms + EPS) * g


NORM_CHUNK = 176


def _norm_rows_into(dst_ref, dst_row0, src_ref, g_ref):
    def body(c, carry):
        r = pl.multiple_of(c * NORM_CHUNK, 16)
        x = src_ref[pl.ds(r, NORM_CHUNK), :]
        dst_ref[pl.ds(pl.multiple_of(dst_row0 + r, 16), NORM_CHUNK), :] = _rms_rows(x, g_ref[...]).astype(BF16)
        return carry

    lax.fori_loop(0, src_ref.shape[0] // NORM_CHUNK, body, 0)


def _norm_matmul_kernel(h_ref, g_ref, w_ref, o_ref, hn_ref):
    @pl.when(pl.program_id(1) == 0)
    def _():
        _norm_rows_into(hn_ref, 0, h_ref, g_ref)

    o_ref[...] = jnp.dot(hn_ref[...], w_ref[...], preferred_element_type=F32)


def norm_matmul(h, g, w, tn):
    n = w.shape[1]
    return pl.pallas_call(
        _norm_matmul_kernel,
        grid=(MT // TM, n // tn),
        in_specs=[
            pl.BlockSpec((TM, D_MODEL), lambda i, j: (i, 0)),
            pl.BlockSpec((1, D_MODEL), lambda i, j: (0, 0)),
            pl.BlockSpec((D_MODEL, tn), lambda i, j: (0, j)),
        ],
        out_specs=pl.BlockSpec((TM, tn), lambda i, j: (i, j)),
        out_shape=jax.ShapeDtypeStruct((MT, n), F32),
        scratch_shapes=[pltpu.VMEM((TM, D_MODEL), BF16)],
        compiler_params=_params(("parallel", "arbitrary")),
        name="norm_matmul",
    )(h, g, w)


def _zero_pad_rows(o_ref, tiles_per_batch):
    @pl.when(pl.program_id(0) % tiles_per_batch == 0)
    def _():
        o_ref[0:FIRST, :] = jnp.zeros((FIRST, o_ref.shape[1]), o_ref.dtype)


def _out_proj_kernel(h_ref, a1_ref, a2_ref, w1_ref, w2_ref, o_ref):
    acc = jnp.dot(a1_ref[...], w1_ref[...], preferred_element_type=F32)
    acc = acc + jnp.dot(a2_ref[...], w2_ref[...], preferred_element_type=F32)
    o_ref[...] = h_ref[...] + acc
    _zero_pad_rows(o_ref, LP // TS)


def out_proj(h, a1, a2, w):
    k1, k2 = a1.shape[1], a2.shape[1]
    assert k1 == k2 and w.shape[0] == k1 + k2
    w1 = w2 = w
    return pl.pallas_call(
        _out_proj_kernel,
        grid=(MT // TS,),
        in_specs=[
            pl.BlockSpec((TS, D_MODEL), lambda i: (i, 0)),
            pl.BlockSpec((TS, k1), lambda i: (i, 0)),
            pl.BlockSpec((TS, k2), lambda i: (i, 0)),
            pl.BlockSpec((k1, D_MODEL), lambda i: (0, 0)),
            pl.BlockSpec((k2, D_MODEL), lambda i: (1, 0)),
        ],
        out_specs=pl.BlockSpec((TS, D_MODEL), lambda i: (i, 0)),
        out_shape=jax.ShapeDtypeStruct((MT, D_MODEL), F32),
        compiler_params=_params(("parallel",)),
        name="out_proj",
    )(h, a1, a2, w1, w2)


def _ffn_kernel(h_ref, halo_ref, g_ref, wv_ref, wg_ref, dwv_ref, dwg_ref, wd_ref, gf_ref,
                o_ref, hn_ref, *, final_norm):
    j = pl.program_id(1)

    @pl.when(j == 0)
    def _():
        hn_ref[0:HALO_FF, :] = _rms_rows(halo_ref[...], g_ref[...]).astype(BF16)
        _norm_rows_into(hn_ref, HALO_FF, h_ref, g_ref)
        o_ref[...] = h_ref[...]

    hn = hn_ref[...]

    def conv(w_ref, dw_ref):
        u = jnp.dot(hn, w_ref[...], preferred_element_type=F32)
        u1 = pltpu.roll(u, 1, 0)
        u2 = pltpu.roll(u, 2, 0)
        c = u * dw_ref[2:3, :] + u1 * dw_ref[1:2, :] + u2 * dw_ref[0:1, :]
        return c[HALO_FF:, :]

    act = (_silu(conv(wg_ref, dwg_ref)) * conv(wv_ref, dwv_ref)).astype(BF16)
    for n in range(D_MODEL // TN_FF):
        cols = slice(n * TN_FF, (n + 1) * TN_FF)
        o_ref[:, cols] += jnp.dot(act, wd_ref[:, cols], preferred_element_type=F32)

    @pl.when(j == pl.num_programs(1) - 1)
    def _():
        if final_norm:
            def body(c, carry):
                rows = pl.ds(pl.multiple_of(c * NORM_CHUNK, 16), NORM_CHUNK)
                o_ref[rows, :] = _rms_rows(o_ref[rows, :], gf_ref[...])
                return carry

            lax.fori_loop(0, TM // NORM_CHUNK, body, 0)
        _zero_pad_rows(o_ref, LP // TM)


def conv_ffn(h, g, w_up, dw, w_down, g_final, layer, final_norm):
    nj = D_FF // TN_FF
    halo_blocks = TM // HALO_FF
    return pl.pallas_call(
        functools.partial(_ffn_kernel, final_norm=final_norm),
        grid=(MT // TM, nj),
        in_specs=[
            pl.BlockSpec((TM, D_MODEL), lambda i, j: (i, 0)),
            pl.BlockSpec((HALO_FF, D_MODEL), lambda i, j: (jnp.maximum(i * halo_blocks - 1, 0), 0)),
            pl.BlockSpec((1, D_MODEL), lambda i, j: (0, 0)),
            pl.BlockSpec((None, D_MODEL, TN_FF), lambda i, j: (layer, 0, j)),
            pl.BlockSpec((None, D_MODEL, TN_FF), lambda i, j: (layer, 0, nj + j)),
            pl.BlockSpec((None, 3, TN_FF), lambda i, j: (layer, 0, j)),
            pl.BlockSpec((None, 3, TN_FF), lambda i, j: (layer, 0, nj + j)),
            pl.BlockSpec((None, TN_FF, D_MODEL), lambda i, j: (layer, j, 0)),
            pl.BlockSpec((1, D_MODEL), lambda i, j: (0, 0)),
        ],
        out_specs=pl.BlockSpec((TM, D_MODEL), lambda i, j: (i, 0)),
        out_shape=jax.ShapeDtypeStruct((MT, D_MODEL), F32),
        scratch_shapes=[pltpu.VMEM((HALO_FF + TM, D_MODEL), BF16)],
        compiler_params=_params(("parallel", "arbitrary")),
        name="conv_ffn",
    )(h, h, g, w_up, w_up, dw, dw, w_down, g_final)


CONF_RB = 88


def _mix_ab_kernel(av_ref, ag_ref, sb_ref, sc_ref, sx_ref, hav_ref, hag_ref, hsc_ref, hsx_ref,
                   cw_ref, cb_ref, lg_ref, lb_ref, sw_ref, a_ref, s_ref, xs_ref, ac_ref, cx_ref):
    xs_ref[0:HALO_CONF, :] = hav_ref[...] * _sigmoid(hag_ref[...])
    xs_ref[HALO_CONF:, :] = av_ref[...] * _sigmoid(ag_ref[...])
    base = HALO_CONF - (CONF_K - 1)

    def lane_chunk(c, carry):
        cols = pl.ds(pl.multiple_of(c * LANES, LANES), LANES)
        for rb in range(TS // CONF_RB):
            r0 = rb * CONF_RB
            acc = jnp.broadcast_to(cb_ref[:, cols], (CONF_RB, LANES))
            for t in range(CONF_K):
                acc = acc + xs_ref[pl.ds(r0 + base + t, CONF_RB), cols] * cw_ref[pl.ds(t, 1), cols]
            ac_ref[pl.ds(r0, CONF_RB), cols] = acc
        return carry

    lax.fori_loop(0, CONF_W // LANES, lane_chunk, 0)
    a = ac_ref[...]
    mu = jnp.mean(a, axis=-1, keepdims=True)
    d = a - mu
    var = jnp.mean(d * d, axis=-1, keepdims=True)
    y = d * lax.rsqrt(var + LN_EPS) * lg_ref[...] + lb_ref[...]
    a_ref[...] = _silu(y).astype(BF16)

    cx_ref[0:HALO_SC, :] = hsc_ref[...] * hsx_ref[...]
    cx_ref[HALO_SC:, :] = sc_ref[...] * sx_ref[...]
    cv = cx_ref[pl.ds(HALO_SC - 2, TS), :] * sw_ref[0:1, :]
    cv = cv + cx_ref[pl.ds(HALO_SC - 1, TS), :] * sw_ref[1:2, :]
    cv = cv + cx_ref[pl.ds(HALO_SC, TS), :] * sw_ref[2:3, :]
    s_ref[...] = (sb_ref[...] * cv).astype(BF16)


def mix_ab(z, conf_dw, conf_b, ln_g, ln_b, sc_dw):
    w = CONF_W
    hc, hs = TS // HALO_CONF, TS // HALO_SC

    def col(k):
        return pl.BlockSpec((TS, w), lambda i, k=k: (i, k))

    def halo(rows, per_tile, k):
        return pl.BlockSpec((rows, w), lambda i, k=k: (jnp.maximum(i * per_tile - 1, 0), k))

    def full(r):
        return pl.BlockSpec((r, w), lambda i: (0, 0))

    return pl.pallas_call(
        _mix_ab_kernel,
        grid=(MT // TS,),
        in_specs=[col(0), col(1), col(2), col(3), col(4),
                  halo(HALO_CONF, hc, 0), halo(HALO_CONF, hc, 1),
                  halo(HALO_SC, hs, 3), halo(HALO_SC, hs, 4),
                  full(CONF_K), full(1), full(1), full(1), full(SC_K)],
        out_specs=[pl.BlockSpec((TS, w), lambda i: (i, 0)), pl.BlockSpec((TS, w), lambda i: (i, 0))],
        out_shape=[jax.ShapeDtypeStruct((MT, w), BF16), jax.ShapeDtypeStruct((MT, w), BF16)],
        scratch_shapes=[pltpu.VMEM((HALO_CONF + TS, w), F32), pltpu.VMEM((TS, w), F32),
                        pltpu.VMEM((HALO_SC + TS, w), F32)],
        compiler_params=_params(("parallel",)),
        name="mix_ab",
    )(z, z, z, z, z, z, z, z, z, conf_dw, conf_b, ln_g, ln_b, sc_dw)


def _gla_kernel(q_ref, k_ref, v_ref, go_ref, glr_ref, w2_ref, b_ref, gn_ref, o_ref, st_ref):
    c_len = GLA_CHUNK
    tb = pl.program_id(1)

    @pl.when(tb == 0)
    def _():
        st_ref[...] = jnp.zeros(st_ref.shape, F32)

    ri = lax.broadcasted_iota(jnp.int32, (c_len, c_len), 0)
    ci = lax.broadcasted_iota(jnp.int32, (c_len, c_len), 1)
    tril = ri >= ci
    tril_f = tril.astype(F32)
    row = lax.broadcasted_iota(jnp.int32, (c_len, 1), 0)
    nt_dims = (((1,), (1,)), ((), ()))

    def chunk(c, carry):
        rows = pl.ds(pl.multiple_of(c * c_len, c_len), c_len)
        lin = jnp.dot(glr_ref[rows, :], w2_ref[...], preferred_element_type=F32) + b_ref[...]
        log_a = -_softplus(-lin) / GLA_GATE_NORM
        log_a = jnp.where(tb * GLA_TB + c * c_len + row >= FIRST, log_a, 0.0)
        cum_all = jnp.dot(tril_f, log_a, preferred_element_type=F32, precision=lax.Precision.HIGHEST)
        q_all = q_ref[rows, :] * (GLA_DK ** -0.5)
        k_all = k_ref[rows, :]
        v_all = v_ref[rows, :]
        go_all = go_ref[rows, :]
        outs = []
        for h in range(GLA_HEADS):
            kc = slice(h * GLA_DK, (h + 1) * GLA_DK)
            vc = slice(h * GLA_DV, (h + 1) * GLA_DV)
            cum, q, k, v = cum_all[:, kc], q_all[:, kc], k_all[:, kc], v_all[:, vc]
            last = cum[c_len - 1:c_len, :]
            q_dec = q * jnp.exp(cum)
            k_dec = k * jnp.exp(-cum)
            k_st = k * jnp.exp(last - cum)
            scores = lax.dot_general(q_dec, k_dec, nt_dims, preferred_element_type=F32)
            scores = jnp.where(tril, scores, 0.0)
            st = st_ref[h]
            o = jnp.dot(scores, v, preferred_element_type=F32)
            o = o + lax.dot_general(q_dec, st, nt_dims, preferred_element_type=F32)
            st_ref[h] = st * jnp.exp(last) + jnp.dot(v.T, k_st, preferred_element_type=F32)
            o = o * lax.rsqrt(jnp.mean(o * o, axis=-1, keepdims=True) + EPS)
            outs.append(o * gn_ref[:, vc] * _silu(go_all[:, vc]))
        o_ref[rows, :] = jnp.concatenate(outs, axis=-1).astype(BF16)
        return carry

    lax.fori_loop(0, GLA_TB // c_len, chunk, 0)


def gla(z, w2p, b, gn):
    z3 = z.reshape(BATCH, LP, CD_COLS)

    def zcol(width, k):
        return pl.BlockSpec((None, GLA_TB, width), lambda b_, t, k=k: (b_, t, k))

    def full(shape):
        return pl.BlockSpec(shape, lambda b_, t: (0, 0))

    return pl.pallas_call(
        _gla_kernel,
        grid=(BATCH, LP // GLA_TB),
        in_specs=[zcol(GLA_QK, 0), zcol(GLA_QK, 1), zcol(GLA_V, 1), zcol(GLA_V, 2),
                  zcol(GLR_PAD, (GLA_MAIN + RW_COLS) // GLR_PAD),
                  full((GLR_PAD, GLA_QK)), full((1, GLA_QK)), full((1, GLA_V))],
        out_specs=pl.BlockSpec((None, GLA_TB, GLA_V), lambda b_, t: (b_, t, 0)),
        out_shape=jax.ShapeDtypeStruct((BATCH, LP, GLA_V), BF16),
        scratch_shapes=[pltpu.VMEM((GLA_HEADS, GLA_DV, GLA_DK), F32)],
        compiler_params=_params(("parallel", "arbitrary")),
        name="gla",
    )(z3, z3, z3, z3, z3, w2p, b, gn).reshape(MT, GLA_V)


def _head_sum(x, ones_blk):
    parts = [jnp.dot(x[:, c * LANES:(c + 1) * LANES], ones_blk, preferred_element_type=F32,
                     precision=lax.Precision.HIGHEST) for c in range(RW_W // LANES)]
    return jnp.concatenate(parts, axis=-1)


def _head_ones():
    r = lax.broadcasted_iota(jnp.int32, (LANES, LANES), 0) // RW_N
    c = lax.broadcasted_iota(jnp.int32, (LANES, LANES), 1) // RW_N
    return (r == c).astype(F32)


def _rw_prep_kernel(r_ref, k_ref, v_ref, lo_ref, hr_ref, hk_ref, hv_ref, hlo_ref, mu_ref, mulo_ref,
                    w0_ref, w2_ref, a0_ref, a2_ref, g2_ref, kkp_ref, kap_ref,
                    ro_ref, wo_ref, ko_ref, vo_ref, kko_ref, bo_ref, go_ref):
    row = lax.broadcasted_iota(jnp.int32, (TS, 1), 0)

    def shift_mix(x_ref, halo_ref, mu):
        x = x_ref[...]
        prev = jnp.where(row == 0, halo_ref[HALO_SC - 1:HALO_SC, :], pltpu.roll(x, 1, 0))
        return x + (prev - x) * mu

    r = shift_mix(r_ref, hr_ref, mu_ref[0:1, :])
    k = shift_mix(k_ref, hk_ref, mu_ref[1:2, :])
    v = shift_mix(v_ref, hv_ref, mu_ref[2:3, :])
    lo = shift_mix(lo_ref, hlo_ref, mulo_ref[...])
    xw, xa, xg = lo[:, 0:64], lo[:, 64:128], lo[:, 128:256]
    w_log = -_softplus(-(w0_ref[...] + jnp.dot(jnp.tanh(xw), w2_ref[...], preferred_element_type=F32))) - 0.5
    decay = jnp.exp(-jnp.exp(w_log))
    a = _sigmoid(a0_ref[...] + jnp.dot(xa, a2_ref[...], preferred_element_type=F32))
    g = jnp.dot(_sigmoid(xg), g2_ref[...], preferred_element_type=F32)
    real = jnp.logical_or(pl.program_id(0) % (LP // TS) != 0, row >= FIRST)
    k = jnp.where(real, k, 0.0)
    v = jnp.where(real, v, 0.0)
    kk = k * kkp_ref[...]
    kk = kk / jnp.maximum(jnp.sqrt(_head_sum(kk * kk, _head_ones())), 1e-12)
    ro_ref[...] = r
    wo_ref[...] = decay
    ko_ref[...] = k * (1.0 + (a - 1.0) * kap_ref[...])
    vo_ref[...] = v
    kko_ref[...] = kk
    bo_ref[...] = kk * a
    go_ref[...] = g


def rw_prep(z, mu, w0, w2, a0, a2, g2, kkp, kap):
    w = RW_W
    hs = TS // HALO_SC
    c0 = GLA_MAIN // w
    lo0 = (GLA_MAIN + 3 * w) // RW_LORA

    def col(k):
        return pl.BlockSpec((TS, w), lambda i, k=k: (i, c0 + k))

    def halo(k):
        return pl.BlockSpec((HALO_SC, w), lambda i, k=k: (jnp.maximum(i * hs - 1, 0), c0 + k))

    def full(shape):
        return pl.BlockSpec(shape, lambda i: (0, 0))

    out = pl.BlockSpec((TS, w), lambda i: (i, 0))
    return pl.pallas_call(
        _rw_prep_kernel,
        grid=(MT // TS,),
        in_specs=[col(0), col(1), col(2),
                  pl.BlockSpec((TS, RW_LORA), lambda i: (i, lo0)),
                  halo(0), halo(1), halo(2),
                  pl.BlockSpec((HALO_SC, RW_LORA), lambda i: (jnp.maximum(i * hs - 1, 0), lo0)),
                  full((3, w)), full((1, RW_LORA)),
                  full((1, w)), full((64, w)), full((1, w)), full((64, w)), full((128, w)),
                  full((1, w)), full((1, w))],
        out_specs=[out] * 7,
        out_shape=[jax.ShapeDtypeStruct((MT, w), F32)] * 7,
        compiler_params=_params(("parallel",)),
        name="rw_prep",
    )(z, z, z, z, z, z, z, z, mu[:, :3 * w].reshape(3, w), mu[:, 3 * w:], w0, w2, a0, a2, g2, kkp, kap)


RW_PAIRS = RW_HEADS // 2
RW_UNROLL = 8


def _rw_scan_kernel(r_ref, w_ref, k_ref, v_ref, kk_ref, b_ref, y_ref, s_ref, yt_ref):
    @pl.when(pl.program_id(0) == 0)
    def _():
        s_ref[...] = jnp.zeros(s_ref.shape, F32)

    yt_ref[...] = jnp.zeros(yt_ref.shape, F32)
    ones_blk = _head_ones().astype(BF16)
    vi = lax.broadcasted_iota(jnp.int32, (RW_N, LANES), 0)
    li = lax.broadcasted_iota(jnp.int32, (RW_N, LANES), 1)
    diag = (li % RW_N == vi).astype(F32)
    lane_t = li % RW_N

    pairs = [(b, p) for b in range(BATCH) for p in range(RW_PAIRS)]

    def bsum(parts):
        out = jnp.dot(jnp.concatenate(parts, axis=0), ones_blk, preferred_element_type=F32)
        return [out[i * RW_N:(i + 1) * RW_N, :] for i in range(len(parts))]

    def steps(g, carry):
        t0 = pl.multiple_of(g * RW_UNROLL, RW_UNROLL)
        for j in range(RW_UNROLL):
            hit = lane_t == t0 + j

            def rowv(ref, b, p):
                return ref[b, pl.ds(t0, RW_UNROLL), p * LANES:(p + 1) * LANES][j:j + 1, :]

            sa = bsum([(s_ref[b, p] * rowv(kk_ref, b, p)).astype(BF16) for b, p in pairs])
            vb = bsum([(diag * rowv(v_ref, b, p)).astype(BF16) for b, p in pairs])
            sr = []
            for i, (b, p) in enumerate(pairs):
                s = s_ref[b, p] * rowv(w_ref, b, p) - sa[i] * rowv(b_ref, b, p) + vb[i] * rowv(k_ref, b, p)
                s_ref[b, p] = s
                sr.append((s * rowv(r_ref, b, p)).astype(BF16))
            yb = bsum(sr)
            for i, (b, p) in enumerate(pairs):
                yt_ref[b, p] = jnp.where(hit, yb[i], yt_ref[b, p])
        return carry

    lax.fori_loop(0, RW_TB // RW_UNROLL, steps, 0)

    zeros = jnp.zeros((LANES - RW_N, LANES), F32)
    for b in range(BATCH):
        for p in range(RW_PAIRS):
            yt = jnp.concatenate([yt_ref[b, p], zeros], axis=0).T
            lo = yt[0:RW_TB, :]
            hi = pltpu.roll(yt[RW_N:RW_N + RW_TB, :], RW_N, 1)
            lane = lax.broadcasted_iota(jnp.int32, (RW_TB, LANES), 1)
            y_ref[b, :, p * LANES:(p + 1) * LANES] = jnp.where(lane < RW_N, lo, hi)


def rw_scan(r, w, k, v, kk, bb):
    spec = pl.BlockSpec((BATCH, RW_TB, RW_W), lambda t: (0, t, 0))
    args = [x.reshape(BATCH, LP, RW_W) for x in (r, w, k, v, kk, bb)]
    return pl.pallas_call(
        _rw_scan_kernel,
        grid=(LP // RW_TB,),
        in_specs=[spec] * 6,
        out_specs=spec,
        out_shape=jax.ShapeDtypeStruct((BATCH, LP, RW_W), F32),
        scratch_shapes=[pltpu.VMEM((BATCH, RW_PAIRS, RW_N, LANES), F32),
                        pltpu.VMEM((BATCH, RW_PAIRS, RW_N, LANES), F32)],
        compiler_params=_params(("arbitrary",)),
        name="rw_scan",
    )(*args).reshape(MT, RW_W)


def _rw_post_kernel(y_ref, r_ref, k_ref, v_ref, g_ref, lg_ref, lb_ref, rk_ref, o_ref):
    ones_blk = _head_ones()
    y = y_ref[...]
    mu = _head_sum(y, ones_blk) * (1.0 / RW_N)
    d = y - mu
    var = _head_sum(d * d, ones_blk) * (1.0 / RW_N)
    yn = d * lax.rsqrt(var + RW_GN_EPS) * lg_ref[...] + lb_ref[...]
    bonus = _head_sum(r_ref[...] * k_ref[...] * rk_ref[...], ones_blk) * v_ref[...]
    o_ref[...] = ((yn + bonus) * g_ref[...]).astype(BF16)


def rw_post(y, r, k, v, g, lg, lb, rk):
    blk = pl.BlockSpec((TS, RW_W), lambda i: (i, 0))
    par = pl.BlockSpec((1, RW_W), lambda i: (0, 0))
    return pl.pallas_call(
        _rw_post_kernel,
        grid=(MT // TS,),
        in_specs=[blk] * 5 + [par] * 3,
        out_specs=blk,
        out_shape=jax.ShapeDtypeStruct((MT, RW_W), BF16),
        compiler_params=_params(("parallel",)),
        name="rw_post",
    )(y, r, k, v, g, lg, lb, rk)


def kernel(x, meta, ab_w_in, ab_conf_dw, ab_conf_dw_b, ab_conf_ln_g, ab_conf_ln_b, ab_sc_dw, ab_w_out,
           cd_w_in, cd_gla_w2, cd_gla_b, cd_gla_norm_g, cd_rw_mu, cd_rw_w0, cd_rw_w2, cd_rw_a0, cd_rw_a2,
           cd_rw_g2, cd_rw_kk, cd_rw_ka, cd_rw_rk, cd_rw_ln_g, cd_rw_ln_b, cd_w_out, norm_mix, norm_ffn,
           ffn_w_up, ffn_dw, ffn_w_down, norm_final):
    bs = x.shape[0]
    h = jnp.concatenate([jnp.zeros((bs, PAD, D_MODEL), x.dtype),
                         jnp.broadcast_to(meta[None].astype(x.dtype), (bs, N_META, D_MODEL)), x], axis=1)
    h = h.reshape(MT, D_MODEL)
    row2 = lambda p: p.reshape(1, -1)
    g_final = row2(norm_final)

    z = norm_matmul(h, row2(norm_mix[0]), ab_w_in[0].astype(BF16), TN_AB)
    a_act, s_act = mix_ab(z, ab_conf_dw[0], row2(ab_conf_dw_b[0]), row2(ab_conf_ln_g[0]),
                          row2(ab_conf_ln_b[0]), ab_sc_dw[0])
    h = out_proj(h, a_act, s_act, ab_w_out[0].astype(BF16))
    w_up, w_down = ffn_w_up.astype(BF16), ffn_w_down.astype(BF16)
    h = conv_ffn(h, row2(norm_ffn[0]), w_up, ffn_dw, w_down, g_final, 0, False)

    w_in = cd_w_in[0]
    w_in = jnp.concatenate([w_in[:, :GLA_MAIN], w_in[:, GLA_COLS:], w_in[:, GLA_MAIN:GLA_COLS],
                            jnp.zeros((D_MODEL, GLR_PAD - GLA_RANK), w_in.dtype)], axis=1).astype(BF16)
    z = norm_matmul(h, row2(norm_mix[1]), w_in, TN_CD)
    w2p = jnp.concatenate([cd_gla_w2[0], jnp.zeros((GLR_PAD - GLA_RANK, GLA_QK), F32)], axis=0)
    o_act = gla(z, w2p, row2(cd_gla_b[0]), row2(cd_gla_norm_g[0]))
    r, w, k, v, kk, bb, g = rw_prep(z, row2(cd_rw_mu[0]), row2(cd_rw_w0[0]), cd_rw_w2[0], row2(cd_rw_a0[0]),
                                    cd_rw_a2[0], cd_rw_g2[0], row2(cd_rw_kk[0]), row2(cd_rw_ka[0]))
    y = rw_scan(r, w, k, v, kk, bb)
    y_act = rw_post(y, r, k, v, g, row2(cd_rw_ln_g[0]), row2(cd_rw_ln_b[0]), row2(cd_rw_rk[0]))
    h = out_proj(h, o_act, y_act, cd_w_out[0].astype(BF16))
    h = conv_ffn(h, row2(norm_ffn[1]), w_up, ffn_dw, w_down, g_final, 1, True)
    return h.reshape(bs, LP, D_MODEL)[:, PAD + N_META:]
```

```python
import functools

import jax
import jax.numpy as jnp
from jax import lax
from jax.experimental import pallas as pl
from jax.experimental.pallas import tpu as pltpu

D_MODEL = 2048
BATCH = 4
SEQ = 2048
N_META = 16
CONF_W = 1024
CONF_K = 31
SC_W = 1024
SC_K = 3
AB_IN = 5 * 1024
GLA_HEADS = 4
GLA_DK = 128
GLA_DV = 256
GLA_QK = GLA_HEADS * GLA_DK
GLA_V = GLA_HEADS * GLA_DV
GLA_RANK = 16
GLA_GATE_NORM = 16.0
GLA_CHUNK = 64
GLA_COLS = 2 * GLA_QK + 2 * GLA_V + GLA_RANK
RW_HEADS = 16
RW_N = 64
RW_W = RW_HEADS * RW_N
RW_LORA = 64 + 64 + 128
RW_COLS = 3 * RW_W + RW_LORA
RW_GN_EPS = 64e-5
D_FF = 5632
EPS = 1e-6
LN_EPS = 1e-5

PAD = (-N_META) % GLA_CHUNK
LP = PAD + N_META + SEQ
MT = BATCH * LP
FIRST = PAD
TM = LP // 3
TS = LP // 6
LANES = 128
SUBLANES = 8
GLR_PAD = LANES
GLA_MAIN = 2 * GLA_QK + 2 * GLA_V
CD_COLS = GLA_MAIN + RW_COLS + GLR_PAD
TN_AB = 1280
TN_CD = 2176
TN_FF = 512
HALO_FF = 16
HALO_CONF = 32
HALO_SC = 8
RW_TB = 64
GLA_TB = TM
VMEM_LIMIT = 56 * 1024 * 1024

F32 = jnp.float32
BF16 = jnp.bfloat16


def _params(sem):
    return pltpu.CompilerParams(dimension_semantics=sem, vmem_limit_bytes=VMEM_LIMIT)


def _sigmoid(x):
    return 1.0 / (1.0 + jnp.exp(-x))


def _silu(x):
    return x * _sigmoid(x)


def _softplus(x):
    return jnp.maximum(x, 0.0) + jnp.log(1.0 + jnp.exp(-jnp.abs(x)))


def _split3(x):
    hi = x.astype(BF16)
    r = x - hi.astype(F32)
    mid = r.astype(BF16)
    lo = (r - mid.astype(F32)).astype(BF16)
    return hi, mid, lo


def _dot_x_m01(x, m01):
    return sum(jnp.dot(t, m01, preferred_element_type=F32) for t in _split3(x))


def _dot_m01_x(m01, x):
    return sum(jnp.dot(m01, t, preferred_element_type=F32) for t in _split3(x))


def _rms_rows(x, g):
    ms = jnp.mean(x * x, axis=-1, keepdims=True)
    return x * lax.rsqrt(ms + EPS) * g


NORM_CHUNK = 176


def _norm_rows_into(dst_ref, dst_row0, src_ref, g_ref):
    def body(c, carry):
        r = pl.multiple_of(c * NORM_CHUNK, 16)
        x = src_ref[pl.ds(r, NORM_CHUNK), :]
        dst_ref[pl.ds(pl.multiple_of(dst_row0 + r, 16), NORM_CHUNK), :] = _rms_rows(x, g_ref[...]).astype(BF16)
        return carry

    lax.fori_loop(0, src_ref.shape[0] // NORM_CHUNK, body, 0)


def _norm_matmul_kernel(h_ref, g_ref, w_ref, o_ref, hn_ref):
    @pl.when(pl.program_id(1) == 0)
    def _():
        _norm_rows_into(hn_ref, 0, h_ref, g_ref)

    o_ref[...] = jnp.dot(hn_ref[...], w_ref[...], preferred_element_type=F32)


def norm_matmul(h, g, w, tn):
    n = w.shape[1]
    return pl.pallas_call(
        _norm_matmul_kernel,
        grid=(MT // TM, n // tn),
        in_specs=[
            pl.BlockSpec((TM, D_MODEL), lambda i, j: (i, 0)),
            pl.BlockSpec((1, D_MODEL), lambda i, j: (0, 0)),
            pl.BlockSpec((D_MODEL, tn), lambda i, j: (0, j)),
        ],
        out_specs=pl.BlockSpec((TM, tn), lambda i, j: (i, j)),
        out_shape=jax.ShapeDtypeStruct((MT, n), F32),
        scratch_shapes=[pltpu.VMEM((TM, D_MODEL), BF16)],
        compiler_params=_params(("parallel", "arbitrary")),
        name="norm_matmul",
    )(h, g, w)


def _zero_pad_rows(o_ref, tiles_per_batch):
    @pl.when(pl.program_id(0) % tiles_per_batch == 0)
    def _():
        o_ref[0:FIRST, :] = jnp.zeros((FIRST, o_ref.shape[1]), o_ref.dtype)


def _out_proj_kernel(h_ref, a1_ref, a2_ref, w1_ref, w2_ref, o_ref):
    acc = jnp.dot(a1_ref[...], w1_ref[...], preferred_element_type=F32)
    acc = acc + jnp.dot(a2_ref[...], w2_ref[...], preferred_element_type=F32)
    o_ref[...] = h_ref[...] + acc
    _zero_pad_rows(o_ref, LP // TS)


def out_proj(h, a1, a2, w):
    k1, k2 = a1.shape[1], a2.shape[1]
    assert k1 == k2 and w.shape[0] == k1 + k2
    w1 = w2 = w
    return pl.pallas_call(
        _out_proj_kernel,
        grid=(MT // TS,),
        in_specs=[
            pl.BlockSpec((TS, D_MODEL), lambda i: (i, 0)),
            pl.BlockSpec((TS, k1), lambda i: (i, 0)),
            pl.BlockSpec((TS, k2), lambda i: (i, 0)),
            pl.BlockSpec((k1, D_MODEL), lambda i: (0, 0)),
            pl.BlockSpec((k2, D_MODEL), lambda i: (1, 0)),
        ],
        out_specs=pl.BlockSpec((TS, D_MODEL), lambda i: (i, 0)),
        out_shape=jax.ShapeDtypeStruct((MT, D_MODEL), F32),
        compiler_params=_params(("parallel",)),
        name="out_proj",
    )(h, a1, a2, w1, w2)


def _ffn_kernel(h_ref, halo_ref, g_ref, wv_ref, wg_ref, dwv_ref, dwg_ref, wd_ref, gf_ref,
                o_ref, hn_ref, *, final_norm):
    j = pl.program_id(1)

    @pl.when(j == 0)
    def _():
        hn_ref[0:HALO_FF, :] = _rms_rows(halo_ref[...], g_ref[...]).astype(BF16)
        _norm_rows_into(hn_ref, HALO_FF, h_ref, g_ref)
        o_ref[...] = h_ref[...]

    hn = hn_ref[...]

    def conv(w_ref, dw_ref):
        u = jnp.dot(hn, w_ref[...], preferred_element_type=F32)
        u1 = pltpu.roll(u, 1, 0)
        u2 = pltpu.roll(u, 2, 0)
        c = u * dw_ref[2:3, :] + u1 * dw_ref[1:2, :] + u2 * dw_ref[0:1, :]
        return c[HALO_FF:, :]

    act = (_silu(conv(wg_ref, dwg_ref)) * conv(wv_ref, dwv_ref)).astype(BF16)
    for n in range(D_MODEL // TN_FF):
        cols = slice(n * TN_FF, (n + 1) * TN_FF)
        o_ref[:, cols] += jnp.dot(act, wd_ref[:, cols], preferred_element_type=F32)

    @pl.when(j == pl.num_programs(1) - 1)
    def _():
        if final_norm:
            def body(c, carry):
                rows = pl.ds(pl.multiple_of(c * NORM_CHUNK, 16), NORM_CHUNK)
                o_ref[rows, :] = _rms_rows(o_ref[rows, :], gf_ref[...])
                return carry

            lax.fori_loop(0, TM // NORM_CHUNK, body, 0)
        _zero_pad_rows(o_ref, LP // TM)


def conv_ffn(h, g, w_up, dw, w_down, g_final, layer, final_norm):
    nj = D_FF // TN_FF
    halo_blocks = TM // HALO_FF
    return pl.pallas_call(
        functools.partial(_ffn_kernel, final_norm=final_norm),
        grid=(MT // TM, nj),
        in_specs=[
            pl.BlockSpec((TM, D_MODEL), lambda i, j: (i, 0)),
            pl.BlockSpec((HALO_FF, D_MODEL), lambda i, j: (jnp.maximum(i * halo_blocks - 1, 0), 0)),
            pl.BlockSpec((1, D_MODEL), lambda i, j: (0, 0)),
            pl.BlockSpec((None, D_MODEL, TN_FF), lambda i, j: (layer, 0, j)),
            pl.BlockSpec((None, D_MODEL, TN_FF), lambda i, j: (layer, 0, nj + j)),
            pl.BlockSpec((None, 3, TN_FF), lambda i, j: (layer, 0, j)),
            pl.BlockSpec((None, 3, TN_FF), lambda i, j: (layer, 0, nj + j)),
            pl.BlockSpec((None, TN_FF, D_MODEL), lambda i, j: (layer, j, 0)),
            pl.BlockSpec((1, D_MODEL), lambda i, j: (0, 0)),
        ],
        out_specs=pl.BlockSpec((TM, D_MODEL), lambda i, j: (i, 0)),
        out_shape=jax.ShapeDtypeStruct((MT, D_MODEL), F32),
        scratch_shapes=[pltpu.VMEM((HALO_FF + TM, D_MODEL), BF16)],
        compiler_params=_params(("parallel", "arbitrary")),
        name="conv_ffn",
    )(h, h, g, w_up, w_up, dw, dw, w_down, g_final)


CONF_RB = 88


def _mix_ab_kernel(av_ref, ag_ref, sb_ref, sc_ref, sx_ref, hav_ref, hag_ref, hsc_ref, hsx_ref,
                   cw_ref, cb_ref, lg_ref, lb_ref, sw_ref, a_ref, s_ref, xs_ref, sh_ref, ac_ref, cx_ref):
    xs_ref[0:HALO_CONF, :] = hav_ref[...] * _sigmoid(hag_ref[...])
    xs_ref[HALO_CONF:HALO_CONF + TS, :] = av_ref[...] * _sigmoid(ag_ref[...])
    base = HALO_CONF - (CONF_K - 1)

    sub = SUBLANES
    sh_rows = HALO_CONF + TS
    xs_ref[sh_rows:, :] = jnp.zeros((sub, CONF_W), F32)

    def lane_chunk(c, carry):
        cols = pl.ds(pl.multiple_of(c * LANES, LANES), LANES)
        for s in range(sub):
            sh_ref[s] = xs_ref[pl.ds(s, sh_rows), cols]
        for rb in range(TS // CONF_RB):
            r0 = rb * CONF_RB
            acc = jnp.broadcast_to(cb_ref[:, cols], (CONF_RB, LANES))
            for t in range(CONF_K):
                off = base + t
                acc = acc + sh_ref[off % sub, pl.ds(r0 + off - off % sub, CONF_RB), :] * cw_ref[pl.ds(t, 1), cols]
            ac_ref[pl.ds(r0, CONF_RB), cols] = acc
        return carry

    lax.fori_loop(0, CONF_W // LANES, lane_chunk, 0)

    def norm_rows(c, carry):
        rows = pl.ds(pl.multiple_of(c * NORM_CHUNK, 16), NORM_CHUNK)
        a = ac_ref[rows, :]
        mu = jnp.mean(a, axis=-1, keepdims=True)
        d = a - mu
        var = jnp.mean(d * d, axis=-1, keepdims=True)
        y = d * lax.rsqrt(var + LN_EPS) * lg_ref[...] + lb_ref[...]
        a_ref[rows, :] = _silu(y).astype(BF16)
        return carry

    lax.fori_loop(0, TS // NORM_CHUNK, norm_rows, 0)

    cx_ref[0:HALO_SC, :] = hsc_ref[...] * hsx_ref[...]
    cx_ref[HALO_SC:, :] = sc_ref[...] * sx_ref[...]
    cv = cx_ref[pl.ds(HALO_SC - 2, TS), :] * sw_ref[0:1, :]
    cv = cv + cx_ref[pl.ds(HALO_SC - 1, TS), :] * sw_ref[1:2, :]
    cv = cv + cx_ref[pl.ds(HALO_SC, TS), :] * sw_ref[2:3, :]
    s_ref[...] = (sb_ref[...] * cv).astype(BF16)


def mix_ab(z, conf_dw, conf_b, ln_g, ln_b, sc_dw):
    w = CONF_W
    hc, hs = TS // HALO_CONF, TS // HALO_SC

    def col(k):
        return pl.BlockSpec((TS, w), lambda i, k=k: (i, k))

    def halo(rows, per_tile, k):
        return pl.BlockSpec((rows, w), lambda i, k=k: (jnp.maximum(i * per_tile - 1, 0), k))

    def full(r):
        return pl.BlockSpec((r, w), lambda i: (0, 0))

    return pl.pallas_call(
        _mix_ab_kernel,
        grid=(MT // TS,),
        in_specs=[col(0), col(1), col(2), col(3), col(4),
                  halo(HALO_CONF, hc, 0), halo(HALO_CONF, hc, 1),
                  halo(HALO_SC, hs, 3), halo(HALO_SC, hs, 4),
                  full(CONF_K), full(1), full(1), full(1), full(SC_K)],
        out_specs=[pl.BlockSpec((TS, w), lambda i: (i, 0)), pl.BlockSpec((TS, w), lambda i: (i, 0))],
        out_shape=[jax.ShapeDtypeStruct((MT, w), BF16), jax.ShapeDtypeStruct((MT, w), BF16)],
        scratch_shapes=[pltpu.VMEM((HALO_CONF + TS + SUBLANES, w), F32),
                        pltpu.VMEM((SUBLANES, HALO_CONF + TS, LANES), F32),
                        pltpu.VMEM((TS, w), F32), pltpu.VMEM((HALO_SC + TS, w), F32)],
        compiler_params=_params(("parallel",)),
        name="mix_ab",
    )(z, z, z, z, z, z, z, z, z, conf_dw, conf_b, ln_g, ln_b, sc_dw)


def _gla_kernel(q_ref, k_ref, v_ref, go_ref, glr_ref, w2_ref, b_ref, gn_ref, o_ref, st_ref):
    c_len = GLA_CHUNK
    tb = pl.program_id(1)

    @pl.when(tb == 0)
    def _():
        st_ref[...] = jnp.zeros(st_ref.shape, F32)

    ri = lax.broadcasted_iota(jnp.int32, (c_len, c_len), 0)
    ci = lax.broadcasted_iota(jnp.int32, (c_len, c_len), 1)
    tril = ri >= ci
    tril_b = jnp.where(tril, 1.0, 0.0).astype(BF16)
    row = lax.broadcasted_iota(jnp.int32, (c_len, 1), 0)
    nt_dims = (((1,), (1,)), ((), ()))

    def chunk(c, carry):
        rows = pl.ds(pl.multiple_of(c * c_len, c_len), c_len)
        lin = jnp.dot(glr_ref[rows, :], w2_ref[...], preferred_element_type=F32) + b_ref[...]
        log_a = -_softplus(-lin) / GLA_GATE_NORM
        log_a = jnp.where(tb * GLA_TB + c * c_len + row >= FIRST, log_a, 0.0)
        cum_all = _dot_m01_x(tril_b, log_a)
        q_all = q_ref[rows, :] * (GLA_DK ** -0.5)
        k_all = k_ref[rows, :]
        v_all = v_ref[rows, :]
        go_all = go_ref[rows, :]
        outs = []
        for h in range(GLA_HEADS):
            kc = slice(h * GLA_DK, (h + 1) * GLA_DK)
            vc = slice(h * GLA_DV, (h + 1) * GLA_DV)
            cum, q, k, v = cum_all[:, kc], q_all[:, kc], k_all[:, kc], v_all[:, vc]
            last = cum[c_len - 1:c_len, :]
            q_dec = q * jnp.exp(cum)
            k_dec = k * jnp.exp(-cum)
            k_st = k * jnp.exp(last - cum)
            scores = lax.dot_general(q_dec, k_dec, nt_dims, preferred_element_type=F32)
            scores = jnp.where(tril, scores, 0.0)
            st = st_ref[h]
            o = jnp.dot(scores, v, preferred_element_type=F32)
            o = o + lax.dot_general(q_dec, st, nt_dims, preferred_element_type=F32)
            st_ref[h] = st * jnp.exp(last) + jnp.dot(v.T, k_st, preferred_element_type=F32)
            o = o * lax.rsqrt(jnp.mean(o * o, axis=-1, keepdims=True) + EPS)
            outs.append(o * gn_ref[:, vc] * _silu(go_all[:, vc]))
        o_ref[rows, :] = jnp.concatenate(outs, axis=-1).astype(BF16)
        return carry

    lax.fori_loop(0, GLA_TB // c_len, chunk, 0)


def gla(z, w2p, b, gn):
    z3 = z.reshape(BATCH, LP, CD_COLS)

    def zcol(width, k):
        return pl.BlockSpec((None, GLA_TB, width), lambda b_, t, k=k: (b_, t, k))

    def full(shape):
        return pl.BlockSpec(shape, lambda b_, t: (0, 0))

    return pl.pallas_call(
        _gla_kernel,
        grid=(BATCH, LP // GLA_TB),
        in_specs=[zcol(GLA_QK, 0), zcol(GLA_QK, 1), zcol(GLA_V, 1), zcol(GLA_V, 2),
                  zcol(GLR_PAD, (GLA_MAIN + RW_COLS) // GLR_PAD),
                  full((GLR_PAD, GLA_QK)), full((1, GLA_QK)), full((1, GLA_V))],
        out_specs=pl.BlockSpec((None, GLA_TB, GLA_V), lambda b_, t: (b_, t, 0)),
        out_shape=jax.ShapeDtypeStruct((BATCH, LP, GLA_V), BF16),
        scratch_shapes=[pltpu.VMEM((GLA_HEADS, GLA_DV, GLA_DK), F32)],
        compiler_params=_params(("parallel", "arbitrary")),
        name="gla",
    )(z3, z3, z3, z3, z3, w2p, b, gn).reshape(MT, GLA_V)


def _head_sum(x, ones_blk):
    parts = [_dot_x_m01(x[:, c * LANES:(c + 1) * LANES], ones_blk) for c in range(RW_W // LANES)]
    return jnp.concatenate(parts, axis=-1)


def _head_ones():
    r = lax.broadcasted_iota(jnp.int32, (LANES, LANES), 0) // RW_N
    c = lax.broadcasted_iota(jnp.int32, (LANES, LANES), 1) // RW_N
    return jnp.where(r == c, 1.0, 0.0).astype(BF16)


def _rw_prep_kernel(r_ref, k_ref, v_ref, lo_ref, hr_ref, hk_ref, hv_ref, hlo_ref, mu_ref, mulo_ref,
                    w0_ref, w2_ref, a0_ref, a2_ref, g2_ref, kkp_ref, kap_ref,
                    ro_ref, wo_ref, ko_ref, vo_ref, kko_ref, bo_ref, go_ref):
    row = lax.broadcasted_iota(jnp.int32, (TS, 1), 0)

    def shift_mix(x_ref, halo_ref, mu):
        x = x_ref[...]
        prev = jnp.where(row == 0, halo_ref[HALO_SC - 1:HALO_SC, :], pltpu.roll(x, 1, 0))
        return x + (prev - x) * mu

    r = shift_mix(r_ref, hr_ref, mu_ref[0:1, :])
    k = shift_mix(k_ref, hk_ref, mu_ref[1:2, :])
    v = shift_mix(v_ref, hv_ref, mu_ref[2:3, :])
    lo = shift_mix(lo_ref, hlo_ref, mulo_ref[...])
    xw, xa, xg = lo[:, 0:64], lo[:, 64:128], lo[:, 128:256]
    w_log = -_softplus(-(w0_ref[...] + jnp.dot(jnp.tanh(xw), w2_ref[...], preferred_element_type=F32))) - 0.5
    decay = jnp.exp(-jnp.exp(w_log))
    a = _sigmoid(a0_ref[...] + jnp.dot(xa, a2_ref[...], preferred_element_type=F32))
    g = jnp.dot(_sigmoid(xg), g2_ref[...], preferred_element_type=F32)
    real = jnp.logical_or(pl.program_id(0) % (LP // TS) != 0, row >= FIRST)
    k = jnp.where(real, k, 0.0)
    v = jnp.where(real, v, 0.0)
    kk = k * kkp_ref[...]
    kk = kk / jnp.maximum(jnp.sqrt(_head_sum(kk * kk, _head_ones())), 1e-12)
    ro_ref[...] = r
    wo_ref[...] = decay
    ko_ref[...] = k * (1.0 + (a - 1.0) * kap_ref[...])
    vo_ref[...] = v
    kko_ref[...] = kk
    bo_ref[...] = kk * a
    go_ref[...] = g


def rw_prep(z, mu, w0, w2, a0, a2, g2, kkp, kap):
    w = RW_W
    hs = TS // HALO_SC
    c0 = GLA_MAIN // w
    lo0 = (GLA_MAIN + 3 * w) // RW_LORA

    def col(k):
        return pl.BlockSpec((TS, w), lambda i, k=k: (i, c0 + k))

    def halo(k):
        return pl.BlockSpec((HALO_SC, w), lambda i, k=k: (jnp.maximum(i * hs - 1, 0), c0 + k))

    def full(shape):
        return pl.BlockSpec(shape, lambda i: (0, 0))

    out = pl.BlockSpec((TS, w), lambda i: (i, 0))
    return pl.pallas_call(
        _rw_prep_kernel,
        grid=(MT // TS,),
        in_specs=[col(0), col(1), col(2),
                  pl.BlockSpec((TS, RW_LORA), lambda i: (i, lo0)),
                  halo(0), halo(1), halo(2),
                  pl.BlockSpec((HALO_SC, RW_LORA), lambda i: (jnp.maximum(i * hs - 1, 0), lo0)),
                  full((3, w)), full((1, RW_LORA)),
                  full((1, w)), full((64, w)), full((1, w)), full((64, w)), full((128, w)),
                  full((1, w)), full((1, w))],
        out_specs=[out] * 7,
        out_shape=[jax.ShapeDtypeStruct((MT, w), F32)] * 7,
        compiler_params=_params(("parallel",)),
        name="rw_prep",
    )(z, z, z, z, z, z, z, z, mu[:, :3 * w].reshape(3, w), mu[:, 3 * w:], w0, w2, a0, a2, g2, kkp, kap)


RW_PAIRS = RW_HEADS // 2
RW_UNROLL = 8


def _rw_scan_kernel(r_ref, w_ref, k_ref, v_ref, kk_ref, b_ref, y_ref, s_ref, yt_ref):
    @pl.when(pl.program_id(0) == 0)
    def _():
        s_ref[...] = jnp.zeros(s_ref.shape, F32)

    yt_ref[...] = jnp.zeros(yt_ref.shape, F32)
    ones_blk = _head_ones()
    vi = lax.broadcasted_iota(jnp.int32, (RW_N, LANES), 0)
    li = lax.broadcasted_iota(jnp.int32, (RW_N, LANES), 1)
    diag = (li % RW_N == vi).astype(F32)
    lane_t = li % RW_N

    pairs = [(b, p) for b in range(BATCH) for p in range(RW_PAIRS)]

    def bsum(parts):
        out = jnp.dot(jnp.concatenate(parts, axis=0), ones_blk, preferred_element_type=F32)
        return [out[i * RW_N:(i + 1) * RW_N, :] for i in range(len(parts))]

    def steps(g, carry):
        t0 = pl.multiple_of(g * RW_UNROLL, RW_UNROLL)
        for j in range(RW_UNROLL):
            hit = lane_t == t0 + j

            def rowv(ref, b, p):
                return ref[b, pl.ds(t0, RW_UNROLL), p * LANES:(p + 1) * LANES][j:j + 1, :]

            sa = bsum([(s_ref[b, p] * rowv(kk_ref, b, p)).astype(BF16) for b, p in pairs])
            vb = bsum([(diag * rowv(v_ref, b, p)).astype(BF16) for b, p in pairs])
            sr = []
            for i, (b, p) in enumerate(pairs):
                s = s_ref[b, p] * rowv(w_ref, b, p) - sa[i] * rowv(b_ref, b, p) + vb[i] * rowv(k_ref, b, p)
                s_ref[b, p] = s
                sr.append((s * rowv(r_ref, b, p)).astype(BF16))
            yb = bsum(sr)
            for i, (b, p) in enumerate(pairs):
                yt_ref[b, p] = jnp.where(hit, yb[i], yt_ref[b, p])
        return carry

    lax.fori_loop(0, RW_TB // RW_UNROLL, steps, 0)

    zeros = jnp.zeros((LANES - RW_N, LANES), F32)
    for b in range(BATCH):
        for p in range(RW_PAIRS):
            yt = jnp.concatenate([yt_ref[b, p], zeros], axis=0).T
            lo = yt[0:RW_TB, :]
            hi = pltpu.roll(yt[RW_N:RW_N + RW_TB, :], RW_N, 1)
            lane = lax.broadcasted_iota(jnp.int32, (RW_TB, LANES), 1)
            y_ref[b, :, p * LANES:(p + 1) * LANES] = jnp.where(lane < RW_N, lo, hi)


def rw_scan(r, w, k, v, kk, bb):
    spec = pl.BlockSpec((BATCH, RW_TB, RW_W), lambda t: (0, t, 0))
    args = [x.reshape(BATCH, LP, RW_W) for x in (r, w, k, v, kk, bb)]
    return pl.pallas_call(
        _rw_scan_kernel,
        grid=(LP // RW_TB,),
        in_specs=[spec] * 6,
        out_specs=spec,
        out_shape=jax.ShapeDtypeStruct((BATCH, LP, RW_W), F32),
        scratch_shapes=[pltpu.VMEM((BATCH, RW_PAIRS, RW_N, LANES), F32),
                        pltpu.VMEM((BATCH, RW_PAIRS, RW_N, LANES), F32)],
        compiler_params=_params(("arbitrary",)),
        name="rw_scan",
    )(*args).reshape(MT, RW_W)


def _rw_post_kernel(y_ref, r_ref, k_ref, v_ref, g_ref, lg_ref, lb_ref, rk_ref, o_ref):
    ones_blk = _head_ones()
    y = y_ref[...]
    mu = _head_sum(y, ones_blk) * (1.0 / RW_N)
    d = y - mu
    var = _head_sum(d * d, ones_blk) * (1.0 / RW_N)
    yn = d * lax.rsqrt(var + RW_GN_EPS) * lg_ref[...] + lb_ref[...]
    bonus = _head_sum(r_ref[...] * k_ref[...] * rk_ref[...], ones_blk) * v_ref[...]
    o_ref[...] = ((yn + bonus) * g_ref[...]).astype(BF16)


def rw_post(y, r, k, v, g, lg, lb, rk):
    blk = pl.BlockSpec((TS, RW_W), lambda i: (i, 0))
    par = pl.BlockSpec((1, RW_W), lambda i: (0, 0))
    return pl.pallas_call(
        _rw_post_kernel,
        grid=(MT // TS,),
        in_specs=[blk] * 5 + [par] * 3,
        out_specs=blk,
        out_shape=jax.ShapeDtypeStruct((MT, RW_W), BF16),
        compiler_params=_params(("parallel",)),
        name="rw_post",
    )(y, r, k, v, g, lg, lb, rk)


CD_WROWS = 256


def _cd_weight_kernel(w_ref, o_ref):
    o_ref[:, 0:GLA_MAIN] = w_ref[:, 0:GLA_MAIN].astype(BF16)
    o_ref[:, GLA_MAIN:GLA_MAIN + RW_COLS] = w_ref[:, GLA_COLS:GLA_COLS + RW_COLS].astype(BF16)
    lane = lax.broadcasted_iota(jnp.int32, (CD_WROWS, GLR_PAD), 1)
    glr = jnp.where(lane < GLA_RANK, w_ref[:, GLA_MAIN:GLA_MAIN + GLR_PAD], 0.0)
    o_ref[:, GLA_MAIN + RW_COLS:] = glr.astype(BF16)


def cd_weight(w):
    cols = w.shape[1]
    return pl.pallas_call(
        _cd_weight_kernel,
        grid=(D_MODEL // CD_WROWS,),
        in_specs=[pl.BlockSpec((CD_WROWS, cols), lambda i: (i, 0))],
        out_specs=pl.BlockSpec((CD_WROWS, CD_COLS), lambda i: (i, 0)),
        out_shape=jax.ShapeDtypeStruct((D_MODEL, CD_COLS), BF16),
        compiler_params=_params(("parallel",)),
        name="cd_weight",
    )(w)


def kernel(x, meta, ab_w_in, ab_conf_dw, ab_conf_dw_b, ab_conf_ln_g, ab_conf_ln_b, ab_sc_dw, ab_w_out,
           cd_w_in, cd_gla_w2, cd_gla_b, cd_gla_norm_g, cd_rw_mu, cd_rw_w0, cd_rw_w2, cd_rw_a0, cd_rw_a2,
           cd_rw_g2, cd_rw_kk, cd_rw_ka, cd_rw_rk, cd_rw_ln_g, cd_rw_ln_b, cd_w_out, norm_mix, norm_ffn,
           ffn_w_up, ffn_dw, ffn_w_down, norm_final):
    bs = x.shape[0]
    h = jnp.concatenate([jnp.zeros((bs, PAD, D_MODEL), x.dtype),
                         jnp.broadcast_to(meta[None].astype(x.dtype), (bs, N_META, D_MODEL)), x], axis=1)
    h = h.reshape(MT, D_MODEL)
    row2 = lambda p: p.reshape(1, -1)
    g_final = row2(norm_final)

    z = norm_matmul(h, row2(norm_mix[0]), ab_w_in[0].astype(BF16), TN_AB)
    a_act, s_act = mix_ab(z, ab_conf_dw[0], row2(ab_conf_dw_b[0]), row2(ab_conf_ln_g[0]),
                          row2(ab_conf_ln_b[0]), ab_sc_dw[0])
    h = out_proj(h, a_act, s_act, ab_w_out[0].astype(BF16))
    w_up, w_down = ffn_w_up.astype(BF16), ffn_w_down.astype(BF16)
    h = conv_ffn(h, row2(norm_ffn[0]), w_up, ffn_dw, w_down, g_final, 0, False)

    z = norm_matmul(h, row2(norm_mix[1]), cd_weight(cd_w_in[0]), TN_CD)
    w2p = jnp.concatenate([cd_gla_w2[0], jnp.zeros((GLR_PAD - GLA_RANK, GLA_QK), F32)], axis=0)
    o_act = gla(z, w2p, row2(cd_gla_b[0]), row2(cd_gla_norm_g[0]))
    r, w, k, v, kk, bb, g = rw_prep(z, row2(cd_rw_mu[0]), row2(cd_rw_w0[0]), cd_rw_w2[0], row2(cd_rw_a0[0]),
                                    cd_rw_a2[0], cd_rw_g2[0], row2(cd_rw_kk[0]), row2(cd_rw_ka[0]))
    y = rw_scan(r, w, k, v, kk, bb)
    y_act = rw_post(y, r, k, v, g, row2(cd_rw_ln_g[0]), row2(cd_rw_ln_b[0]), row2(cd_rw_rk[0]))
    h = out_proj(h, o_act, y_act, cd_w_out[0].astype(BF16))
    h = conv_ffn(h, row2(norm_ffn[1]), w_up, ffn_dw, w_down, g_final, 1, True)
    return h.reshape(bs, LP, D_MODEL)[:, PAD + N_META:]
```

```python
import functools

import jax
import jax.numpy as jnp
from jax import lax
from jax.experimental import pallas as pl
from jax.experimental.pallas import tpu as pltpu

D_MODEL = 2048
BATCH = 4
SEQ = 2048
N_META = 16
CONF_W = 1024
CONF_K = 31
SC_W = 1024
SC_K = 3
AB_IN = 5 * 1024
GLA_HEADS = 4
GLA_DK = 128
GLA_DV = 256
GLA_QK = GLA_HEADS * GLA_DK
GLA_V = GLA_HEADS * GLA_DV
GLA_RANK = 16
GLA_GATE_NORM = 16.0
GLA_CHUNK = 64
GLA_COLS = 2 * GLA_QK + 2 * GLA_V + GLA_RANK
RW_HEADS = 16
RW_N = 64
RW_W = RW_HEADS * RW_N
RW_LORA = 64 + 64 + 128
RW_COLS = 3 * RW_W + RW_LORA
RW_GN_EPS = 64e-5
D_FF = 5632
EPS = 1e-6
LN_EPS = 1e-5

PAD = (-N_META) % GLA_CHUNK
LP = PAD + N_META + SEQ
MT = BATCH * LP
FIRST = PAD
TM = LP // 3
TS = LP // 6
LANES = 128
SUBLANES = 8
GLR_PAD = LANES
GLA_MAIN = 2 * GLA_QK + 2 * GLA_V
CD_COLS = GLA_MAIN + RW_COLS + GLR_PAD
TN_AB = 1280
TN_CD = 2176
TN_FF = 512
TM_LAST = 512
FF_LAST_CHUNK = 128
HALO_FF = 16
HALO_CONF = 32
HALO_SC = 8
RW_TB = 64
GLA_TB = TM
GLA_NB = 2
VMEM_LIMIT = 56 * 1024 * 1024

F32 = jnp.float32
BF16 = jnp.bfloat16


def _params(sem):
    return pltpu.CompilerParams(dimension_semantics=sem, vmem_limit_bytes=VMEM_LIMIT)


def _sigmoid(x):
    return 1.0 / (1.0 + jnp.exp(-x))


def _silu(x):
    return x * _sigmoid(x)


def _softplus(x):
    return jnp.maximum(x, 0.0) + jnp.log(1.0 + jnp.exp(-jnp.abs(x)))


def _split3(x):
    hi = x.astype(BF16)
    r = x - hi.astype(F32)
    mid = r.astype(BF16)
    lo = (r - mid.astype(F32)).astype(BF16)
    return hi, mid, lo


def _dot_x_m01(x, m01):
    return sum(jnp.dot(t, m01, preferred_element_type=F32) for t in _split3(x))


def _dot_m01_x(m01, x):
    return sum(jnp.dot(m01, t, preferred_element_type=F32) for t in _split3(x))


def _rms_rows(x, g):
    ms = jnp.mean(x * x, axis=-1, keepdims=True)
    return x * lax.rsqrt(ms + EPS) * g


NORM_CHUNK = 176


def _norm_rows_into(dst_ref, dst_row0, src_ref, g_ref, chunk=NORM_CHUNK):
    assert src_ref.shape[0] % chunk == 0 and chunk % 16 == 0

    def body(c, carry):
        r = pl.multiple_of(c * chunk, 16)
        x = src_ref[pl.ds(r, chunk), :]
        dst_ref[pl.ds(pl.multiple_of(dst_row0 + r, 16), chunk), :] = _rms_rows(x, g_ref[...]).astype(BF16)
        return carry

    lax.fori_loop(0, src_ref.shape[0] // chunk, body, 0)


def _norm_matmul_kernel(h_ref, g_ref, w_ref, o_ref, hn_ref):
    @pl.when(pl.program_id(1) == 0)
    def _():
        _norm_rows_into(hn_ref, 0, h_ref, g_ref)

    o_ref[...] = jnp.dot(hn_ref[...], w_ref[...], preferred_element_type=F32)


def norm_matmul(h, g, w, tn):
    n = w.shape[1]
    return pl.pallas_call(
        _norm_matmul_kernel,
        grid=(MT // TM, n // tn),
        in_specs=[
            pl.BlockSpec((TM, D_MODEL), lambda i, j: (i, 0)),
            pl.BlockSpec((1, D_MODEL), lambda i, j: (0, 0)),
            pl.BlockSpec((D_MODEL, tn), lambda i, j: (0, j)),
        ],
        out_specs=pl.BlockSpec((TM, tn), lambda i, j: (i, j)),
        out_shape=jax.ShapeDtypeStruct((MT, n), F32),
        scratch_shapes=[pltpu.VMEM((TM, D_MODEL), BF16)],
        compiler_params=_params(("parallel", "arbitrary")),
        name="norm_matmul",
    )(h, g, w)


def _zero_pad_rows(o_ref, tiles_per_batch):
    @pl.when(pl.program_id(0) % tiles_per_batch == 0)
    def _():
        o_ref[0:FIRST, :] = jnp.zeros((FIRST, o_ref.shape[1]), o_ref.dtype)


def _out_proj_kernel(h_ref, a1_ref, a2_ref, w1_ref, w2_ref, o_ref):
    acc = jnp.dot(a1_ref[...], w1_ref[...], preferred_element_type=F32)
    acc = acc + jnp.dot(a2_ref[...], w2_ref[...], preferred_element_type=F32)
    o_ref[...] = h_ref[...] + acc
    _zero_pad_rows(o_ref, LP // TS)


def out_proj(h, a1, a2, w):
    k1, k2 = a1.shape[1], a2.shape[1]
    assert k1 == k2 and w.shape[0] == k1 + k2
    w1 = w2 = w
    return pl.pallas_call(
        _out_proj_kernel,
        grid=(MT // TS,),
        in_specs=[
            pl.BlockSpec((TS, D_MODEL), lambda i: (i, 0)),
            pl.BlockSpec((TS, k1), lambda i: (i, 0)),
            pl.BlockSpec((TS, k2), lambda i: (i, 0)),
            pl.BlockSpec((k1, D_MODEL), lambda i: (0, 0)),
            pl.BlockSpec((k2, D_MODEL), lambda i: (1, 0)),
        ],
        out_specs=pl.BlockSpec((TS, D_MODEL), lambda i: (i, 0)),
        out_shape=jax.ShapeDtypeStruct((MT, D_MODEL), F32),
        compiler_params=_params(("parallel",)),
        name="out_proj",
    )(h, a1, a2, w1, w2)


def _ffn_kernel(h_ref, halo_ref, g_ref, wv_ref, wg_ref, dwv_ref, dwg_ref, wd_ref, gf_ref,
                o_ref, hn_ref, *, last_layer):
    j = pl.program_id(1)
    tm = h_ref.shape[0]
    chunk = FF_LAST_CHUNK if last_layer else NORM_CHUNK

    @pl.when(j == 0)
    def _():
        hn_ref[0:HALO_FF, :] = _rms_rows(halo_ref[...], g_ref[...]).astype(BF16)
        _norm_rows_into(hn_ref, HALO_FF, h_ref, g_ref, chunk)
        o_ref[...] = h_ref[...]

    hn = hn_ref[...]

    def conv(w_ref, dw_ref):
        u = jnp.dot(hn, w_ref[...], preferred_element_type=F32)
        u1 = pltpu.roll(u, 1, 0)
        u2 = pltpu.roll(u, 2, 0)
        c = u * dw_ref[2:3, :] + u1 * dw_ref[1:2, :] + u2 * dw_ref[0:1, :]
        return c[HALO_FF:, :]

    act = (_silu(conv(wg_ref, dwg_ref)) * conv(wv_ref, dwv_ref)).astype(BF16)
    for n in range(D_MODEL // TN_FF):
        cols = slice(n * TN_FF, (n + 1) * TN_FF)
        o_ref[:, cols] += jnp.dot(act, wd_ref[:, cols], preferred_element_type=F32)

    @pl.when(j == pl.num_programs(1) - 1)
    def _():
        if last_layer:
            def body(c, carry):
                rows = pl.ds(pl.multiple_of(c * chunk, 16), chunk)
                o_ref[rows, :] = _rms_rows(o_ref[rows, :], gf_ref[...])
                return carry

            lax.fori_loop(0, tm // chunk, body, 0)
        else:
            _zero_pad_rows(o_ref, LP // TM)


def conv_ffn(h, g, w_up, dw, w_down, g_final, layer, last_layer):
    nj = D_FF // TN_FF
    if last_layer:
        tm, rows_out = TM_LAST, BATCH * SEQ
        per_batch = SEQ // tm

        def row0(i):
            return pl.multiple_of((i // per_batch) * LP + (PAD + N_META) + (i % per_batch) * tm, HALO_FF)

        h_spec = pl.BlockSpec((pl.Element(tm), pl.Element(D_MODEL)), lambda i, j: (row0(i), 0))
        halo_spec = pl.BlockSpec((pl.Element(HALO_FF), pl.Element(D_MODEL)),
                                 lambda i, j: (pl.multiple_of(row0(i) - HALO_FF, HALO_FF), 0))
    else:
        tm, rows_out = TM, MT
        halo_blocks = tm // HALO_FF
        h_spec = pl.BlockSpec((tm, D_MODEL), lambda i, j: (i, 0))
        halo_spec = pl.BlockSpec((HALO_FF, D_MODEL), lambda i, j: (jnp.maximum(i * halo_blocks - 1, 0), 0))
    return pl.pallas_call(
        functools.partial(_ffn_kernel, last_layer=last_layer),
        grid=(rows_out // tm, nj),
        in_specs=[
            h_spec,
            halo_spec,
            pl.BlockSpec((1, D_MODEL), lambda i, j: (0, 0)),
            pl.BlockSpec((None, D_MODEL, TN_FF), lambda i, j: (layer, 0, j)),
            pl.BlockSpec((None, D_MODEL, TN_FF), lambda i, j: (layer, 0, nj + j)),
            pl.BlockSpec((None, 3, TN_FF), lambda i, j: (layer, 0, j)),
            pl.BlockSpec((None, 3, TN_FF), lambda i, j: (layer, 0, nj + j)),
            pl.BlockSpec((None, TN_FF, D_MODEL), lambda i, j: (layer, j, 0)),
            pl.BlockSpec((1, D_MODEL), lambda i, j: (0, 0)),
        ],
        out_specs=pl.BlockSpec((tm, D_MODEL), lambda i, j: (i, 0)),
        out_shape=jax.ShapeDtypeStruct((rows_out, D_MODEL), F32),
        scratch_shapes=[pltpu.VMEM((HALO_FF + tm, D_MODEL), BF16)],
        compiler_params=_params(("parallel", "arbitrary")),
        name="conv_ffn",
    )(h, h, g, w_up, w_up, dw, dw, w_down, g_final)


CONF_RB = 88


def _mix_ab_kernel(av_ref, ag_ref, sb_ref, sc_ref, sx_ref, hav_ref, hag_ref, hsc_ref, hsx_ref,
                   cw_ref, cb_ref, lg_ref, lb_ref, sw_ref, a_ref, s_ref, xs_ref, sh_ref, ac_ref, cx_ref):
    xs_ref[0:HALO_CONF, :] = hav_ref[...] * _sigmoid(hag_ref[...])
    xs_ref[HALO_CONF:HALO_CONF + TS, :] = av_ref[...] * _sigmoid(ag_ref[...])
    base = HALO_CONF - (CONF_K - 1)

    sub = SUBLANES
    sh_rows = HALO_CONF + TS
    xs_ref[sh_rows:, :] = jnp.zeros((sub, CONF_W), F32)

    def lane_chunk(c, carry):
        cols = pl.ds(pl.multiple_of(c * LANES, LANES), LANES)
        for s in range(sub):
            sh_ref[s] = xs_ref[pl.ds(s, sh_rows), cols]
        for rb in range(TS // CONF_RB):
            r0 = rb * CONF_RB
            acc = jnp.broadcast_to(cb_ref[:, cols], (CONF_RB, LANES))
            for t in range(CONF_K):
                off = base + t
                acc = acc + sh_ref[off % sub, pl.ds(r0 + off - off % sub, CONF_RB), :] * cw_ref[pl.ds(t, 1), cols]
            ac_ref[pl.ds(r0, CONF_RB), cols] = acc
        return carry

    lax.fori_loop(0, CONF_W // LANES, lane_chunk, 0)

    def norm_rows(c, carry):
        rows = pl.ds(pl.multiple_of(c * NORM_CHUNK, 16), NORM_CHUNK)
        a = ac_ref[rows, :]
        mu = jnp.mean(a, axis=-1, keepdims=True)
        d = a - mu
        var = jnp.mean(d * d, axis=-1, keepdims=True)
        y = d * lax.rsqrt(var + LN_EPS) * lg_ref[...] + lb_ref[...]
        a_ref[rows, :] = _silu(y).astype(BF16)
        return carry

    lax.fori_loop(0, TS // NORM_CHUNK, norm_rows, 0)

    cx_ref[0:HALO_SC, :] = hsc_ref[...] * hsx_ref[...]
    cx_ref[HALO_SC:, :] = sc_ref[...] * sx_ref[...]
    cv = cx_ref[pl.ds(HALO_SC - 2, TS), :] * sw_ref[0:1, :]
    cv = cv + cx_ref[pl.ds(HALO_SC - 1, TS), :] * sw_ref[1:2, :]
    cv = cv + cx_ref[pl.ds(HALO_SC, TS), :] * sw_ref[2:3, :]
    s_ref[...] = (sb_ref[...] * cv).astype(BF16)


def mix_ab(z, conf_dw, conf_b, ln_g, ln_b, sc_dw):
    w = CONF_W
    hc, hs = TS // HALO_CONF, TS // HALO_SC

    def col(k):
        return pl.BlockSpec((TS, w), lambda i, k=k: (i, k))

    def halo(rows, per_tile, k):
        return pl.BlockSpec((rows, w), lambda i, k=k: (jnp.maximum(i * per_tile - 1, 0), k))

    def full(r):
        return pl.BlockSpec((r, w), lambda i: (0, 0))

    return pl.pallas_call(
        _mix_ab_kernel,
        grid=(MT // TS,),
        in_specs=[col(0), col(1), col(2), col(3), col(4),
                  halo(HALO_CONF, hc, 0), halo(HALO_CONF, hc, 1),
                  halo(HALO_SC, hs, 3), halo(HALO_SC, hs, 4),
                  full(CONF_K), full(1), full(1), full(1), full(SC_K)],
        out_specs=[pl.BlockSpec((TS, w), lambda i: (i, 0)), pl.BlockSpec((TS, w), lambda i: (i, 0))],
        out_shape=[jax.ShapeDtypeStruct((MT, w), BF16), jax.ShapeDtypeStruct((MT, w), BF16)],
        scratch_shapes=[pltpu.VMEM((HALO_CONF + TS + SUBLANES, w), F32),
                        pltpu.VMEM((SUBLANES, HALO_CONF + TS, LANES), F32),
                        pltpu.VMEM((TS, w), F32), pltpu.VMEM((HALO_SC + TS, w), F32)],
        compiler_params=_params(("parallel",)),
        name="mix_ab",
    )(z, z, z, z, z, z, z, z, z, conf_dw, conf_b, ln_g, ln_b, sc_dw)


def _gla_kernel(q_ref, k_ref, v_ref, go_ref, glr_ref, w2_ref, b_ref, gn_ref, o_ref, st_ref):
    c_len = GLA_CHUNK
    tb = pl.program_id(1)

    @pl.when(tb == 0)
    def _():
        st_ref[...] = jnp.zeros(st_ref.shape, F32)

    ri = lax.broadcasted_iota(jnp.int32, (c_len, c_len), 0)
    ci = lax.broadcasted_iota(jnp.int32, (c_len, c_len), 1)
    tril = ri >= ci
    tril_b = jnp.where(tril, 1.0, 0.0).astype(BF16)
    row = lax.broadcasted_iota(jnp.int32, (c_len, 1), 0)
    nt_dims = (((1,), (1,)), ((), ()))

    def chunk(c, carry):
        rows = pl.ds(pl.multiple_of(c * c_len, c_len), c_len)
        real = tb * GLA_TB + c * c_len + row >= FIRST
        for n in range(GLA_NB):
            lin = jnp.dot(glr_ref[n, rows, :], w2_ref[...], preferred_element_type=F32) + b_ref[...]
            log_a = jnp.where(real, -_softplus(-lin) / GLA_GATE_NORM, 0.0)
            cum_all = _dot_m01_x(tril_b, log_a)
            q_all = q_ref[n, rows, :] * (GLA_DK ** -0.5)
            k_all = k_ref[n, rows, :]
            v_all = v_ref[n, rows, :]
            go_all = go_ref[n, rows, :]
            outs = []
            for h in range(GLA_HEADS):
                kc = slice(h * GLA_DK, (h + 1) * GLA_DK)
                vc = slice(h * GLA_DV, (h + 1) * GLA_DV)
                cum, q, k, v = cum_all[:, kc], q_all[:, kc], k_all[:, kc], v_all[:, vc]
                last = cum[c_len - 1:c_len, :]
                q_dec = q * jnp.exp(cum)
                k_dec = k * jnp.exp(-cum)
                k_st = k * jnp.exp(last - cum)
                scores = lax.dot_general(q_dec, k_dec, nt_dims, preferred_element_type=F32)
                scores = jnp.where(tril, scores, 0.0)
                st = st_ref[n, h]
                o = jnp.dot(scores, v, preferred_element_type=F32)
                o = o + lax.dot_general(q_dec, st, nt_dims, preferred_element_type=F32)
                st_ref[n, h] = st * jnp.exp(last) + jnp.dot(v.T, k_st, preferred_element_type=F32)
                o = o * lax.rsqrt(jnp.mean(o * o, axis=-1, keepdims=True) + EPS)
                outs.append(o * gn_ref[:, vc] * _silu(go_all[:, vc]))
            o_ref[n, rows, :] = jnp.concatenate(outs, axis=-1).astype(BF16)
        return carry

    lax.fori_loop(0, GLA_TB // c_len, chunk, 0)


def gla(z, w2p, b, gn):
    z3 = z.reshape(BATCH, LP, CD_COLS)

    def zcol(width, k):
        return pl.BlockSpec((GLA_NB, GLA_TB, width), lambda b_, t, k=k: (b_, t, k))

    def full(shape):
        return pl.BlockSpec(shape, lambda b_, t: (0, 0))

    return pl.pallas_call(
        _gla_kernel,
        grid=(BATCH // GLA_NB, LP // GLA_TB),
        in_specs=[zcol(GLA_QK, 0), zcol(GLA_QK, 1), zcol(GLA_V, 1), zcol(GLA_V, 2),
                  zcol(GLR_PAD, (GLA_MAIN + RW_COLS) // GLR_PAD),
                  full((GLR_PAD, GLA_QK)), full((1, GLA_QK)), full((1, GLA_V))],
        out_specs=pl.BlockSpec((GLA_NB, GLA_TB, GLA_V), lambda b_, t: (b_, t, 0)),
        out_shape=jax.ShapeDtypeStruct((BATCH, LP, GLA_V), BF16),
        scratch_shapes=[pltpu.VMEM((GLA_NB, GLA_HEADS, GLA_DV, GLA_DK), F32)],
        compiler_params=_params(("parallel", "arbitrary")),
        name="gla",
    )(z3, z3, z3, z3, z3, w2p, b, gn).reshape(MT, GLA_V)


def _head_sum(x, ones_blk):
    parts = [_dot_x_m01(x[:, c * LANES:(c + 1) * LANES], ones_blk) for c in range(RW_W // LANES)]
    return jnp.concatenate(parts, axis=-1)


def _head_ones():
    r = lax.broadcasted_iota(jnp.int32, (LANES, LANES), 0) // RW_N
    c = lax.broadcasted_iota(jnp.int32, (LANES, LANES), 1) // RW_N
    return jnp.where(r == c, 1.0, 0.0).astype(BF16)


def _rw_prep_kernel(r_ref, k_ref, v_ref, lo_ref, hr_ref, hk_ref, hv_ref, hlo_ref, mu_ref, mulo_ref,
                    w0_ref, w2_ref, a0_ref, a2_ref, g2_ref, kkp_ref, kap_ref,
                    ro_ref, wo_ref, ko_ref, vo_ref, kko_ref, bo_ref, go_ref):
    row = lax.broadcasted_iota(jnp.int32, (TS, 1), 0)

    def shift_mix(x_ref, halo_ref, mu):
        x = x_ref[...]
        prev = jnp.where(row == 0, halo_ref[HALO_SC - 1:HALO_SC, :], pltpu.roll(x, 1, 0))
        return x + (prev - x) * mu

    r = shift_mix(r_ref, hr_ref, mu_ref[0:1, :])
    k = shift_mix(k_ref, hk_ref, mu_ref[1:2, :])
    v = shift_mix(v_ref, hv_ref, mu_ref[2:3, :])
    lo = shift_mix(lo_ref, hlo_ref, mulo_ref[...])
    xw, xa, xg = lo[:, 0:64], lo[:, 64:128], lo[:, 128:256]
    w_log = -_softplus(-(w0_ref[...] + jnp.dot(jnp.tanh(xw), w2_ref[...], preferred_element_type=F32))) - 0.5
    decay = jnp.exp(-jnp.exp(w_log))
    a = _sigmoid(a0_ref[...] + jnp.dot(xa, a2_ref[...], preferred_element_type=F32))
    g = jnp.dot(_sigmoid(xg), g2_ref[...], preferred_element_type=F32)
    real = jnp.logical_or(pl.program_id(0) % (LP // TS) != 0, row >= FIRST)
    k = jnp.where(real, k, 0.0)
    v = jnp.where(real, v, 0.0)
    kk = k * kkp_ref[...]
    kk = kk / jnp.maximum(jnp.sqrt(_head_sum(kk * kk, _head_ones())), 1e-12)
    ro_ref[...] = r
    wo_ref[...] = decay
    ko_ref[...] = k * (1.0 + (a - 1.0) * kap_ref[...])
    vo_ref[...] = v
    kko_ref[...] = kk
    bo_ref[...] = kk * a
    go_ref[...] = g


def rw_prep(z, mu, w0, w2, a0, a2, g2, kkp, kap):
    w = RW_W
    hs = TS // HALO_SC
    c0 = GLA_MAIN // w
    lo0 = (GLA_MAIN + 3 * w) // RW_LORA

    def col(k):
        return pl.BlockSpec((TS, w), lambda i, k=k: (i, c0 + k))

    def halo(k):
        return pl.BlockSpec((HALO_SC, w), lambda i, k=k: (jnp.maximum(i * hs - 1, 0), c0 + k))

    def full(shape):
        return pl.BlockSpec(shape, lambda i: (0, 0))

    out = pl.BlockSpec((TS, w), lambda i: (i, 0))
    return pl.pallas_call(
        _rw_prep_kernel,
        grid=(MT // TS,),
        in_specs=[col(0), col(1), col(2),
                  pl.BlockSpec((TS, RW_LORA), lambda i: (i, lo0)),
                  halo(0), halo(1), halo(2),
                  pl.BlockSpec((HALO_SC, RW_LORA), lambda i: (jnp.maximum(i * hs - 1, 0), lo0)),
                  full((3, w)), full((1, RW_LORA)),
                  full((1, w)), full((64, w)), full((1, w)), full((64, w)), full((128, w)),
                  full((1, w)), full((1, w))],
        out_specs=[out] * 7,
        out_shape=[jax.ShapeDtypeStruct((MT, w), F32)] * 7,
        compiler_params=_params(("parallel",)),
        name="rw_prep",
    )(z, z, z, z, z, z, z, z, mu[:, :3 * w].reshape(3, w), mu[:, 3 * w:], w0, w2, a0, a2, g2, kkp, kap)


RW_PAIRS = RW_HEADS // 2
RW_UNROLL = 8


def _rw_scan_kernel(r_ref, w_ref, k_ref, v_ref, kk_ref, b_ref, y_ref, s_ref, yt_ref):
    @pl.when(pl.program_id(0) == 0)
    def _():
        s_ref[...] = jnp.zeros(s_ref.shape, F32)

    yt_ref[...] = jnp.zeros(yt_ref.shape, F32)
    ones_blk = _head_ones()
    vi = lax.broadcasted_iota(jnp.int32, (RW_N, LANES), 0)
    li = lax.broadcasted_iota(jnp.int32, (RW_N, LANES), 1)
    diag = (li % RW_N == vi).astype(F32)
    lane_t = li % RW_N

    pairs = [(b, p) for b in range(BATCH) for p in range(RW_PAIRS)]

    def bsum(parts):
        out = jnp.dot(jnp.concatenate(parts, axis=0), ones_blk, preferred_element_type=F32)
        return [out[i * RW_N:(i + 1) * RW_N, :] for i in range(len(parts))]

    def steps(g, carry):
        t0 = pl.multiple_of(g * RW_UNROLL, RW_UNROLL)
        for j in range(RW_UNROLL):
            hit = lane_t == t0 + j

            def rowv(ref, b, p):
                return ref[b, pl.ds(t0, RW_UNROLL), p * LANES:(p + 1) * LANES][j:j + 1, :]

            sa = bsum([(s_ref[b, p] * rowv(kk_ref, b, p)).astype(BF16) for b, p in pairs])
            vb = bsum([(diag * rowv(v_ref, b, p)).astype(BF16) for b, p in pairs])
            sr = []
            for i, (b, p) in enumerate(pairs):
                s = s_ref[b, p] * rowv(w_ref, b, p) - sa[i] * rowv(b_ref, b, p) + vb[i] * rowv(k_ref, b, p)
                s_ref[b, p] = s
                sr.append((s * rowv(r_ref, b, p)).astype(BF16))
            yb = bsum(sr)
            for i, (b, p) in enumerate(pairs):
                yt_ref[b, p] = jnp.where(hit, yb[i], yt_ref[b, p])
        return carry

    lax.fori_loop(0, RW_TB // RW_UNROLL, steps, 0)

    zeros = jnp.zeros((LANES - RW_N, LANES), F32)
    for b in range(BATCH):
        for p in range(RW_PAIRS):
            yt = jnp.concatenate([yt_ref[b, p], zeros], axis=0).T
            lo = yt[0:RW_TB, :]
            hi = pltpu.roll(yt[RW_N:RW_N + RW_TB, :], RW_N, 1)
            lane = lax.broadcasted_iota(jnp.int32, (RW_TB, LANES), 1)
            y_ref[b, :, p * LANES:(p + 1) * LANES] = jnp.where(lane < RW_N, lo, hi)


def rw_scan(r, w, k, v, kk, bb):
    spec = pl.BlockSpec((BATCH, RW_TB, RW_W), lambda t: (0, t, 0))
    args = [x.reshape(BATCH, LP, RW_W) for x in (r, w, k, v, kk, bb)]
    return pl.pallas_call(
        _rw_scan_kernel,
        grid=(LP // RW_TB,),
        in_specs=[spec] * 6,
        out_specs=spec,
        out_shape=jax.ShapeDtypeStruct((BATCH, LP, RW_W), F32),
        scratch_shapes=[pltpu.VMEM((BATCH, RW_PAIRS, RW_N, LANES), F32),
                        pltpu.VMEM((BATCH, RW_PAIRS, RW_N, LANES), F32)],
        compiler_params=_params(("arbitrary",)),
        name="rw_scan",
    )(*args).reshape(MT, RW_W)


def _rw_post_kernel(y_ref, r_ref, k_ref, v_ref, g_ref, lg_ref, lb_ref, rk_ref, o_ref):
    ones_blk = _head_ones()
    y = y_ref[...]
    mu = _head_sum(y, ones_blk) * (1.0 / RW_N)
    d = y - mu
    var = _head_sum(d * d, ones_blk) * (1.0 / RW_N)
    yn = d * lax.rsqrt(var + RW_GN_EPS) * lg_ref[...] + lb_ref[...]
    bonus = _head_sum(r_ref[...] * k_ref[...] * rk_ref[...], ones_blk) * v_ref[...]
    o_ref[...] = ((yn + bonus) * g_ref[...]).astype(BF16)


def rw_post(y, r, k, v, g, lg, lb, rk):
    blk = pl.BlockSpec((TS, RW_W), lambda i: (i, 0))
    par = pl.BlockSpec((1, RW_W), lambda i: (0, 0))
    return pl.pallas_call(
        _rw_post_kernel,
        grid=(MT // TS,),
        in_specs=[blk] * 5 + [par] * 3,
        out_specs=blk,
        out_shape=jax.ShapeDtypeStruct((MT, RW_W), BF16),
        compiler_params=_params(("parallel",)),
        name="rw_post",
    )(y, r, k, v, g, lg, lb, rk)


CD_WROWS = 256


def _cd_weight_kernel(w_ref, o_ref):
    o_ref[:, 0:GLA_MAIN] = w_ref[:, 0:GLA_MAIN].astype(BF16)
    o_ref[:, GLA_MAIN:GLA_MAIN + RW_COLS] = w_ref[:, GLA_COLS:GLA_COLS + RW_COLS].astype(BF16)
    lane = lax.broadcasted_iota(jnp.int32, (CD_WROWS, GLR_PAD), 1)
    glr = jnp.where(lane < GLA_RANK, w_ref[:, GLA_MAIN:GLA_MAIN + GLR_PAD], 0.0)
    o_ref[:, GLA_MAIN + RW_COLS:] = glr.astype(BF16)


def cd_weight(w, layer):
    cols = w.shape[2]
    return pl.pallas_call(
        _cd_weight_kernel,
        grid=(D_MODEL // CD_WROWS,),
        in_specs=[pl.BlockSpec((None, CD_WROWS, cols), lambda i: (layer, i, 0))],
        out_specs=pl.BlockSpec((CD_WROWS, CD_COLS), lambda i: (i, 0)),
        out_shape=jax.ShapeDtypeStruct((D_MODEL, CD_COLS), BF16),
        compiler_params=_params(("parallel",)),
        name="cd_weight",
    )(w)


def kernel(x, meta, ab_w_in, ab_conf_dw, ab_conf_dw_b, ab_conf_ln_g, ab_conf_ln_b, ab_sc_dw, ab_w_out,
           cd_w_in, cd_gla_w2, cd_gla_b, cd_gla_norm_g, cd_rw_mu, cd_rw_w0, cd_rw_w2, cd_rw_a0, cd_rw_a2,
           cd_rw_g2, cd_rw_kk, cd_rw_ka, cd_rw_rk, cd_rw_ln_g, cd_rw_ln_b, cd_w_out, norm_mix, norm_ffn,
           ffn_w_up, ffn_dw, ffn_w_down, norm_final):
    bs = x.shape[0]
    h = jnp.concatenate([jnp.zeros((bs, PAD, D_MODEL), x.dtype),
                         jnp.broadcast_to(meta[None].astype(x.dtype), (bs, N_META, D_MODEL)), x], axis=1)
    h = h.reshape(MT, D_MODEL)
    row2 = lambda p: p.reshape(1, -1)
    g_final = row2(norm_final)

    z = norm_matmul(h, row2(norm_mix[0]), ab_w_in[0].astype(BF16), TN_AB)
    a_act, s_act = mix_ab(z, ab_conf_dw[0], row2(ab_conf_dw_b[0]), row2(ab_conf_ln_g[0]),
                          row2(ab_conf_ln_b[0]), ab_sc_dw[0])
    h = out_proj(h, a_act, s_act, ab_w_out[0].astype(BF16))
    w_up, w_down = ffn_w_up.astype(BF16), ffn_w_down.astype(BF16)
    h = conv_ffn(h, row2(norm_ffn[0]), w_up, ffn_dw, w_down, g_final, 0, False)

    z = norm_matmul(h, row2(norm_mix[1]), cd_weight(cd_w_in, 0), TN_CD)
    w2p = jnp.concatenate([cd_gla_w2[0], jnp.zeros((GLR_PAD - GLA_RANK, GLA_QK), F32)], axis=0)
    o_act = gla(z, w2p, row2(cd_gla_b[0]), row2(cd_gla_norm_g[0]))
    r, w, k, v, kk, bb, g = rw_prep(z, row2(cd_rw_mu[0]), row2(cd_rw_w0[0]), cd_rw_w2[0], row2(cd_rw_a0[0]),
                                    cd_rw_a2[0], cd_rw_g2[0], row2(cd_rw_kk[0]), row2(cd_rw_ka[0]))
    y = rw_scan(r, w, k, v, kk, bb)
    y_act = rw_post(y, r, k, v, g, row2(cd_rw_ln_g[0]), row2(cd_rw_ln_b[0]), row2(cd_rw_rk[0]))
    h = out_proj(h, o_act, y_act, cd_w_out[0].astype(BF16))
    h = conv_ffn(h, row2(norm_ffn[1]), w_up, ffn_dw, w_down, g_final, 1, True)
    return h.reshape(bs, SEQ, D_MODEL)
```

```python
import functools

import jax
import jax.numpy as jnp
from jax import lax
from jax.experimental import pallas as pl
from jax.experimental.pallas import tpu as pltpu

D_MODEL = 2048
BATCH = 4
SEQ = 2048
N_META = 16
CONF_W = 1024
CONF_K = 31
SC_W = 1024
SC_K = 3
AB_IN = 5 * 1024
GLA_HEADS = 4
GLA_DK = 128
GLA_DV = 256
GLA_QK = GLA_HEADS * GLA_DK
GLA_V = GLA_HEADS * GLA_DV
GLA_RANK = 16
GLA_GATE_NORM = 16.0
GLA_CHUNK = 64
GLA_COLS = 2 * GLA_QK + 2 * GLA_V + GLA_RANK
RW_HEADS = 16
RW_N = 64
RW_W = RW_HEADS * RW_N
RW_LORA = 64 + 64 + 128
RW_COLS = 3 * RW_W + RW_LORA
RW_GN_EPS = 64e-5
D_FF = 5632
EPS = 1e-6
LN_EPS = 1e-5

PAD = (-N_META) % GLA_CHUNK
LP = PAD + N_META + SEQ
MT = BATCH * LP
FIRST = PAD
TM = LP // 3
TS = LP // 6
LANES = 128
SUBLANES = 8
GLR_PAD = LANES
GLA_MAIN = 2 * GLA_QK + 2 * GLA_V
CD_COLS = GLA_MAIN + RW_COLS + GLR_PAD
TN_AB = 1280
TN_CD = 2176
TN_FF = 512
TM_LAST = 512
FF_LAST_CHUNK = 128
HALO_FF = 16
HALO_CONF = 32
HALO_SC = 8
RW_TB = 64
GLA_TB = TM
GLA_NB = 2
VMEM_LIMIT = 56 * 1024 * 1024

F32 = jnp.float32
BF16 = jnp.bfloat16


def _params(sem):
    return pltpu.CompilerParams(dimension_semantics=sem, vmem_limit_bytes=VMEM_LIMIT)


def _sigmoid(x):
    return 1.0 / (1.0 + jnp.exp(-x))


def _silu(x):
    return x * _sigmoid(x)


def _softplus(x):
    return jnp.maximum(x, 0.0) + jnp.log(1.0 + jnp.exp(-jnp.abs(x)))


def _split3(x):
    hi = x.astype(BF16)
    r = x - hi.astype(F32)
    mid = r.astype(BF16)
    lo = (r - mid.astype(F32)).astype(BF16)
    return hi, mid, lo


def _dot_x_m01(x, m01):
    return sum(jnp.dot(t, m01, preferred_element_type=F32) for t in _split3(x))


def _dot_m01_x(m01, x):
    return sum(jnp.dot(m01, t, preferred_element_type=F32) for t in _split3(x))


def _rms_rows(x, g):
    ms = jnp.mean(x * x, axis=-1, keepdims=True)
    return x * lax.rsqrt(ms + EPS) * g


NORM_CHUNK = 176


def _norm_rows_into(dst_ref, dst_row0, src_ref, g_ref, chunk=NORM_CHUNK):
    assert src_ref.shape[0] % chunk == 0 and chunk % 16 == 0

    def body(c, carry):
        r = pl.multiple_of(c * chunk, 16)
        x = src_ref[pl.ds(r, chunk), :]
        dst_ref[pl.ds(pl.multiple_of(dst_row0 + r, 16), chunk), :] = _rms_rows(x, g_ref[...]).astype(BF16)
        return carry

    lax.fori_loop(0, src_ref.shape[0] // chunk, body, 0)


def _norm_matmul_kernel(h_ref, g_ref, w_ref, o_ref, hn_ref):
    @pl.when(pl.program_id(1) == 0)
    def _():
        _norm_rows_into(hn_ref, 0, h_ref, g_ref)

    o_ref[...] = jnp.dot(hn_ref[...], w_ref[...], preferred_element_type=F32)


def norm_matmul(h, g, w, tn):
    n = w.shape[1]
    return pl.pallas_call(
        _norm_matmul_kernel,
        grid=(MT // TM, n // tn),
        in_specs=[
            pl.BlockSpec((TM, D_MODEL), lambda i, j: (i, 0)),
            pl.BlockSpec((1, D_MODEL), lambda i, j: (0, 0)),
            pl.BlockSpec((D_MODEL, tn), lambda i, j: (0, j)),
        ],
        out_specs=pl.BlockSpec((TM, tn), lambda i, j: (i, j)),
        out_shape=jax.ShapeDtypeStruct((MT, n), F32),
        scratch_shapes=[pltpu.VMEM((TM, D_MODEL), BF16)],
        compiler_params=_params(("parallel", "arbitrary")),
        name="norm_matmul",
    )(h, g, w)


def _zero_pad_rows(o_ref, tiles_per_batch):
    @pl.when(pl.program_id(0) % tiles_per_batch == 0)
    def _():
        o_ref[0:FIRST, :] = jnp.zeros((FIRST, o_ref.shape[1]), o_ref.dtype)


def _out_proj_kernel(h_ref, a1_ref, a2_ref, w1_ref, w2_ref, o_ref):
    acc = jnp.dot(a1_ref[...], w1_ref[...], preferred_element_type=F32)
    acc = acc + jnp.dot(a2_ref[...], w2_ref[...], preferred_element_type=F32)
    o_ref[...] = h_ref[...] + acc
    _zero_pad_rows(o_ref, LP // TS)


def out_proj(h, a1, a2, w):
    k1, k2 = a1.shape[1], a2.shape[1]
    assert k1 == k2 and w.shape[0] == k1 + k2
    w1 = w2 = w
    return pl.pallas_call(
        _out_proj_kernel,
        grid=(MT // TS,),
        in_specs=[
            pl.BlockSpec((TS, D_MODEL), lambda i: (i, 0)),
            pl.BlockSpec((TS, k1), lambda i: (i, 0)),
            pl.BlockSpec((TS, k2), lambda i: (i, 0)),
            pl.BlockSpec((k1, D_MODEL), lambda i: (0, 0)),
            pl.BlockSpec((k2, D_MODEL), lambda i: (1, 0)),
        ],
        out_specs=pl.BlockSpec((TS, D_MODEL), lambda i: (i, 0)),
        out_shape=jax.ShapeDtypeStruct((MT, D_MODEL), F32),
        compiler_params=_params(("parallel",)),
        name="out_proj",
    )(h, a1, a2, w1, w2)


def _ffn_kernel(h_ref, halo_ref, g_ref, wv_ref, wg_ref, dwv_ref, dwg_ref, wd_ref, gf_ref,
                o_ref, hn_ref, *, last_layer):
    j = pl.program_id(1)
    tm = h_ref.shape[0]
    chunk = FF_LAST_CHUNK if last_layer else NORM_CHUNK

    @pl.when(j == 0)
    def _():
        hn_ref[0:HALO_FF, :] = _rms_rows(halo_ref[...], g_ref[...]).astype(BF16)
        _norm_rows_into(hn_ref, HALO_FF, h_ref, g_ref, chunk)
        o_ref[...] = h_ref[...]

    hn = hn_ref[...]

    def conv(w_ref, dw_ref):
        u = jnp.dot(hn, w_ref[...], preferred_element_type=F32)
        u1 = pltpu.roll(u, 1, 0)
        u2 = pltpu.roll(u, 2, 0)
        c = u * dw_ref[2:3, :] + u1 * dw_ref[1:2, :] + u2 * dw_ref[0:1, :]
        return c[HALO_FF:, :]

    act = (_silu(conv(wg_ref, dwg_ref)) * conv(wv_ref, dwv_ref)).astype(BF16)
    for n in range(D_MODEL // TN_FF):
        cols = slice(n * TN_FF, (n + 1) * TN_FF)
        o_ref[:, cols] += jnp.dot(act, wd_ref[:, cols], preferred_element_type=F32)

    @pl.when(j == pl.num_programs(1) - 1)
    def _():
        if last_layer:
            def body(c, carry):
                rows = pl.ds(pl.multiple_of(c * chunk, 16), chunk)
                o_ref[rows, :] = _rms_rows(o_ref[rows, :], gf_ref[...])
                return carry

            lax.fori_loop(0, tm // chunk, body, 0)
        else:
            _zero_pad_rows(o_ref, LP // TM)


def conv_ffn(h, g, w_up, dw, w_down, g_final, layer, last_layer):
    nj = D_FF // TN_FF
    if last_layer:
        tm, rows_out = TM_LAST, BATCH * SEQ
        per_batch = SEQ // tm

        def row0(i):
            return pl.multiple_of((i // per_batch) * LP + (PAD + N_META) + (i % per_batch) * tm, HALO_FF)

        h_spec = pl.BlockSpec((pl.Element(tm), pl.Element(D_MODEL)), lambda i, j: (row0(i), 0))
        halo_spec = pl.BlockSpec((pl.Element(HALO_FF), pl.Element(D_MODEL)),
                                 lambda i, j: (pl.multiple_of(row0(i) - HALO_FF, HALO_FF), 0))
    else:
        tm, rows_out = TM, MT
        halo_blocks = tm // HALO_FF
        h_spec = pl.BlockSpec((tm, D_MODEL), lambda i, j: (i, 0))
        halo_spec = pl.BlockSpec((HALO_FF, D_MODEL), lambda i, j: (jnp.maximum(i * halo_blocks - 1, 0), 0))
    return pl.pallas_call(
        functools.partial(_ffn_kernel, last_layer=last_layer),
        grid=(rows_out // tm, nj),
        in_specs=[
            h_spec,
            halo_spec,
            pl.BlockSpec((1, D_MODEL), lambda i, j: (0, 0)),
            pl.BlockSpec((None, D_MODEL, TN_FF), lambda i, j: (layer, 0, j)),
            pl.BlockSpec((None, D_MODEL, TN_FF), lambda i, j: (layer, 0, nj + j)),
            pl.BlockSpec((None, 3, TN_FF), lambda i, j: (layer, 0, j)),
            pl.BlockSpec((None, 3, TN_FF), lambda i, j: (layer, 0, nj + j)),
            pl.BlockSpec((None, TN_FF, D_MODEL), lambda i, j: (layer, j, 0)),
            pl.BlockSpec((1, D_MODEL), lambda i, j: (0, 0)),
        ],
        out_specs=pl.BlockSpec((tm, D_MODEL), lambda i, j: (i, 0)),
        out_shape=jax.ShapeDtypeStruct((rows_out, D_MODEL), F32),
        scratch_shapes=[pltpu.VMEM((HALO_FF + tm, D_MODEL), BF16)],
        compiler_params=_params(("parallel", "arbitrary")),
        name="conv_ffn",
    )(h, h, g, w_up, w_up, dw, dw, w_down, g_final)


CONF_RB = 88


def _mix_ab_kernel(av_ref, ag_ref, sb_ref, sc_ref, sx_ref, hav_ref, hag_ref, hsc_ref, hsx_ref,
                   cw_ref, cb_ref, lg_ref, lb_ref, sw_ref, a_ref, s_ref, xs_ref, sh_ref, ac_ref, cx_ref):
    xs_ref[0:HALO_CONF, :] = hav_ref[...] * _sigmoid(hag_ref[...])
    xs_ref[HALO_CONF:HALO_CONF + TS, :] = av_ref[...] * _sigmoid(ag_ref[...])
    base = HALO_CONF - (CONF_K - 1)

    sub = SUBLANES
    sh_rows = HALO_CONF + TS
    xs_ref[sh_rows:, :] = jnp.zeros((sub, CONF_W), F32)

    def lane_chunk(c, carry):
        cols = pl.ds(pl.multiple_of(c * LANES, LANES), LANES)
        for s in range(sub):
            sh_ref[s] = xs_ref[pl.ds(s, sh_rows), cols]
        for rb in range(TS // CONF_RB):
            r0 = rb * CONF_RB
            acc = jnp.broadcast_to(cb_ref[:, cols], (CONF_RB, LANES))
            for t in range(CONF_K):
                off = base + t
                acc = acc + sh_ref[off % sub, pl.ds(r0 + off - off % sub, CONF_RB), :] * cw_ref[pl.ds(t, 1), cols]
            ac_ref[pl.ds(r0, CONF_RB), cols] = acc
        return carry

    lax.fori_loop(0, CONF_W // LANES, lane_chunk, 0)

    def norm_rows(c, carry):
        rows = pl.ds(pl.multiple_of(c * NORM_CHUNK, 16), NORM_CHUNK)
        a = ac_ref[rows, :]
        mu = jnp.mean(a, axis=-1, keepdims=True)
        d = a - mu
        var = jnp.mean(d * d, axis=-1, keepdims=True)
        y = d * lax.rsqrt(var + LN_EPS) * lg_ref[...] + lb_ref[...]
        a_ref[rows, :] = _silu(y).astype(BF16)
        return carry

    lax.fori_loop(0, TS // NORM_CHUNK, norm_rows, 0)

    cx_ref[0:HALO_SC, :] = hsc_ref[...] * hsx_ref[...]
    cx_ref[HALO_SC:, :] = sc_ref[...] * sx_ref[...]
    cv = cx_ref[pl.ds(HALO_SC - 2, TS), :] * sw_ref[0:1, :]
    cv = cv + cx_ref[pl.ds(HALO_SC - 1, TS), :] * sw_ref[1:2, :]
    cv = cv + cx_ref[pl.ds(HALO_SC, TS), :] * sw_ref[2:3, :]
    s_ref[...] = (sb_ref[...] * cv).astype(BF16)


def mix_ab(z, conf_dw, conf_b, ln_g, ln_b, sc_dw):
    w = CONF_W
    hc, hs = TS // HALO_CONF, TS // HALO_SC

    def col(k):
        return pl.BlockSpec((TS, w), lambda i, k=k: (i, k))

    def halo(rows, per_tile, k):
        return pl.BlockSpec((rows, w), lambda i, k=k: (jnp.maximum(i * per_tile - 1, 0), k))

    def full(r):
        return pl.BlockSpec((r, w), lambda i: (0, 0))

    return pl.pallas_call(
        _mix_ab_kernel,
        grid=(MT // TS,),
        in_specs=[col(0), col(1), col(2), col(3), col(4),
                  halo(HALO_CONF, hc, 0), halo(HALO_CONF, hc, 1),
                  halo(HALO_SC, hs, 3), halo(HALO_SC, hs, 4),
                  full(CONF_K), full(1), full(1), full(1), full(SC_K)],
        out_specs=[pl.BlockSpec((TS, w), lambda i: (i, 0)), pl.BlockSpec((TS, w), lambda i: (i, 0))],
        out_shape=[jax.ShapeDtypeStruct((MT, w), BF16), jax.ShapeDtypeStruct((MT, w), BF16)],
        scratch_shapes=[pltpu.VMEM((HALO_CONF + TS + SUBLANES, w), F32),
                        pltpu.VMEM((SUBLANES, HALO_CONF + TS, LANES), F32),
                        pltpu.VMEM((TS, w), F32), pltpu.VMEM((HALO_SC + TS, w), F32)],
        compiler_params=_params(("parallel",)),
        name="mix_ab",
    )(z, z, z, z, z, z, z, z, z, conf_dw, conf_b, ln_g, ln_b, sc_dw)


def _gla_kernel(q_ref, k_ref, v_ref, go_ref, glr_ref, w2_ref, b_ref, gn_ref, o_ref, st_ref):
    c_len = GLA_CHUNK
    tb = pl.program_id(1)

    @pl.when(tb == 0)
    def _():
        st_ref[...] = jnp.zeros(st_ref.shape, F32)

    ri = lax.broadcasted_iota(jnp.int32, (c_len, c_len), 0)
    ci = lax.broadcasted_iota(jnp.int32, (c_len, c_len), 1)
    tril = ri >= ci
    tril_b = jnp.where(tril, 1.0, 0.0).astype(BF16)
    row = lax.broadcasted_iota(jnp.int32, (c_len, 1), 0)
    nt_dims = (((1,), (1,)), ((), ()))

    def chunk(c, carry):
        rows = pl.ds(pl.multiple_of(c * c_len, c_len), c_len)
        real = tb * GLA_TB + c * c_len + row >= FIRST
        for n in range(GLA_NB):
            lin = jnp.dot(glr_ref[n, rows, :], w2_ref[...], preferred_element_type=F32) + b_ref[...]
            log_a = jnp.where(real, -_softplus(-lin) / GLA_GATE_NORM, 0.0)
            cum_all = _dot_m01_x(tril_b, log_a)
            q_all = q_ref[n, rows, :] * (GLA_DK ** -0.5)
            k_all = k_ref[n, rows, :]
            v_all = v_ref[n, rows, :]
            go_all = go_ref[n, rows, :]
            outs = []
            for h in range(GLA_HEADS):
                kc = slice(h * GLA_DK, (h + 1) * GLA_DK)
                vc = slice(h * GLA_DV, (h + 1) * GLA_DV)
                cum, q, k, v = cum_all[:, kc], q_all[:, kc], k_all[:, kc], v_all[:, vc]
                last = cum[c_len - 1:c_len, :]
                q_dec = q * jnp.exp(cum)
                k_dec = k * jnp.exp(-cum)
                k_st = k * jnp.exp(last - cum)
                scores = lax.dot_general(q_dec, k_dec, nt_dims, preferred_element_type=F32)
                scores = jnp.where(tril, scores, 0.0)
                st = st_ref[n, h]
                o = jnp.dot(scores, v, preferred_element_type=F32)
                o = o + lax.dot_general(q_dec, st, nt_dims, preferred_element_type=F32)
                st_ref[n, h] = st * jnp.exp(last) + jnp.dot(v.T, k_st, preferred_element_type=F32)
                o = o * lax.rsqrt(jnp.mean(o * o, axis=-1, keepdims=True) + EPS)
                outs.append(o * gn_ref[:, vc] * _silu(go_all[:, vc]))
            o_ref[n, rows, :] = jnp.concatenate(outs, axis=-1).astype(BF16)
        return carry

    lax.fori_loop(0, GLA_TB // c_len, chunk, 0)


def gla(z, w2p, b, gn):
    z3 = z.reshape(BATCH, LP, CD_COLS)

    def zcol(width, k):
        return pl.BlockSpec((GLA_NB, GLA_TB, width), lambda b_, t, k=k: (b_, t, k))

    def full(shape):
        return pl.BlockSpec(shape, lambda b_, t: (0, 0))

    return pl.pallas_call(
        _gla_kernel,
        grid=(BATCH // GLA_NB, LP // GLA_TB),
        in_specs=[zcol(GLA_QK, 0), zcol(GLA_QK, 1), zcol(GLA_V, 1), zcol(GLA_V, 2),
                  zcol(GLR_PAD, (GLA_MAIN + RW_COLS) // GLR_PAD),
                  full((GLR_PAD, GLA_QK)), full((1, GLA_QK)), full((1, GLA_V))],
        out_specs=pl.BlockSpec((GLA_NB, GLA_TB, GLA_V), lambda b_, t: (b_, t, 0)),
        out_shape=jax.ShapeDtypeStruct((BATCH, LP, GLA_V), BF16),
        scratch_shapes=[pltpu.VMEM((GLA_NB, GLA_HEADS, GLA_DV, GLA_DK), F32)],
        compiler_params=_params(("parallel", "arbitrary")),
        name="gla",
    )(z3, z3, z3, z3, z3, w2p, b, gn).reshape(MT, GLA_V)


def _head_sum(x, ones_blk):
    parts = [_dot_x_m01(x[:, c * LANES:(c + 1) * LANES], ones_blk) for c in range(RW_W // LANES)]
    return jnp.concatenate(parts, axis=-1)


def _head_ones():
    r = lax.broadcasted_iota(jnp.int32, (LANES, LANES), 0) // RW_N
    c = lax.broadcasted_iota(jnp.int32, (LANES, LANES), 1) // RW_N
    return jnp.where(r == c, 1.0, 0.0).astype(BF16)


def _rw_prep_kernel(r_ref, k_ref, v_ref, lo_ref, hr_ref, hk_ref, hv_ref, hlo_ref, mu_ref, mulo_ref,
                    w0_ref, w2_ref, a0_ref, a2_ref, g2_ref, kkp_ref, kap_ref,
                    ro_ref, wo_ref, ko_ref, vo_ref, kko_ref, bo_ref, go_ref):
    row = lax.broadcasted_iota(jnp.int32, (TS, 1), 0)

    def shift_mix(x_ref, halo_ref, mu):
        x = x_ref[...]
        prev = jnp.where(row == 0, halo_ref[HALO_SC - 1:HALO_SC, :], pltpu.roll(x, 1, 0))
        return x + (prev - x) * mu

    r = shift_mix(r_ref, hr_ref, mu_ref[0:1, :])
    k = shift_mix(k_ref, hk_ref, mu_ref[1:2, :])
    v = shift_mix(v_ref, hv_ref, mu_ref[2:3, :])
    lo = shift_mix(lo_ref, hlo_ref, mulo_ref[...])
    xw, xa, xg = lo[:, 0:64], lo[:, 64:128], lo[:, 128:256]
    w_log = -_softplus(-(w0_ref[...] + jnp.dot(jnp.tanh(xw), w2_ref[...], preferred_element_type=F32))) - 0.5
    decay = jnp.exp(-jnp.exp(w_log))
    a = _sigmoid(a0_ref[...] + jnp.dot(xa, a2_ref[...], preferred_element_type=F32))
    g = jnp.dot(_sigmoid(xg), g2_ref[...], preferred_element_type=F32)
    real = jnp.logical_or(pl.program_id(0) % (LP // TS) != 0, row >= FIRST)
    k = jnp.where(real, k, 0.0)
    v = jnp.where(real, v, 0.0)
    kk = k * kkp_ref[...]
    kk = kk / jnp.maximum(jnp.sqrt(_head_sum(kk * kk, _head_ones())), 1e-12)
    ro_ref[...] = r
    wo_ref[...] = decay
    ko_ref[...] = k * (1.0 + (a - 1.0) * kap_ref[...])
    vo_ref[...] = v
    kko_ref[...] = kk
    bo_ref[...] = kk * a
    go_ref[...] = g


def rw_prep(z, mu, w0, w2, a0, a2, g2, kkp, kap):
    w = RW_W
    hs = TS // HALO_SC
    c0 = GLA_MAIN // w
    lo0 = (GLA_MAIN + 3 * w) // RW_LORA

    def col(k):
        return pl.BlockSpec((TS, w), lambda i, k=k: (i, c0 + k))

    def halo(k):
        return pl.BlockSpec((HALO_SC, w), lambda i, k=k: (jnp.maximum(i * hs - 1, 0), c0 + k))

    def full(shape):
        return pl.BlockSpec(shape, lambda i: (0, 0))

    out = pl.BlockSpec((TS, w), lambda i: (i, 0))
    return pl.pallas_call(
        _rw_prep_kernel,
        grid=(MT // TS,),
        in_specs=[col(0), col(1), col(2),
                  pl.BlockSpec((TS, RW_LORA), lambda i: (i, lo0)),
                  halo(0), halo(1), halo(2),
                  pl.BlockSpec((HALO_SC, RW_LORA), lambda i: (jnp.maximum(i * hs - 1, 0), lo0)),
                  full((3, w)), full((1, RW_LORA)),
                  full((1, w)), full((64, w)), full((1, w)), full((64, w)), full((128, w)),
                  full((1, w)), full((1, w))],
        out_specs=[out] * 7,
        out_shape=[jax.ShapeDtypeStruct((MT, w), F32)] * 7,
        compiler_params=_params(("parallel",)),
        name="rw_prep",
    )(z, z, z, z, z, z, z, z, mu[:, :3 * w].reshape(3, w), mu[:, 3 * w:], w0, w2, a0, a2, g2, kkp, kap)


RW_PAIRS = RW_HEADS // 2
RW_UNROLL = 8


def _rw_scan_kernel(r_ref, w_ref, k_ref, v_ref, kk_ref, b_ref, y_ref, s_ref, yt_ref):
    @pl.when(pl.program_id(0) == 0)
    def _():
        s_ref[...] = jnp.zeros(s_ref.shape, F32)

    yt_ref[...] = jnp.zeros(yt_ref.shape, F32)
    ones_blk = _head_ones()
    vi = lax.broadcasted_iota(jnp.int32, (RW_N, LANES), 0)
    li = lax.broadcasted_iota(jnp.int32, (RW_N, LANES), 1)
    diag = (li % RW_N == vi).astype(F32)
    lane_t = li % RW_N

    pairs = [(b, p) for b in range(BATCH) for p in range(RW_PAIRS)]

    def bsum(parts):
        out = jnp.dot(jnp.concatenate(parts, axis=0), ones_blk, preferred_element_type=F32)
        return [out[i * RW_N:(i + 1) * RW_N, :] for i in range(len(parts))]

    def steps(g, carry):
        t0 = pl.multiple_of(g * RW_UNROLL, RW_UNROLL)
        for j in range(RW_UNROLL):
            hit = lane_t == t0 + j

            def rowv(ref, b, p):
                return ref[b, pl.ds(t0, RW_UNROLL), p * LANES:(p + 1) * LANES][j:j + 1, :]

            sa = bsum([(s_ref[b, p] * rowv(kk_ref, b, p)).astype(BF16) for b, p in pairs])
            vb = bsum([(diag * rowv(v_ref, b, p)).astype(BF16) for b, p in pairs])
            sr = []
            for i, (b, p) in enumerate(pairs):
                s = s_ref[b, p] * rowv(w_ref, b, p) - sa[i] * rowv(b_ref, b, p) + vb[i] * rowv(k_ref, b, p)
                s_ref[b, p] = s
                sr.append((s * rowv(r_ref, b, p)).astype(BF16))
            yb = bsum(sr)
            for i, (b, p) in enumerate(pairs):
                yt_ref[b, p] = jnp.where(hit, yb[i], yt_ref[b, p])
        return carry

    lax.fori_loop(0, RW_TB // RW_UNROLL, steps, 0)

    zeros = jnp.zeros((LANES - RW_N, LANES), F32)
    for b in range(BATCH):
        for p in range(RW_PAIRS):
            yt = jnp.concatenate([yt_ref[b, p], zeros], axis=0).T
            lo = yt[0:RW_TB, :]
            hi = pltpu.roll(yt[RW_N:RW_N + RW_TB, :], RW_N, 1)
            lane = lax.broadcasted_iota(jnp.int32, (RW_TB, LANES), 1)
            y_ref[b, :, p * LANES:(p + 1) * LANES] = jnp.where(lane < RW_N, lo, hi)


def rw_scan(r, w, k, v, kk, bb):
    spec = pl.BlockSpec((BATCH, RW_TB, RW_W), lambda t: (0, t, 0))
    args = [x.reshape(BATCH, LP, RW_W) for x in (r, w, k, v, kk, bb)]
    return pl.pallas_call(
        _rw_scan_kernel,
        grid=(LP // RW_TB,),
        in_specs=[spec] * 6,
        out_specs=spec,
        out_shape=jax.ShapeDtypeStruct((BATCH, LP, RW_W), F32),
        scratch_shapes=[pltpu.VMEM((BATCH, RW_PAIRS, RW_N, LANES), F32),
                        pltpu.VMEM((BATCH, RW_PAIRS, RW_N, LANES), F32)],
        compiler_params=_params(("arbitrary",)),
        name="rw_scan",
    )(*args).reshape(MT, RW_W)


def _rw_post_kernel(y_ref, r_ref, k_ref, v_ref, g_ref, lg_ref, lb_ref, rk_ref, o_ref):
    ones_blk = _head_ones()
    y = y_ref[...]
    mu = _head_sum(y, ones_blk) * (1.0 / RW_N)
    d = y - mu
    var = _head_sum(d * d, ones_blk) * (1.0 / RW_N)
    yn = d * lax.rsqrt(var + RW_GN_EPS) * lg_ref[...] + lb_ref[...]
    bonus = _head_sum(r_ref[...] * k_ref[...] * rk_ref[...], ones_blk) * v_ref[...]
    o_ref[...] = ((yn + bonus) * g_ref[...]).astype(BF16)


def rw_post(y, r, k, v, g, lg, lb, rk):
    blk = pl.BlockSpec((TS, RW_W), lambda i: (i, 0))
    par = pl.BlockSpec((1, RW_W), lambda i: (0, 0))
    return pl.pallas_call(
        _rw_post_kernel,
        grid=(MT // TS,),
        in_specs=[blk] * 5 + [par] * 3,
        out_specs=blk,
        out_shape=jax.ShapeDtypeStruct((MT, RW_W), BF16),
        compiler_params=_params(("parallel",)),
        name="rw_post",
    )(y, r, k, v, g, lg, lb, rk)


CD_WROWS = 256


def _cd_weight_kernel(wt_ref, o_ref):
    pad = jnp.zeros((GLR_PAD - GLA_RANK, CD_WROWS), F32)
    perm = jnp.concatenate([wt_ref[0:GLA_MAIN, :], wt_ref[GLA_COLS:GLA_COLS + RW_COLS, :],
                            wt_ref[GLA_MAIN:GLA_COLS, :], pad], axis=0)
    o_ref[...] = perm.T.astype(BF16)


def cd_weight(w, layer):
    wt = jnp.swapaxes(w, 1, 2)
    cols = wt.shape[1]
    return pl.pallas_call(
        _cd_weight_kernel,
        grid=(D_MODEL // CD_WROWS,),
        in_specs=[pl.BlockSpec((None, cols, CD_WROWS), lambda i: (layer, 0, i))],
        out_specs=pl.BlockSpec((CD_WROWS, CD_COLS), lambda i: (i, 0)),
        out_shape=jax.ShapeDtypeStruct((D_MODEL, CD_COLS), BF16),
        compiler_params=_params(("parallel",)),
        name="cd_weight",
    )(wt)


def kernel(x, meta, ab_w_in, ab_conf_dw, ab_conf_dw_b, ab_conf_ln_g, ab_conf_ln_b, ab_sc_dw, ab_w_out,
           cd_w_in, cd_gla_w2, cd_gla_b, cd_gla_norm_g, cd_rw_mu, cd_rw_w0, cd_rw_w2, cd_rw_a0, cd_rw_a2,
           cd_rw_g2, cd_rw_kk, cd_rw_ka, cd_rw_rk, cd_rw_ln_g, cd_rw_ln_b, cd_w_out, norm_mix, norm_ffn,
           ffn_w_up, ffn_dw, ffn_w_down, norm_final):
    bs = x.shape[0]
    h = jnp.concatenate([jnp.zeros((bs, PAD, D_MODEL), x.dtype),
                         jnp.broadcast_to(meta[None].astype(x.dtype), (bs, N_META, D_MODEL)), x], axis=1)
    h = h.reshape(MT, D_MODEL)
    row2 = lambda p: p.reshape(1, -1)
    g_final = row2(norm_final)

    z = norm_matmul(h, row2(norm_mix[0]), ab_w_in[0].astype(BF16), TN_AB)
    a_act, s_act = mix_ab(z, ab_conf_dw[0], row2(ab_conf_dw_b[0]), row2(ab_conf_ln_g[0]),
                          row2(ab_conf_ln_b[0]), ab_sc_dw[0])
    h = out_proj(h, a_act, s_act, ab_w_out[0].astype(BF16))
    w_up, w_down = ffn_w_up.astype(BF16), ffn_w_down.astype(BF16)
    h = conv_ffn(h, row2(norm_ffn[0]), w_up, ffn_dw, w_down, g_final, 0, False)

    z = norm_matmul(h, row2(norm_mix[1]), cd_weight(cd_w_in, 0), TN_CD)
    w2p = jnp.concatenate([cd_gla_w2[0], jnp.zeros((GLR_PAD - GLA_RANK, GLA_QK), F32)], axis=0)
    o_act = gla(z, w2p, row2(cd_gla_b[0]), row2(cd_gla_norm_g[0]))
    r, w, k, v, kk, bb, g = rw_prep(z, row2(cd_rw_mu[0]), row2(cd_rw_w0[0]), cd_rw_w2[0], row2(cd_rw_a0[0]),
                                    cd_rw_a2[0], cd_rw_g2[0], row2(cd_rw_kk[0]), row2(cd_rw_ka[0]))
    y = rw_scan(r, w, k, v, kk, bb)
    y_act = rw_post(y, r, k, v, g, row2(cd_rw_ln_g[0]), row2(cd_rw_ln_b[0]), row2(cd_rw_rk[0]))
    h = out_proj(h, o_act, y_act, cd_w_out[0].astype(BF16))
    h = conv_ffn(h, row2(norm_ffn[1]), w_up, ffn_dw, w_down, g_final, 1, True)
    return h.reshape(bs, SEQ, D_MODEL)
```

```python
import functools

import jax
import jax.numpy as jnp
from jax import lax
from jax.experimental import pallas as pl
from jax.experimental.pallas import tpu as pltpu

D_MODEL = 2048
BATCH = 4
SEQ = 2048
N_META = 16
CONF_W = 1024
CONF_K = 31
SC_W = 1024
SC_K = 3
AB_IN = 5 * 1024
GLA_HEADS = 4
GLA_DK = 128
GLA_DV = 256
GLA_QK = GLA_HEADS * GLA_DK
GLA_V = GLA_HEADS * GLA_DV
GLA_RANK = 16
GLA_GATE_NORM = 16.0
GLA_CHUNK = 64
GLA_COLS = 2 * GLA_QK + 2 * GLA_V + GLA_RANK
RW_HEADS = 16
RW_N = 64
RW_W = RW_HEADS * RW_N
RW_LORA = 64 + 64 + 128
RW_COLS = 3 * RW_W + RW_LORA
RW_GN_EPS = 64e-5
D_FF = 5632
EPS = 1e-6
LN_EPS = 1e-5

PAD = (-N_META) % GLA_CHUNK
LP = PAD + N_META + SEQ
MT = BATCH * LP
FIRST = PAD
TM = LP // 3
TS = LP // 6
LANES = 128
SUBLANES = 8
GLR_PAD = LANES
GLA_MAIN = 2 * GLA_QK + 2 * GLA_V
CD_COLS = GLA_MAIN + RW_COLS + GLR_PAD
TN_AB = 1280
TN_CD = 2176
TN_FF = 512
TM_LAST = 512
FF_LAST_CHUNK = 128
HALO_FF = 16
HALO_CONF = 32
HALO_SC = 8
GLA_TB = TM
GLA_NB = 2
VMEM_LIMIT = 56 * 1024 * 1024

F32 = jnp.float32
BF16 = jnp.bfloat16


def _params(sem):
    return pltpu.CompilerParams(dimension_semantics=sem, vmem_limit_bytes=VMEM_LIMIT)


def _sigmoid(x):
    return 1.0 / (1.0 + jnp.exp(-x))


def _silu(x):
    return x * _sigmoid(x)


def _softplus(x):
    return jnp.maximum(x, 0.0) + jnp.log(1.0 + jnp.exp(-jnp.abs(x)))


def _split3(x):
    hi = x.astype(BF16)
    r = x - hi.astype(F32)
    mid = r.astype(BF16)
    lo = (r - mid.astype(F32)).astype(BF16)
    return hi, mid, lo


def _dot_x_m01(x, m01):
    return sum(jnp.dot(t, m01, preferred_element_type=F32) for t in _split3(x))


def _dot_m01_x(m01, x):
    return sum(jnp.dot(m01, t, preferred_element_type=F32) for t in _split3(x))


def _rms_rows(x, g):
    ms = jnp.mean(x * x, axis=-1, keepdims=True)
    return x * lax.rsqrt(ms + EPS) * g


NORM_CHUNK = 176


def _norm_rows_into(dst_ref, dst_row0, src_ref, g_ref, chunk=NORM_CHUNK):
    assert src_ref.shape[0] % chunk == 0 and chunk % 16 == 0

    def body(c, carry):
        r = pl.multiple_of(c * chunk, 16)
        x = src_ref[pl.ds(r, chunk), :]
        dst_ref[pl.ds(pl.multiple_of(dst_row0 + r, 16), chunk), :] = _rms_rows(x, g_ref[...]).astype(BF16)
        return carry

    lax.fori_loop(0, src_ref.shape[0] // chunk, body, 0)


def _norm_matmul_kernel(h_ref, g_ref, w_ref, o_ref, hn_ref):
    @pl.when(pl.program_id(1) == 0)
    def _():
        _norm_rows_into(hn_ref, 0, h_ref, g_ref)

    o_ref[...] = jnp.dot(hn_ref[...], w_ref[...], preferred_element_type=F32)


def norm_matmul(h, g, w, tn):
    n = w.shape[1]
    return pl.pallas_call(
        _norm_matmul_kernel,
        grid=(MT // TM, n // tn),
        in_specs=[
            pl.BlockSpec((TM, D_MODEL), lambda i, j: (i, 0)),
            pl.BlockSpec((1, D_MODEL), lambda i, j: (0, 0)),
            pl.BlockSpec((D_MODEL, tn), lambda i, j: (0, j)),
        ],
        out_specs=pl.BlockSpec((TM, tn), lambda i, j: (i, j)),
        out_shape=jax.ShapeDtypeStruct((MT, n), F32),
        scratch_shapes=[pltpu.VMEM((TM, D_MODEL), BF16)],
        compiler_params=_params(("parallel", "arbitrary")),
        name="norm_matmul",
    )(h, g, w)


def _zero_pad_rows(o_ref, tiles_per_batch):
    @pl.when(pl.program_id(0) % tiles_per_batch == 0)
    def _():
        o_ref[0:FIRST, :] = jnp.zeros((FIRST, o_ref.shape[1]), o_ref.dtype)


def _out_proj_kernel(h_ref, a1_ref, a2_ref, w1_ref, w2_ref, o_ref):
    acc = jnp.dot(a1_ref[...], w1_ref[...], preferred_element_type=F32)
    acc = acc + jnp.dot(a2_ref[...], w2_ref[...], preferred_element_type=F32)
    o_ref[...] = h_ref[...] + acc
    _zero_pad_rows(o_ref, LP // TS)


def out_proj(h, a1, a2, w):
    k1, k2 = a1.shape[1], a2.shape[1]
    assert k1 == k2 and w.shape[0] == k1 + k2
    w1 = w2 = w
    return pl.pallas_call(
        _out_proj_kernel,
        grid=(MT // TS,),
        in_specs=[
            pl.BlockSpec((TS, D_MODEL), lambda i: (i, 0)),
            pl.BlockSpec((TS, k1), lambda i: (i, 0)),
            pl.BlockSpec((TS, k2), lambda i: (i, 0)),
            pl.BlockSpec((k1, D_MODEL), lambda i: (0, 0)),
            pl.BlockSpec((k2, D_MODEL), lambda i: (1, 0)),
        ],
        out_specs=pl.BlockSpec((TS, D_MODEL), lambda i: (i, 0)),
        out_shape=jax.ShapeDtypeStruct((MT, D_MODEL), F32),
        compiler_params=_params(("parallel",)),
        name="out_proj",
    )(h, a1, a2, w1, w2)


def _ffn_kernel(h_ref, halo_ref, g_ref, wv_ref, wg_ref, dwv_ref, dwg_ref, wd_ref, gf_ref,
                o_ref, hn_ref, *, last_layer):
    j = pl.program_id(1)
    tm = h_ref.shape[0]
    chunk = FF_LAST_CHUNK if last_layer else NORM_CHUNK

    @pl.when(j == 0)
    def _():
        hn_ref[0:HALO_FF, :] = _rms_rows(halo_ref[...], g_ref[...]).astype(BF16)
        _norm_rows_into(hn_ref, HALO_FF, h_ref, g_ref, chunk)
        o_ref[...] = h_ref[...]

    hn = hn_ref[...]

    def conv(w_ref, dw_ref):
        u = jnp.dot(hn, w_ref[...], preferred_element_type=F32)
        u1 = pltpu.roll(u, 1, 0)
        u2 = pltpu.roll(u, 2, 0)
        c = u * dw_ref[2:3, :] + u1 * dw_ref[1:2, :] + u2 * dw_ref[0:1, :]
        return c[HALO_FF:, :]

    act = (_silu(conv(wg_ref, dwg_ref)) * conv(wv_ref, dwv_ref)).astype(BF16)
    for n in range(D_MODEL // TN_FF):
        cols = slice(n * TN_FF, (n + 1) * TN_FF)
        o_ref[:, cols] += jnp.dot(act, wd_ref[:, cols], preferred_element_type=F32)

    @pl.when(j == pl.num_programs(1) - 1)
    def _():
        if last_layer:
            def body(c, carry):
                rows = pl.ds(pl.multiple_of(c * chunk, 16), chunk)
                o_ref[rows, :] = _rms_rows(o_ref[rows, :], gf_ref[...])
                return carry

            lax.fori_loop(0, tm // chunk, body, 0)
        else:
            _zero_pad_rows(o_ref, LP // TM)


def conv_ffn(h, g, w_up, dw, w_down, g_final, layer, last_layer):
    nj = D_FF // TN_FF
    if last_layer:
        tm, rows_out = TM_LAST, BATCH * SEQ
        per_batch = SEQ // tm

        def row0(i):
            return pl.multiple_of((i // per_batch) * LP + (PAD + N_META) + (i % per_batch) * tm, HALO_FF)

        h_spec = pl.BlockSpec((pl.Element(tm), pl.Element(D_MODEL)), lambda i, j: (row0(i), 0))
        halo_spec = pl.BlockSpec((pl.Element(HALO_FF), pl.Element(D_MODEL)),
                                 lambda i, j: (pl.multiple_of(row0(i) - HALO_FF, HALO_FF), 0))
    else:
        tm, rows_out = TM, MT
        halo_blocks = tm // HALO_FF
        h_spec = pl.BlockSpec((tm, D_MODEL), lambda i, j: (i, 0))
        halo_spec = pl.BlockSpec((HALO_FF, D_MODEL), lambda i, j: (jnp.maximum(i * halo_blocks - 1, 0), 0))
    return pl.pallas_call(
        functools.partial(_ffn_kernel, last_layer=last_layer),
        grid=(rows_out // tm, nj),
        in_specs=[
            h_spec,
            halo_spec,
            pl.BlockSpec((1, D_MODEL), lambda i, j: (0, 0)),
            pl.BlockSpec((None, D_MODEL, TN_FF), lambda i, j: (layer, 0, j)),
            pl.BlockSpec((None, D_MODEL, TN_FF), lambda i, j: (layer, 0, nj + j)),
            pl.BlockSpec((None, 3, TN_FF), lambda i, j: (layer, 0, j)),
            pl.BlockSpec((None, 3, TN_FF), lambda i, j: (layer, 0, nj + j)),
            pl.BlockSpec((None, TN_FF, D_MODEL), lambda i, j: (layer, j, 0)),
            pl.BlockSpec((1, D_MODEL), lambda i, j: (0, 0)),
        ],
        out_specs=pl.BlockSpec((tm, D_MODEL), lambda i, j: (i, 0)),
        out_shape=jax.ShapeDtypeStruct((rows_out, D_MODEL), F32),
        scratch_shapes=[pltpu.VMEM((HALO_FF + tm, D_MODEL), BF16)],
        compiler_params=_params(("parallel", "arbitrary")),
        name="conv_ffn",
    )(h, h, g, w_up, w_up, dw, dw, w_down, g_final)


CONF_RB = 88


def _mix_ab_kernel(av_ref, ag_ref, sb_ref, sc_ref, sx_ref, hav_ref, hag_ref, hsc_ref, hsx_ref,
                   cw_ref, cb_ref, lg_ref, lb_ref, sw_ref, a_ref, s_ref, xs_ref, sh_ref, ac_ref, cx_ref):
    xs_ref[0:HALO_CONF, :] = hav_ref[...] * _sigmoid(hag_ref[...])
    xs_ref[HALO_CONF:HALO_CONF + TS, :] = av_ref[...] * _sigmoid(ag_ref[...])
    base = HALO_CONF - (CONF_K - 1)

    sub = SUBLANES
    sh_rows = HALO_CONF + TS
    xs_ref[sh_rows:, :] = jnp.zeros((sub, CONF_W), F32)

    def lane_chunk(c, carry):
        cols = pl.ds(pl.multiple_of(c * LANES, LANES), LANES)
        for s in range(sub):
            sh_ref[s] = xs_ref[pl.ds(s, sh_rows), cols]
        for rb in range(TS // CONF_RB):
            r0 = rb * CONF_RB
            acc = jnp.broadcast_to(cb_ref[:, cols], (CONF_RB, LANES))
            for t in range(CONF_K):
                off = base + t
                acc = acc + sh_ref[off % sub, pl.ds(r0 + off - off % sub, CONF_RB), :] * cw_ref[pl.ds(t, 1), cols]
            ac_ref[pl.ds(r0, CONF_RB), cols] = acc
        return carry

    lax.fori_loop(0, CONF_W // LANES, lane_chunk, 0)

    def norm_rows(c, carry):
        rows = pl.ds(pl.multiple_of(c * NORM_CHUNK, 16), NORM_CHUNK)
        a = ac_ref[rows, :]
        mu = jnp.mean(a, axis=-1, keepdims=True)
        d = a - mu
        var = jnp.mean(d * d, axis=-1, keepdims=True)
        y = d * lax.rsqrt(var + LN_EPS) * lg_ref[...] + lb_ref[...]
        a_ref[rows, :] = _silu(y).astype(BF16)
        return carry

    lax.fori_loop(0, TS // NORM_CHUNK, norm_rows, 0)

    cx_ref[0:HALO_SC, :] = hsc_ref[...] * hsx_ref[...]
    cx_ref[HALO_SC:, :] = sc_ref[...] * sx_ref[...]
    cv = cx_ref[pl.ds(HALO_SC - 2, TS), :] * sw_ref[0:1, :]
    cv = cv + cx_ref[pl.ds(HALO_SC - 1, TS), :] * sw_ref[1:2, :]
    cv = cv + cx_ref[pl.ds(HALO_SC, TS), :] * sw_ref[2:3, :]
    s_ref[...] = (sb_ref[...] * cv).astype(BF16)


def mix_ab(z, conf_dw, conf_b, ln_g, ln_b, sc_dw):
    w = CONF_W
    hc, hs = TS // HALO_CONF, TS // HALO_SC

    def col(k):
        return pl.BlockSpec((TS, w), lambda i, k=k: (i, k))

    def halo(rows, per_tile, k):
        return pl.BlockSpec((rows, w), lambda i, k=k: (jnp.maximum(i * per_tile - 1, 0), k))

    def full(r):
        return pl.BlockSpec((r, w), lambda i: (0, 0))

    return pl.pallas_call(
        _mix_ab_kernel,
        grid=(MT // TS,),
        in_specs=[col(0), col(1), col(2), col(3), col(4),
                  halo(HALO_CONF, hc, 0), halo(HALO_CONF, hc, 1),
                  halo(HALO_SC, hs, 3), halo(HALO_SC, hs, 4),
                  full(CONF_K), full(1), full(1), full(1), full(SC_K)],
        out_specs=[pl.BlockSpec((TS, w), lambda i: (i, 0)), pl.BlockSpec((TS, w), lambda i: (i, 0))],
        out_shape=[jax.ShapeDtypeStruct((MT, w), BF16), jax.ShapeDtypeStruct((MT, w), BF16)],
        scratch_shapes=[pltpu.VMEM((HALO_CONF + TS + SUBLANES, w), F32),
                        pltpu.VMEM((SUBLANES, HALO_CONF + TS, LANES), F32),
                        pltpu.VMEM((TS, w), F32), pltpu.VMEM((HALO_SC + TS, w), F32)],
        compiler_params=_params(("parallel",)),
        name="mix_ab",
    )(z, z, z, z, z, z, z, z, z, conf_dw, conf_b, ln_g, ln_b, sc_dw)


def _gla_kernel(q_ref, k_ref, v_ref, go_ref, glr_ref, w2_ref, b_ref, gn_ref, o_ref, st_ref):
    c_len = GLA_CHUNK
    tb = pl.program_id(1)

    @pl.when(tb == 0)
    def _():
        st_ref[...] = jnp.zeros(st_ref.shape, F32)

    ri = lax.broadcasted_iota(jnp.int32, (c_len, c_len), 0)
    ci = lax.broadcasted_iota(jnp.int32, (c_len, c_len), 1)
    tril = ri >= ci
    tril_b = jnp.where(tril, 1.0, 0.0).astype(BF16)
    row = lax.broadcasted_iota(jnp.int32, (c_len, 1), 0)
    nt_dims = (((1,), (1,)), ((), ()))

    def chunk(c, carry):
        rows = pl.ds(pl.multiple_of(c * c_len, c_len), c_len)
        real = tb * GLA_TB + c * c_len + row >= FIRST
        for n in range(GLA_NB):
            lin = jnp.dot(glr_ref[n, rows, :], w2_ref[...], preferred_element_type=F32) + b_ref[...]
            log_a = jnp.where(real, -_softplus(-lin) / GLA_GATE_NORM, 0.0)
            cum_all = _dot_m01_x(tril_b, log_a)
            q_all = q_ref[n, rows, :] * (GLA_DK ** -0.5)
            k_all = k_ref[n, rows, :]
            v_all = v_ref[n, rows, :]
            go_all = go_ref[n, rows, :]
            outs = []
            for h in range(GLA_HEADS):
                kc = slice(h * GLA_DK, (h + 1) * GLA_DK)
                vc = slice(h * GLA_DV, (h + 1) * GLA_DV)
                cum, q, k, v = cum_all[:, kc], q_all[:, kc], k_all[:, kc], v_all[:, vc]
                last = cum[c_len - 1:c_len, :]
                q_dec = q * jnp.exp(cum)
                k_dec = k * jnp.exp(-cum)
                k_st = k * jnp.exp(last - cum)
                scores = lax.dot_general(q_dec, k_dec, nt_dims, preferred_element_type=F32)
                scores = jnp.where(tril, scores, 0.0)
                st = st_ref[n, h]
                o = jnp.dot(scores, v, preferred_element_type=F32)
                o = o + lax.dot_general(q_dec, st, nt_dims, preferred_element_type=F32)
                st_ref[n, h] = st * jnp.exp(last) + jnp.dot(v.T, k_st, preferred_element_type=F32)
                o = o * lax.rsqrt(jnp.mean(o * o, axis=-1, keepdims=True) + EPS)
                outs.append(o * gn_ref[:, vc] * _silu(go_all[:, vc]))
            o_ref[n, rows, :] = jnp.concatenate(outs, axis=-1).astype(BF16)
        return carry

    lax.fori_loop(0, GLA_TB // c_len, chunk, 0)


def gla(z, w2p, b, gn):
    z3 = z.reshape(BATCH, LP, CD_COLS)

    def zcol(width, k):
        return pl.BlockSpec((GLA_NB, GLA_TB, width), lambda b_, t, k=k: (b_, t, k))

    def full(shape):
        return pl.BlockSpec(shape, lambda b_, t: (0, 0))

    return pl.pallas_call(
        _gla_kernel,
        grid=(BATCH // GLA_NB, LP // GLA_TB),
        in_specs=[zcol(GLA_QK, 0), zcol(GLA_QK, 1), zcol(GLA_V, 1), zcol(GLA_V, 2),
                  zcol(GLR_PAD, (GLA_MAIN + RW_COLS) // GLR_PAD),
                  full((GLR_PAD, GLA_QK)), full((1, GLA_QK)), full((1, GLA_V))],
        out_specs=pl.BlockSpec((GLA_NB, GLA_TB, GLA_V), lambda b_, t: (b_, t, 0)),
        out_shape=jax.ShapeDtypeStruct((BATCH, LP, GLA_V), BF16),
        scratch_shapes=[pltpu.VMEM((GLA_NB, GLA_HEADS, GLA_DV, GLA_DK), F32)],
        compiler_params=_params(("parallel", "arbitrary")),
        name="gla",
    )(z3, z3, z3, z3, z3, w2p, b, gn).reshape(MT, GLA_V)


def _head_sum(x, ones_blk):
    parts = [_dot_x_m01(x[:, c * LANES:(c + 1) * LANES], ones_blk) for c in range(RW_W // LANES)]
    return jnp.concatenate(parts, axis=-1)


def _head_ones():
    r = lax.broadcasted_iota(jnp.int32, (LANES, LANES), 0) // RW_N
    c = lax.broadcasted_iota(jnp.int32, (LANES, LANES), 1) // RW_N
    return jnp.where(r == c, 1.0, 0.0).astype(BF16)


def _rw_prep_kernel(r_ref, k_ref, v_ref, lo_ref, hr_ref, hk_ref, hv_ref, hlo_ref, mu_ref, mulo_ref,
                    w0_ref, w2_ref, a0_ref, a2_ref, g2_ref, kkp_ref, kap_ref,
                    ro_ref, wo_ref, ko_ref, vo_ref, kko_ref, bo_ref, go_ref):
    row = lax.broadcasted_iota(jnp.int32, (TS, 1), 0)

    def shift_mix(x_ref, halo_ref, mu):
        x = x_ref[...]
        prev = jnp.where(row == 0, halo_ref[HALO_SC - 1:HALO_SC, :], pltpu.roll(x, 1, 0))
        return x + (prev - x) * mu

    r = shift_mix(r_ref, hr_ref, mu_ref[0:1, :])
    k = shift_mix(k_ref, hk_ref, mu_ref[1:2, :])
    v = shift_mix(v_ref, hv_ref, mu_ref[2:3, :])
    lo = shift_mix(lo_ref, hlo_ref, mulo_ref[...])
    xw, xa, xg = lo[:, 0:64], lo[:, 64:128], lo[:, 128:256]
    w_log = -_softplus(-(w0_ref[...] + jnp.dot(jnp.tanh(xw), w2_ref[...], preferred_element_type=F32))) - 0.5
    log_decay = -jnp.exp(w_log)
    a = _sigmoid(a0_ref[...] + jnp.dot(xa, a2_ref[...], preferred_element_type=F32))
    g = jnp.dot(_sigmoid(xg), g2_ref[...], preferred_element_type=F32)
    real = jnp.logical_or(pl.program_id(0) % (LP // TS) != 0, row >= FIRST)
    k = jnp.where(real, k, 0.0)
    v = jnp.where(real, v, 0.0)
    kk = k * kkp_ref[...]
    kk = kk / jnp.maximum(jnp.sqrt(_head_sum(kk * kk, _head_ones())), 1e-12)
    ro_ref[...] = r
    wo_ref[...] = log_decay
    ko_ref[...] = k * (1.0 + (a - 1.0) * kap_ref[...])
    vo_ref[...] = v
    kko_ref[...] = kk
    bo_ref[...] = kk * a
    go_ref[...] = g


def rw_prep(z, mu, w0, w2, a0, a2, g2, kkp, kap):
    w = RW_W
    hs = TS // HALO_SC
    c0 = GLA_MAIN // w
    lo0 = (GLA_MAIN + 3 * w) // RW_LORA

    def col(k):
        return pl.BlockSpec((TS, w), lambda i, k=k: (i, c0 + k))

    def halo(k):
        return pl.BlockSpec((HALO_SC, w), lambda i, k=k: (jnp.maximum(i * hs - 1, 0), c0 + k))

    def full(shape):
        return pl.BlockSpec(shape, lambda i: (0, 0))

    out = pl.BlockSpec((TS, w), lambda i: (i, 0))
    return pl.pallas_call(
        _rw_prep_kernel,
        grid=(MT // TS,),
        in_specs=[col(0), col(1), col(2),
                  pl.BlockSpec((TS, RW_LORA), lambda i: (i, lo0)),
                  halo(0), halo(1), halo(2),
                  pl.BlockSpec((HALO_SC, RW_LORA), lambda i: (jnp.maximum(i * hs - 1, 0), lo0)),
                  full((3, w)), full((1, RW_LORA)),
                  full((1, w)), full((64, w)), full((1, w)), full((64, w)), full((128, w)),
                  full((1, w)), full((1, w))],
        out_specs=[out] * 7,
        out_shape=[jax.ShapeDtypeStruct((MT, w), F32)] * 7,
        compiler_params=_params(("parallel",)),
        name="rw_prep",
    )(z, z, z, z, z, z, z, z, mu[:, :3 * w].reshape(3, w), mu[:, 3 * w:], w0, w2, a0, a2, g2, kkp, kap)


RW_PAIRS = RW_HEADS // 2
RW_C = 64


def _rw_chunk_kernel(r_ref, lw_ref, k_ref, v_ref, kk_ref, b_ref, y_ref, s_ref):
    c = pl.program_id(1)

    @pl.when(c == 0)
    def _():
        s_ref[...] = jnp.zeros(s_ref.shape, F32)

    ri = lax.broadcasted_iota(jnp.int32, (RW_C, RW_C), 0)
    ci = lax.broadcasted_iota(jnp.int32, (RW_C, RW_C), 1)
    tril_b = jnp.where(ri >= ci, 1.0, 0.0).astype(BF16)
    lw = lw_ref[...]
    cum = _dot_m01_x(tril_b, lw)
    e_in, e_out, e_prev = jnp.exp(cum), jnp.exp(-cum), jnp.exp(cum - lw)
    e_last = jnp.exp(cum[RW_C - 1:RW_C, :])
    a_all = -kk_ref[...] * e_prev
    bt_all = b_ref[...] * e_out
    kt_all = k_ref[...] * e_out
    rt_all = r_ref[...] * e_in
    bh_all = bt_all * e_last
    kh_all = kt_all * e_last
    v_all = v_ref[...]

    lane = lax.broadcasted_iota(jnp.int32, (RW_C, LANES), 1)
    first = lane < RW_N
    r2 = lax.broadcasted_iota(jnp.int32, (LANES, LANES), 0)
    c2 = lax.broadcasted_iota(jnp.int32, (LANES, LANES), 1)
    strict, incl = r2 % RW_C > c2 % RW_C, r2 % RW_C >= c2 % RW_C
    eye = jnp.where(r2 == c2, 1.0, 0.0)
    zeros = jnp.zeros((LANES, LANES), F32)
    nt = (((1,), (1,)), ((), ()))

    def bd(x):
        return jnp.concatenate([jnp.where(first, x, 0.0), jnp.where(first, 0.0, x)], axis=0)

    def mm(x, y):
        return jnp.dot(x.astype(BF16), y.astype(BF16), preferred_element_type=F32)

    def cat(xs, axis):
        return jnp.concatenate(xs, axis=axis)

    prs = range(RW_PAIRS)
    cols = [slice(p * LANES, (p + 1) * LANES) for p in prs]
    a, bt, kt, rt, bh, kh, vv = ([bd(x[:, cs]) for cs in cols]
                                 for x in (a_all, bt_all, kt_all, rt_all, bh_all, kh_all, v_all))
    pm = [lax.dot_general(cat([a[p], rt[p]], 0).astype(BF16), cat([bt[p], kt[p]], 0).astype(BF16), nt,
                          preferred_element_type=F32) for p in prs]
    l_ak = [jnp.where(strict, pm[p][0:LANES, LANES:], 0.0) for p in prs]
    m_rb = [jnp.where(incl, pm[p][LANES:, 0:LANES], 0.0) for p in prs]
    m_rk = [jnp.where(incl, pm[p][LANES:, LANES:], 0.0) for p in prs]
    x = [jnp.where(strict, pm[p][0:LANES, 0:LANES], 0.0) for p in prs]
    t_inv = [eye + x[p] for p in prs]
    for _ in range(RW_C.bit_length() - 2):
        x = [mm(x[p], x[p]) for p in prs]
        t_inv = [t_inv[p] + mm(t_inv[p], x[p]) for p in prs]
    lv = [mm(l_ak[p], vv[p]) for p in prs]
    w12 = [mm(t_inv[p], cat([a[p], lv[p]], 1)) for p in prs]
    qy = [mm(cat([m_rb[p], m_rk[p]], 1), cat([w12[p], cat([zeros, vv[p]], 1)], 0)) for p in prs]
    g = [mm(w12[p][:, 0:LANES].T, bh[p]) for p in prs]
    hm = [mm(cat([w12[p][:, LANES:], vv[p]], 0).T, cat([bh[p], kh[p]], 0)) for p in prs]
    s = [s_ref[p] for p in prs]
    y = [lax.dot_general((rt[p] + qy[p][:, 0:LANES]).astype(BF16), s[p].astype(BF16), nt,
                         preferred_element_type=F32) + qy[p][:, LANES:] for p in prs]
    for p in prs:
        y_ref[:, cols[p]] = y[p][0:RW_C, :] + y[p][RW_C:, :]
    sg = [sum(jnp.dot(t, g[p].astype(BF16), preferred_element_type=F32) for t in _split3(s[p])) for p in prs]
    for p in prs:
        s_ref[p] = s[p] * e_last[:, cols[p]] + sg[p] + hm[p]


def rw_chunk(r, lw, k, v, kk, bb):
    spec = pl.BlockSpec((None, RW_C, RW_W), lambda b, c: (b, c, 0))
    args = [x.reshape(BATCH, LP, RW_W) for x in (r, lw, k, v, kk, bb)]
    return pl.pallas_call(
        _rw_chunk_kernel,
        grid=(BATCH, LP // RW_C),
        in_specs=[spec] * 6,
        out_specs=spec,
        out_shape=jax.ShapeDtypeStruct((BATCH, LP, RW_W), F32),
        scratch_shapes=[pltpu.VMEM((RW_PAIRS, LANES, LANES), F32)],
        compiler_params=_params(("parallel", "arbitrary")),
        name="rw_chunk",
    )(*args).reshape(MT, RW_W)


def _rw_post_kernel(y_ref, r_ref, k_ref, v_ref, g_ref, lg_ref, lb_ref, rk_ref, o_ref):
    ones_blk = _head_ones()
    y = y_ref[...]
    mu = _head_sum(y, ones_blk) * (1.0 / RW_N)
    d = y - mu
    var = _head_sum(d * d, ones_blk) * (1.0 / RW_N)
    yn = d * lax.rsqrt(var + RW_GN_EPS) * lg_ref[...] + lb_ref[...]
    bonus = _head_sum(r_ref[...] * k_ref[...] * rk_ref[...], ones_blk) * v_ref[...]
    o_ref[...] = ((yn + bonus) * g_ref[...]).astype(BF16)


def rw_post(y, r, k, v, g, lg, lb, rk):
    blk = pl.BlockSpec((TS, RW_W), lambda i: (i, 0))
    par = pl.BlockSpec((1, RW_W), lambda i: (0, 0))
    return pl.pallas_call(
        _rw_post_kernel,
        grid=(MT // TS,),
        in_specs=[blk] * 5 + [par] * 3,
        out_specs=blk,
        out_shape=jax.ShapeDtypeStruct((MT, RW_W), BF16),
        compiler_params=_params(("parallel",)),
        name="rw_post",
    )(y, r, k, v, g, lg, lb, rk)


CD_WROWS = 256


def _cd_weight_kernel(wt_ref, o_ref):
    pad = jnp.zeros((GLR_PAD - GLA_RANK, CD_WROWS), F32)
    perm = jnp.concatenate([wt_ref[0:GLA_MAIN, :], wt_ref[GLA_COLS:GLA_COLS + RW_COLS, :],
                            wt_ref[GLA_MAIN:GLA_COLS, :], pad], axis=0)
    o_ref[...] = perm.T.astype(BF16)


def cd_weight(w, layer):
    wt = jnp.swapaxes(w, 1, 2)
    cols = wt.shape[1]
    return pl.pallas_call(
        _cd_weight_kernel,
        grid=(D_MODEL // CD_WROWS,),
        in_specs=[pl.BlockSpec((None, cols, CD_WROWS), lambda i: (layer, 0, i))],
        out_specs=pl.BlockSpec((CD_WROWS, CD_COLS), lambda i: (i, 0)),
        out_shape=jax.ShapeDtypeStruct((D_MODEL, CD_COLS), BF16),
        compiler_params=_params(("parallel",)),
        name="cd_weight",
    )(wt)


def kernel(x, meta, ab_w_in, ab_conf_dw, ab_conf_dw_b, ab_conf_ln_g, ab_conf_ln_b, ab_sc_dw, ab_w_out,
           cd_w_in, cd_gla_w2, cd_gla_b, cd_gla_norm_g, cd_rw_mu, cd_rw_w0, cd_rw_w2, cd_rw_a0, cd_rw_a2,
           cd_rw_g2, cd_rw_kk, cd_rw_ka, cd_rw_rk, cd_rw_ln_g, cd_rw_ln_b, cd_w_out, norm_mix, norm_ffn,
           ffn_w_up, ffn_dw, ffn_w_down, norm_final):
    bs = x.shape[0]
    h = jnp.concatenate([jnp.zeros((bs, PAD, D_MODEL), x.dtype),
                         jnp.broadcast_to(meta[None].astype(x.dtype), (bs, N_META, D_MODEL)), x], axis=1)
    h = h.reshape(MT, D_MODEL)
    row2 = lambda p: p.reshape(1, -1)
    g_final = row2(norm_final)

    z = norm_matmul(h, row2(norm_mix[0]), ab_w_in[0].astype(BF16), TN_AB)
    a_act, s_act = mix_ab(z, ab_conf_dw[0], row2(ab_conf_dw_b[0]), row2(ab_conf_ln_g[0]),
                          row2(ab_conf_ln_b[0]), ab_sc_dw[0])
    h = out_proj(h, a_act, s_act, ab_w_out[0].astype(BF16))
    w_up, w_down = ffn_w_up.astype(BF16), ffn_w_down.astype(BF16)
    h = conv_ffn(h, row2(norm_ffn[0]), w_up, ffn_dw, w_down, g_final, 0, False)

    z = norm_matmul(h, row2(norm_mix[1]), cd_weight(cd_w_in, 0), TN_CD)
    w2p = jnp.concatenate([cd_gla_w2[0], jnp.zeros((GLR_PAD - GLA_RANK, GLA_QK), F32)], axis=0)
    o_act = gla(z, w2p, row2(cd_gla_b[0]), row2(cd_gla_norm_g[0]))
    r, lw, k, v, kk, bb, g = rw_prep(z, row2(cd_rw_mu[0]), row2(cd_rw_w0[0]), cd_rw_w2[0], row2(cd_rw_a0[0]),
                                    cd_rw_a2[0], cd_rw_g2[0], row2(cd_rw_kk[0]), row2(cd_rw_ka[0]))
    y = rw_chunk(r, lw, k, v, kk, bb)
    y_act = rw_post(y, r, k, v, g, row2(cd_rw_ln_g[0]), row2(cd_rw_ln_b[0]), row2(cd_rw_rk[0]))
    h = out_proj(h, o_act, y_act, cd_w_out[0].astype(BF16))
    h = conv_ffn(h, row2(norm_ffn[1]), w_up, ffn_dw, w_down, g_final, 1, True)
    return h.reshape(bs, SEQ, D_MODEL)
```

```python
import functools

import jax
import jax.numpy as jnp
from jax import lax
from jax.experimental import pallas as pl
from jax.experimental.pallas import tpu as pltpu

D_MODEL = 2048
BATCH = 4
SEQ = 2048
N_META = 16
CONF_W = 1024
CONF_K = 31
SC_W = 1024
SC_K = 3
AB_IN = 5 * 1024
GLA_HEADS = 4
GLA_DK = 128
GLA_DV = 256
GLA_QK = GLA_HEADS * GLA_DK
GLA_V = GLA_HEADS * GLA_DV
GLA_RANK = 16
GLA_GATE_NORM = 16.0
GLA_CHUNK = 64
GLA_COLS = 2 * GLA_QK + 2 * GLA_V + GLA_RANK
RW_HEADS = 16
RW_N = 64
RW_W = RW_HEADS * RW_N
RW_LORA = 64 + 64 + 128
RW_COLS = 3 * RW_W + RW_LORA
RW_GN_EPS = 64e-5
D_FF = 5632
EPS = 1e-6
LN_EPS = 1e-5

PAD = (-N_META) % GLA_CHUNK
LP = PAD + N_META + SEQ
MT = BATCH * LP
FIRST = PAD
TM = LP // 3
TS = LP // 6
LANES = 128
SUBLANES = 8
GLR_PAD = LANES
GLA_MAIN = 2 * GLA_QK + 2 * GLA_V
CD_COLS = GLA_MAIN + RW_COLS + GLR_PAD
TN_AB = 1280
TN_CD = 2176
TN_FF = 512
TM_LAST = 512
FF_LAST_CHUNK = 128
HALO_FF = 16
HALO_CONF = 32
HALO_SC = 8
GLA_TB = TM
GLA_NB = 2
VMEM_LIMIT = 56 * 1024 * 1024

F32 = jnp.float32
BF16 = jnp.bfloat16


def _params(sem):
    return pltpu.CompilerParams(dimension_semantics=sem, vmem_limit_bytes=VMEM_LIMIT)


def _sigmoid(x):
    return 1.0 / (1.0 + jnp.exp(-x))


def _silu(x):
    return x * _sigmoid(x)


def _softplus(x):
    return jnp.maximum(x, 0.0) + jnp.log(1.0 + jnp.exp(-jnp.abs(x)))


def _split3(x):
    hi = x.astype(BF16)
    r = x - hi.astype(F32)
    mid = r.astype(BF16)
    lo = (r - mid.astype(F32)).astype(BF16)
    return hi, mid, lo


def _dot_x_m01(x, m01):
    return sum(jnp.dot(t, m01, preferred_element_type=F32) for t in _split3(x))


def _dot_m01_x(m01, x):
    return sum(jnp.dot(m01, t, preferred_element_type=F32) for t in _split3(x))


def _rms_rows(x, g):
    ms = jnp.mean(x * x, axis=-1, keepdims=True)
    return x * lax.rsqrt(ms + EPS) * g


NORM_CHUNK = 176


def _norm_rows_into(dst_ref, dst_row0, src_ref, g_ref, chunk=NORM_CHUNK):
    assert src_ref.shape[0] % chunk == 0 and chunk % 16 == 0

    def body(c, carry):
        r = pl.multiple_of(c * chunk, 16)
        x = src_ref[pl.ds(r, chunk), :]
        dst_ref[pl.ds(pl.multiple_of(dst_row0 + r, 16), chunk), :] = _rms_rows(x, g_ref[...]).astype(BF16)
        return carry

    lax.fori_loop(0, src_ref.shape[0] // chunk, body, 0)


def _norm_matmul_kernel(h_ref, g_ref, w_ref, o_ref, hn_ref):
    @pl.when(pl.program_id(1) == 0)
    def _():
        _norm_rows_into(hn_ref, 0, h_ref, g_ref)

    o_ref[...] = jnp.dot(hn_ref[...], w_ref[...], preferred_element_type=F32)


def norm_matmul(h, g, w, tn):
    n = w.shape[1]
    return pl.pallas_call(
        _norm_matmul_kernel,
        grid=(MT // TM, n // tn),
        in_specs=[
            pl.BlockSpec((TM, D_MODEL), lambda i, j: (i, 0)),
            pl.BlockSpec((1, D_MODEL), lambda i, j: (0, 0)),
            pl.BlockSpec((D_MODEL, tn), lambda i, j: (0, j)),
        ],
        out_specs=pl.BlockSpec((TM, tn), lambda i, j: (i, j)),
        out_shape=jax.ShapeDtypeStruct((MT, n), F32),
        scratch_shapes=[pltpu.VMEM((TM, D_MODEL), BF16)],
        compiler_params=_params(("parallel", "arbitrary")),
        name="norm_matmul",
    )(h, g, w)


def _zero_pad_rows(o_ref, tiles_per_batch):
    @pl.when(pl.program_id(0) % tiles_per_batch == 0)
    def _():
        o_ref[0:FIRST, :] = jnp.zeros((FIRST, o_ref.shape[1]), o_ref.dtype)


def _out_proj_kernel(h_ref, a1_ref, a2_ref, w1_ref, w2_ref, o_ref):
    acc = jnp.dot(a1_ref[...], w1_ref[...], preferred_element_type=F32)
    acc = acc + jnp.dot(a2_ref[...], w2_ref[...], preferred_element_type=F32)
    o_ref[...] = h_ref[...] + acc
    _zero_pad_rows(o_ref, LP // TS)


def out_proj(h, a1, a2, w):
    k1, k2 = a1.shape[1], a2.shape[1]
    assert k1 == k2 and w.shape[0] == k1 + k2
    w1 = w2 = w
    return pl.pallas_call(
        _out_proj_kernel,
        grid=(MT // TS,),
        in_specs=[
            pl.BlockSpec((TS, D_MODEL), lambda i: (i, 0)),
            pl.BlockSpec((TS, k1), lambda i: (i, 0)),
            pl.BlockSpec((TS, k2), lambda i: (i, 0)),
            pl.BlockSpec((k1, D_MODEL), lambda i: (0, 0)),
            pl.BlockSpec((k2, D_MODEL), lambda i: (1, 0)),
        ],
        out_specs=pl.BlockSpec((TS, D_MODEL), lambda i: (i, 0)),
        out_shape=jax.ShapeDtypeStruct((MT, D_MODEL), F32),
        compiler_params=_params(("parallel",)),
        name="out_proj",
    )(h, a1, a2, w1, w2)


def _ffn_kernel(h_ref, halo_ref, g_ref, wv_ref, wg_ref, dwv_ref, dwg_ref, wd_ref, gf_ref,
                o_ref, hn_ref, *, last_layer):
    j = pl.program_id(1)
    tm = h_ref.shape[0]
    chunk = FF_LAST_CHUNK if last_layer else NORM_CHUNK

    @pl.when(j == 0)
    def _():
        hn_ref[0:HALO_FF, :] = _rms_rows(halo_ref[...], g_ref[...]).astype(BF16)
        _norm_rows_into(hn_ref, HALO_FF, h_ref, g_ref, chunk)
        o_ref[...] = h_ref[...]

    hn = hn_ref[...]

    def conv(w_ref, dw_ref):
        u = jnp.dot(hn, w_ref[...], preferred_element_type=F32)
        u1 = pltpu.roll(u, 1, 0)
        u2 = pltpu.roll(u, 2, 0)
        c = u * dw_ref[2:3, :] + u1 * dw_ref[1:2, :] + u2 * dw_ref[0:1, :]
        return c[HALO_FF:, :]

    act = (_silu(conv(wg_ref, dwg_ref)) * conv(wv_ref, dwv_ref)).astype(BF16)
    for n in range(D_MODEL // TN_FF):
        cols = slice(n * TN_FF, (n + 1) * TN_FF)
        o_ref[:, cols] += jnp.dot(act, wd_ref[:, cols], preferred_element_type=F32)

    @pl.when(j == pl.num_programs(1) - 1)
    def _():
        if last_layer:
            def body(c, carry):
                rows = pl.ds(pl.multiple_of(c * chunk, 16), chunk)
                o_ref[rows, :] = _rms_rows(o_ref[rows, :], gf_ref[...])
                return carry

            lax.fori_loop(0, tm // chunk, body, 0)
        else:
            _zero_pad_rows(o_ref, LP // TM)


def conv_ffn(h, g, w_up, dw, w_down, g_final, layer, last_layer):
    nj = D_FF // TN_FF
    if last_layer:
        tm, rows_out = TM_LAST, BATCH * SEQ
        per_batch = SEQ // tm

        def row0(i):
            return pl.multiple_of((i // per_batch) * LP + (PAD + N_META) + (i % per_batch) * tm, HALO_FF)

        h_spec = pl.BlockSpec((pl.Element(tm), pl.Element(D_MODEL)), lambda i, j: (row0(i), 0))
        halo_spec = pl.BlockSpec((pl.Element(HALO_FF), pl.Element(D_MODEL)),
                                 lambda i, j: (pl.multiple_of(row0(i) - HALO_FF, HALO_FF), 0))
    else:
        tm, rows_out = TM, MT
        halo_blocks = tm // HALO_FF
        h_spec = pl.BlockSpec((tm, D_MODEL), lambda i, j: (i, 0))
        halo_spec = pl.BlockSpec((HALO_FF, D_MODEL), lambda i, j: (jnp.maximum(i * halo_blocks - 1, 0), 0))
    return pl.pallas_call(
        functools.partial(_ffn_kernel, last_layer=last_layer),
        grid=(rows_out // tm, nj),
        in_specs=[
            h_spec,
            halo_spec,
            pl.BlockSpec((1, D_MODEL), lambda i, j: (0, 0)),
            pl.BlockSpec((None, D_MODEL, TN_FF), lambda i, j: (layer, 0, j)),
            pl.BlockSpec((None, D_MODEL, TN_FF), lambda i, j: (layer, 0, nj + j)),
            pl.BlockSpec((None, 3, TN_FF), lambda i, j: (layer, 0, j)),
            pl.BlockSpec((None, 3, TN_FF), lambda i, j: (layer, 0, nj + j)),
            pl.BlockSpec((None, TN_FF, D_MODEL), lambda i, j: (layer, j, 0)),
            pl.BlockSpec((1, D_MODEL), lambda i, j: (0, 0)),
        ],
        out_specs=pl.BlockSpec((tm, D_MODEL), lambda i, j: (i, 0)),
        out_shape=jax.ShapeDtypeStruct((rows_out, D_MODEL), F32),
        scratch_shapes=[pltpu.VMEM((HALO_FF + tm, D_MODEL), BF16)],
        compiler_params=_params(("parallel", "arbitrary")),
        name="conv_ffn",
    )(h, h, g, w_up, w_up, dw, dw, w_down, g_final)


CONF_RB = 88


def _mix_ab_kernel(av_ref, ag_ref, sb_ref, sc_ref, sx_ref, hav_ref, hag_ref, hsc_ref, hsx_ref,
                   cw_ref, cb_ref, lg_ref, lb_ref, sw_ref, a_ref, s_ref, xs_ref, sh_ref, ac_ref, cx_ref):
    xs_ref[0:HALO_CONF, :] = hav_ref[...] * _sigmoid(hag_ref[...])
    xs_ref[HALO_CONF:HALO_CONF + TS, :] = av_ref[...] * _sigmoid(ag_ref[...])
    base = HALO_CONF - (CONF_K - 1)

    sub = SUBLANES
    sh_rows = HALO_CONF + TS
    xs_ref[sh_rows:, :] = jnp.zeros((sub, CONF_W), F32)

    def lane_chunk(c, carry):
        cols = pl.ds(pl.multiple_of(c * LANES, LANES), LANES)
        for s in range(sub):
            sh_ref[s] = xs_ref[pl.ds(s, sh_rows), cols]
        for rb in range(TS // CONF_RB):
            r0 = rb * CONF_RB
            acc = jnp.broadcast_to(cb_ref[:, cols], (CONF_RB, LANES))
            for t in range(CONF_K):
                off = base + t
                acc = acc + sh_ref[off % sub, pl.ds(r0 + off - off % sub, CONF_RB), :] * cw_ref[pl.ds(t, 1), cols]
            ac_ref[pl.ds(r0, CONF_RB), cols] = acc
        return carry

    lax.fori_loop(0, CONF_W // LANES, lane_chunk, 0)

    def norm_rows(c, carry):
        rows = pl.ds(pl.multiple_of(c * NORM_CHUNK, 16), NORM_CHUNK)
        a = ac_ref[rows, :]
        mu = jnp.mean(a, axis=-1, keepdims=True)
        d = a - mu
        var = jnp.mean(d * d, axis=-1, keepdims=True)
        y = d * lax.rsqrt(var + LN_EPS) * lg_ref[...] + lb_ref[...]
        a_ref[rows, :] = _silu(y).astype(BF16)
        return carry

    lax.fori_loop(0, TS // NORM_CHUNK, norm_rows, 0)

    cx_ref[0:HALO_SC, :] = hsc_ref[...] * hsx_ref[...]
    cx_ref[HALO_SC:, :] = sc_ref[...] * sx_ref[...]
    cv = cx_ref[pl.ds(HALO_SC - 2, TS), :] * sw_ref[0:1, :]
    cv = cv + cx_ref[pl.ds(HALO_SC - 1, TS), :] * sw_ref[1:2, :]
    cv = cv + cx_ref[pl.ds(HALO_SC, TS), :] * sw_ref[2:3, :]
    s_ref[...] = (sb_ref[...] * cv).astype(BF16)


def mix_ab(z, conf_dw, conf_b, ln_g, ln_b, sc_dw):
    w = CONF_W
    hc, hs = TS // HALO_CONF, TS // HALO_SC

    def col(k):
        return pl.BlockSpec((TS, w), lambda i, k=k: (i, k))

    def halo(rows, per_tile, k):
        return pl.BlockSpec((rows, w), lambda i, k=k: (jnp.maximum(i * per_tile - 1, 0), k))

    def full(r):
        return pl.BlockSpec((r, w), lambda i: (0, 0))

    return pl.pallas_call(
        _mix_ab_kernel,
        grid=(MT // TS,),
        in_specs=[col(0), col(1), col(2), col(3), col(4),
                  halo(HALO_CONF, hc, 0), halo(HALO_CONF, hc, 1),
                  halo(HALO_SC, hs, 3), halo(HALO_SC, hs, 4),
                  full(CONF_K), full(1), full(1), full(1), full(SC_K)],
        out_specs=[pl.BlockSpec((TS, w), lambda i: (i, 0)), pl.BlockSpec((TS, w), lambda i: (i, 0))],
        out_shape=[jax.ShapeDtypeStruct((MT, w), BF16), jax.ShapeDtypeStruct((MT, w), BF16)],
        scratch_shapes=[pltpu.VMEM((HALO_CONF + TS + SUBLANES, w), F32),
                        pltpu.VMEM((SUBLANES, HALO_CONF + TS, LANES), F32),
                        pltpu.VMEM((TS, w), F32), pltpu.VMEM((HALO_SC + TS, w), F32)],
        compiler_params=_params(("parallel",)),
        name="mix_ab",
    )(z, z, z, z, z, z, z, z, z, conf_dw, conf_b, ln_g, ln_b, sc_dw)


def _gla_kernel(q_ref, k_ref, v_ref, go_ref, glr_ref, w2_ref, b_ref, gn_ref, o_ref, st_ref):
    c_len = GLA_CHUNK
    tb = pl.program_id(1)

    @pl.when(tb == 0)
    def _():
        st_ref[...] = jnp.zeros(st_ref.shape, F32)

    ri = lax.broadcasted_iota(jnp.int32, (c_len, c_len), 0)
    ci = lax.broadcasted_iota(jnp.int32, (c_len, c_len), 1)
    tril = ri >= ci
    tril_b = jnp.where(tril, 1.0, 0.0).astype(BF16)
    row = lax.broadcasted_iota(jnp.int32, (c_len, 1), 0)
    nt_dims = (((1,), (1,)), ((), ()))

    def chunk(c, carry):
        rows = pl.ds(pl.multiple_of(c * c_len, c_len), c_len)
        real = tb * GLA_TB + c * c_len + row >= FIRST
        nb, hs = range(GLA_NB), range(GLA_HEADS)
        ch = [(n, h) for n in nb for h in hs]
        lin = [jnp.dot(glr_ref[n, rows, :], w2_ref[...], preferred_element_type=F32) + b_ref[...] for n in nb]
        log_a = [jnp.where(real, -_softplus(-lin[n]) / GLA_GATE_NORM, 0.0) for n in nb]
        cum_all = [_dot_m01_x(tril_b, log_a[n]) for n in nb]
        q_all = [q_ref[n, rows, :] * (GLA_DK ** -0.5) for n in nb]
        k_all = [k_ref[n, rows, :] for n in nb]
        v_all = [v_ref[n, rows, :] for n in nb]
        kc = [slice(h * GLA_DK, (h + 1) * GLA_DK) for h in hs]
        vc = [slice(h * GLA_DV, (h + 1) * GLA_DV) for h in hs]
        cum = [cum_all[n][:, kc[h]] for n, h in ch]
        last = [x[c_len - 1:c_len, :] for x in cum]
        v = [v_all[n][:, vc[h]] for n, h in ch]
        q_dec = [q_all[n][:, kc[h]] * jnp.exp(cum[i]) for i, (n, h) in enumerate(ch)]
        k_dec = [k_all[n][:, kc[h]] * jnp.exp(-cum[i]) for i, (n, h) in enumerate(ch)]
        k_st = [k_all[n][:, kc[h]] * jnp.exp(last[i] - cum[i]) for i, (n, h) in enumerate(ch)]
        scores = [jnp.where(tril, lax.dot_general(q_dec[i], k_dec[i], nt_dims, preferred_element_type=F32), 0.0)
                  for i in range(len(ch))]
        st = [st_ref[n, h] for n, h in ch]
        o = [jnp.dot(scores[i], v[i], preferred_element_type=F32) for i in range(len(ch))]
        o = [o[i] + lax.dot_general(q_dec[i], st[i], nt_dims, preferred_element_type=F32) for i in range(len(ch))]
        upd = [jnp.dot(v[i].T, k_st[i], preferred_element_type=F32) for i in range(len(ch))]
        for i, (n, h) in enumerate(ch):
            st_ref[n, h] = st[i] * jnp.exp(last[i]) + upd[i]
        o = [x * lax.rsqrt(jnp.mean(x * x, axis=-1, keepdims=True) + EPS) for x in o]
        for n in nb:
            go = go_ref[n, rows, :]
            outs = [o[n * GLA_HEADS + h] * gn_ref[:, vc[h]] * _silu(go[:, vc[h]]) for h in hs]
            o_ref[n, rows, :] = jnp.concatenate(outs, axis=-1).astype(BF16)
        return carry

    lax.fori_loop(0, GLA_TB // c_len, chunk, 0)


def gla(z, w2p, b, gn):
    z3 = z.reshape(BATCH, LP, CD_COLS)

    def zcol(width, k):
        return pl.BlockSpec((GLA_NB, GLA_TB, width), lambda b_, t, k=k: (b_, t, k))

    def full(shape):
        return pl.BlockSpec(shape, lambda b_, t: (0, 0))

    return pl.pallas_call(
        _gla_kernel,
        grid=(BATCH // GLA_NB, LP // GLA_TB),
        in_specs=[zcol(GLA_QK, 0), zcol(GLA_QK, 1), zcol(GLA_V, 1), zcol(GLA_V, 2),
                  zcol(GLR_PAD, (GLA_MAIN + RW_COLS) // GLR_PAD),
                  full((GLR_PAD, GLA_QK)), full((1, GLA_QK)), full((1, GLA_V))],
        out_specs=pl.BlockSpec((GLA_NB, GLA_TB, GLA_V), lambda b_, t: (b_, t, 0)),
        out_shape=jax.ShapeDtypeStruct((BATCH, LP, GLA_V), BF16),
        scratch_shapes=[pltpu.VMEM((GLA_NB, GLA_HEADS, GLA_DV, GLA_DK), F32)],
        compiler_params=_params(("parallel", "arbitrary")),
        name="gla",
    )(z3, z3, z3, z3, z3, w2p, b, gn).reshape(MT, GLA_V)


def _head_sum(x, ones_blk):
    parts = [_dot_x_m01(x[:, c * LANES:(c + 1) * LANES], ones_blk) for c in range(RW_W // LANES)]
    return jnp.concatenate(parts, axis=-1)


def _head_ones():
    r = lax.broadcasted_iota(jnp.int32, (LANES, LANES), 0) // RW_N
    c = lax.broadcasted_iota(jnp.int32, (LANES, LANES), 1) // RW_N
    return jnp.where(r == c, 1.0, 0.0).astype(BF16)


def _rw_prep_kernel(r_ref, k_ref, v_ref, lo_ref, hr_ref, hk_ref, hv_ref, hlo_ref, mu_ref, mulo_ref,
                    w0_ref, w2_ref, a0_ref, a2_ref, g2_ref, kkp_ref, kap_ref,
                    ro_ref, wo_ref, ko_ref, vo_ref, kko_ref, bo_ref, go_ref):
    row = lax.broadcasted_iota(jnp.int32, (TS, 1), 0)

    def shift_mix(x_ref, halo_ref, mu):
        x = x_ref[...]
        prev = jnp.where(row == 0, halo_ref[HALO_SC - 1:HALO_SC, :], pltpu.roll(x, 1, 0))
        return x + (prev - x) * mu

    r = shift_mix(r_ref, hr_ref, mu_ref[0:1, :])
    k = shift_mix(k_ref, hk_ref, mu_ref[1:2, :])
    v = shift_mix(v_ref, hv_ref, mu_ref[2:3, :])
    lo = shift_mix(lo_ref, hlo_ref, mulo_ref[...])
    xw, xa, xg = lo[:, 0:64], lo[:, 64:128], lo[:, 128:256]
    w_log = -_softplus(-(w0_ref[...] + jnp.dot(jnp.tanh(xw), w2_ref[...], preferred_element_type=F32))) - 0.5
    log_decay = -jnp.exp(w_log)
    a = _sigmoid(a0_ref[...] + jnp.dot(xa, a2_ref[...], preferred_element_type=F32))
    g = jnp.dot(_sigmoid(xg), g2_ref[...], preferred_element_type=F32)
    real = jnp.logical_or(pl.program_id(0) % (LP // TS) != 0, row >= FIRST)
    k = jnp.where(real, k, 0.0)
    v = jnp.where(real, v, 0.0)
    kk = k * kkp_ref[...]
    kk = kk / jnp.maximum(jnp.sqrt(_head_sum(kk * kk, _head_ones())), 1e-12)
    ro_ref[...] = r
    wo_ref[...] = log_decay
    ko_ref[...] = k * (1.0 + (a - 1.0) * kap_ref[...])
    vo_ref[...] = v
    kko_ref[...] = kk
    bo_ref[...] = kk * a
    go_ref[...] = g


def rw_prep(z, mu, w0, w2, a0, a2, g2, kkp, kap):
    w = RW_W
    hs = TS // HALO_SC
    c0 = GLA_MAIN // w
    lo0 = (GLA_MAIN + 3 * w) // RW_LORA

    def col(k):
        return pl.BlockSpec((TS, w), lambda i, k=k: (i, c0 + k))

    def halo(k):
        return pl.BlockSpec((HALO_SC, w), lambda i, k=k: (jnp.maximum(i * hs - 1, 0), c0 + k))

    def full(shape):
        return pl.BlockSpec(shape, lambda i: (0, 0))

    out = pl.BlockSpec((TS, w), lambda i: (i, 0))
    return pl.pallas_call(
        _rw_prep_kernel,
        grid=(MT // TS,),
        in_specs=[col(0), col(1), col(2),
                  pl.BlockSpec((TS, RW_LORA), lambda i: (i, lo0)),
                  halo(0), halo(1), halo(2),
                  pl.BlockSpec((HALO_SC, RW_LORA), lambda i: (jnp.maximum(i * hs - 1, 0), lo0)),
                  full((3, w)), full((1, RW_LORA)),
                  full((1, w)), full((64, w)), full((1, w)), full((64, w)), full((128, w)),
                  full((1, w)), full((1, w))],
        out_specs=[out] * 7,
        out_shape=[jax.ShapeDtypeStruct((MT, w), F32)] * 7,
        compiler_params=_params(("parallel",)),
        name="rw_prep",
    )(z, z, z, z, z, z, z, z, mu[:, :3 * w].reshape(3, w), mu[:, 3 * w:], w0, w2, a0, a2, g2, kkp, kap)


RW_PAIRS = RW_HEADS // 2
RW_C = 64
RW_NB = 2


def _rw_chunk_kernel(r_ref, lw_ref, k_ref, v_ref, kk_ref, b_ref, y_ref, s_ref):
    c = pl.program_id(1)

    @pl.when(c == 0)
    def _():
        s_ref[...] = jnp.zeros(s_ref.shape, F32)

    ri = lax.broadcasted_iota(jnp.int32, (RW_C, RW_C), 0)
    ci = lax.broadcasted_iota(jnp.int32, (RW_C, RW_C), 1)
    tril_b = jnp.where(ri >= ci, 1.0, 0.0).astype(BF16)
    def wide(ref):
        return jnp.concatenate([ref[n] for n in range(RW_NB)], axis=1)

    lw = wide(lw_ref)
    cum = _dot_m01_x(tril_b, lw)
    e_in, e_out, e_prev = jnp.exp(cum), jnp.exp(-cum), jnp.exp(cum - lw)
    e_last = jnp.exp(cum[RW_C - 1:RW_C, :])
    a_all = -wide(kk_ref) * e_prev
    bt_all = wide(b_ref) * e_out
    kt_all = wide(k_ref) * e_out
    rt_all = wide(r_ref) * e_in
    bh_all = bt_all * e_last
    kh_all = kt_all * e_last
    v_all = wide(v_ref)

    lane = lax.broadcasted_iota(jnp.int32, (RW_C, LANES), 1)
    first = lane < RW_N
    r2 = lax.broadcasted_iota(jnp.int32, (LANES, LANES), 0)
    c2 = lax.broadcasted_iota(jnp.int32, (LANES, LANES), 1)
    strict, incl = r2 % RW_C > c2 % RW_C, r2 % RW_C >= c2 % RW_C
    eye = jnp.where(r2 == c2, 1.0, 0.0)
    zeros = jnp.zeros((LANES, LANES), F32)
    nt = (((1,), (1,)), ((), ()))

    def bd(x):
        return jnp.concatenate([jnp.where(first, x, 0.0), jnp.where(first, 0.0, x)], axis=0)

    def mm(x, y):
        return jnp.dot(x.astype(BF16), y.astype(BF16), preferred_element_type=F32)

    def cat(xs, axis):
        return jnp.concatenate(xs, axis=axis)

    prs = range(RW_NB * RW_PAIRS)
    cols = [slice(p * LANES, (p + 1) * LANES) for p in prs]
    a, bt, kt, rt, bh, kh, vv = ([bd(x[:, cs]) for cs in cols]
                                 for x in (a_all, bt_all, kt_all, rt_all, bh_all, kh_all, v_all))
    pm = [lax.dot_general(cat([a[p], rt[p]], 0).astype(BF16), cat([bt[p], kt[p]], 0).astype(BF16), nt,
                          preferred_element_type=F32) for p in prs]
    l_ak = [jnp.where(strict, pm[p][0:LANES, LANES:], 0.0) for p in prs]
    m_rb = [jnp.where(incl, pm[p][LANES:, 0:LANES], 0.0) for p in prs]
    m_rk = [jnp.where(incl, pm[p][LANES:, LANES:], 0.0) for p in prs]
    x = [jnp.where(strict, pm[p][0:LANES, 0:LANES], 0.0) for p in prs]
    t_inv = [eye + x[p] for p in prs]
    for _ in range(RW_C.bit_length() - 2):
        x = [mm(x[p], x[p]) for p in prs]
        t_inv = [t_inv[p] + mm(t_inv[p], x[p]) for p in prs]
    lv = [mm(l_ak[p], vv[p]) for p in prs]
    w12 = [mm(t_inv[p], cat([a[p], lv[p]], 1)) for p in prs]
    qy = [mm(cat([m_rb[p], m_rk[p]], 1), cat([w12[p], cat([zeros, vv[p]], 1)], 0)) for p in prs]
    g = [mm(w12[p][:, 0:LANES].T, bh[p]) for p in prs]
    hm = [mm(cat([w12[p][:, LANES:], vv[p]], 0).T, cat([bh[p], kh[p]], 0)) for p in prs]
    s = [s_ref[p // RW_PAIRS, p % RW_PAIRS] for p in prs]
    y = [lax.dot_general((rt[p] + qy[p][:, 0:LANES]).astype(BF16), s[p].astype(BF16), nt,
                         preferred_element_type=F32) + qy[p][:, LANES:] for p in prs]
    for p in prs:
        y_ref[p // RW_PAIRS, :, cols[p % RW_PAIRS]] = y[p][0:RW_C, :] + y[p][RW_C:, :]
    sg = [sum(jnp.dot(t, g[p].astype(BF16), preferred_element_type=F32) for t in _split3(s[p])) for p in prs]
    for p in prs:
        s_ref[p // RW_PAIRS, p % RW_PAIRS] = s[p] * e_last[:, cols[p]] + sg[p] + hm[p]


def rw_chunk(r, lw, k, v, kk, bb):
    spec = pl.BlockSpec((RW_NB, RW_C, RW_W), lambda b, c: (b, c, 0))
    args = [x.reshape(BATCH, LP, RW_W) for x in (r, lw, k, v, kk, bb)]
    return pl.pallas_call(
        _rw_chunk_kernel,
        grid=(BATCH // RW_NB, LP // RW_C),
        in_specs=[spec] * 6,
        out_specs=spec,
        out_shape=jax.ShapeDtypeStruct((BATCH, LP, RW_W), F32),
        scratch_shapes=[pltpu.VMEM((RW_NB, RW_PAIRS, LANES, LANES), F32)],
        compiler_params=_params(("parallel", "arbitrary")),
        name="rw_chunk",
    )(*args).reshape(MT, RW_W)


def _rw_post_kernel(y_ref, r_ref, k_ref, v_ref, g_ref, lg_ref, lb_ref, rk_ref, o_ref):
    ones_blk = _head_ones()
    y = y_ref[...]
    mu = _head_sum(y, ones_blk) * (1.0 / RW_N)
    d = y - mu
    var = _head_sum(d * d, ones_blk) * (1.0 / RW_N)
    yn = d * lax.rsqrt(var + RW_GN_EPS) * lg_ref[...] + lb_ref[...]
    bonus = _head_sum(r_ref[...] * k_ref[...] * rk_ref[...], ones_blk) * v_ref[...]
    o_ref[...] = ((yn + bonus) * g_ref[...]).astype(BF16)


def rw_post(y, r, k, v, g, lg, lb, rk):
    blk = pl.BlockSpec((TS, RW_W), lambda i: (i, 0))
    par = pl.BlockSpec((1, RW_W), lambda i: (0, 0))
    return pl.pallas_call(
        _rw_post_kernel,
        grid=(MT // TS,),
        in_specs=[blk] * 5 + [par] * 3,
        out_specs=blk,
        out_shape=jax.ShapeDtypeStruct((MT, RW_W), BF16),
        compiler_params=_params(("parallel",)),
        name="rw_post",
    )(y, r, k, v, g, lg, lb, rk)


CD_WROWS = 256


def _cd_weight_kernel(wt_ref, o_ref):
    pad = jnp.zeros((GLR_PAD - GLA_RANK, CD_WROWS), F32)
    perm = jnp.concatenate([wt_ref[0:GLA_MAIN, :], wt_ref[GLA_COLS:GLA_COLS + RW_COLS, :],
                            wt_ref[GLA_MAIN:GLA_COLS, :], pad], axis=0)
    o_ref[...] = perm.T.astype(BF16)


def cd_weight(w, layer):
    wt = jnp.swapaxes(w, 1, 2)
    cols = wt.shape[1]
    return pl.pallas_call(
        _cd_weight_kernel,
        grid=(D_MODEL // CD_WROWS,),
        in_specs=[pl.BlockSpec((None, cols, CD_WROWS), lambda i: (layer, 0, i))],
        out_specs=pl.BlockSpec((CD_WROWS, CD_COLS), lambda i: (i, 0)),
        out_shape=jax.ShapeDtypeStruct((D_MODEL, CD_COLS), BF16),
        compiler_params=_params(("parallel",)),
        name="cd_weight",
    )(wt)


def kernel(x, meta, ab_w_in, ab_conf_dw, ab_conf_dw_b, ab_conf_ln_g, ab_conf_ln_b, ab_sc_dw, ab_w_out,
           cd_w_in, cd_gla_w2, cd_gla_b, cd_gla_norm_g, cd_rw_mu, cd_rw_w0, cd_rw_w2, cd_rw_a0, cd_rw_a2,
           cd_rw_g2, cd_rw_kk, cd_rw_ka, cd_rw_rk, cd_rw_ln_g, cd_rw_ln_b, cd_w_out, norm_mix, norm_ffn,
           ffn_w_up, ffn_dw, ffn_w_down, norm_final):
    bs = x.shape[0]
    h = jnp.concatenate([jnp.zeros((bs, PAD, D_MODEL), x.dtype),
                         jnp.broadcast_to(meta[None].astype(x.dtype), (bs, N_META, D_MODEL)), x], axis=1)
    h = h.reshape(MT, D_MODEL)
    row2 = lambda p: p.reshape(1, -1)
    g_final = row2(norm_final)

    z = norm_matmul(h, row2(norm_mix[0]), ab_w_in[0].astype(BF16), TN_AB)
    a_act, s_act = mix_ab(z, ab_conf_dw[0], row2(ab_conf_dw_b[0]), row2(ab_conf_ln_g[0]),
                          row2(ab_conf_ln_b[0]), ab_sc_dw[0])
    h = out_proj(h, a_act, s_act, ab_w_out[0].astype(BF16))
    w_up, w_down = ffn_w_up.astype(BF16), ffn_w_down.astype(BF16)
    h = conv_ffn(h, row2(norm_ffn[0]), w_up, ffn_dw, w_down, g_final, 0, False)

    z = norm_matmul(h, row2(norm_mix[1]), cd_weight(cd_w_in, 0), TN_CD)
    w2p = jnp.concatenate([cd_gla_w2[0], jnp.zeros((GLR_PAD - GLA_RANK, GLA_QK), F32)], axis=0)
    o_act = gla(z, w2p, row2(cd_gla_b[0]), row2(cd_gla_norm_g[0]))
    r, lw, k, v, kk, bb, g = rw_prep(z, row2(cd_rw_mu[0]), row2(cd_rw_w0[0]), cd_rw_w2[0], row2(cd_rw_a0[0]),
                                    cd_rw_a2[0], cd_rw_g2[0], row2(cd_rw_kk[0]), row2(cd_rw_ka[0]))
    y = rw_chunk(r, lw, k, v, kk, bb)
    y_act = rw_post(y, r, k, v, g, row2(cd_rw_ln_g[0]), row2(cd_rw_ln_b[0]), row2(cd_rw_rk[0]))
    h = out_proj(h, o_act, y_act, cd_w_out[0].astype(BF16))
    h = conv_ffn(h, row2(norm_ffn[1]), w_up, ffn_dw, w_down, g_final, 1, True)
    return h.reshape(bs, SEQ, D_MODEL)
```

```python
import functools

import jax
import jax.numpy as jnp
from jax import lax
from jax.experimental import pallas as pl
from jax.experimental.pallas import tpu as pltpu

D_MODEL = 2048
BATCH = 4
SEQ = 2048
N_META = 16
CONF_W = 1024
CONF_K = 31
SC_W = 1024
SC_K = 3
AB_IN = 5 * 1024
GLA_HEADS = 4
GLA_DK = 128
GLA_DV = 256
GLA_QK = GLA_HEADS * GLA_DK
GLA_V = GLA_HEADS * GLA_DV
GLA_RANK = 16
GLA_GATE_NORM = 16.0
GLA_CHUNK = 64
GLA_COLS = 2 * GLA_QK + 2 * GLA_V + GLA_RANK
RW_HEADS = 16
RW_N = 64
RW_W = RW_HEADS * RW_N
RW_LORA = 64 + 64 + 128
RW_COLS = 3 * RW_W + RW_LORA
RW_GN_EPS = 64e-5
D_FF = 5632
EPS = 1e-6
LN_EPS = 1e-5

PAD = (-N_META) % GLA_CHUNK
LP = PAD + N_META + SEQ
MT = BATCH * LP
FIRST = PAD
TM = LP // 3
TS = LP // 6
LANES = 128
SUBLANES = 8
GLR_PAD = LANES
GLA_MAIN = 2 * GLA_QK + 2 * GLA_V
CD_COLS = GLA_MAIN + RW_COLS + GLR_PAD
TN_AB = 1280
TN_CD = 2176
TN_FF = 512
TM_LAST = 512
FF_LAST_CHUNK = 128
HALO_FF = 16
HALO_CONF = 32
HALO_SC = 8
GLA_TB = TM
GLA_NB = 2
VMEM_LIMIT = 56 * 1024 * 1024

F32 = jnp.float32
BF16 = jnp.bfloat16


def _params(sem):
    return pltpu.CompilerParams(dimension_semantics=sem, vmem_limit_bytes=VMEM_LIMIT)


def _sigmoid(x):
    return 1.0 / (1.0 + jnp.exp(-x))


def _silu(x):
    return x * _sigmoid(x)


def _softplus(x):
    return jnp.maximum(x, 0.0) + jnp.log(1.0 + jnp.exp(-jnp.abs(x)))


def _split3(x):
    hi = x.astype(BF16)
    r = x - hi.astype(F32)
    mid = r.astype(BF16)
    lo = (r - mid.astype(F32)).astype(BF16)
    return hi, mid, lo


def _dot_x_m01(x, m01):
    return sum(jnp.dot(t, m01, preferred_element_type=F32) for t in _split3(x))


def _dot_m01_x(m01, x):
    return sum(jnp.dot(m01, t, preferred_element_type=F32) for t in _split3(x))


def _rms_rows(x, g):
    ms = jnp.mean(x * x, axis=-1, keepdims=True)
    return x * lax.rsqrt(ms + EPS) * g


NORM_CHUNK = 176


def _norm_rows_into(dst_ref, dst_row0, src_ref, g_ref, chunk=NORM_CHUNK):
    assert src_ref.shape[0] % chunk == 0 and chunk % 16 == 0

    def body(c, carry):
        r = pl.multiple_of(c * chunk, 16)
        x = src_ref[pl.ds(r, chunk), :]
        dst_ref[pl.ds(pl.multiple_of(dst_row0 + r, 16), chunk), :] = _rms_rows(x, g_ref[...]).astype(BF16)
        return carry

    lax.fori_loop(0, src_ref.shape[0] // chunk, body, 0)


def _norm_matmul_kernel(h_ref, g_ref, w_ref, o_ref, hn_ref):
    @pl.when(pl.program_id(1) == 0)
    def _():
        _norm_rows_into(hn_ref, 0, h_ref, g_ref)

    o_ref[...] = jnp.dot(hn_ref[...], w_ref[...], preferred_element_type=F32)


def norm_matmul(h, g, w, tn):
    n = w.shape[1]
    return pl.pallas_call(
        _norm_matmul_kernel,
        grid=(MT // TM, n // tn),
        in_specs=[
            pl.BlockSpec((TM, D_MODEL), lambda i, j: (i, 0)),
            pl.BlockSpec((1, D_MODEL), lambda i, j: (0, 0)),
            pl.BlockSpec((D_MODEL, tn), lambda i, j: (0, j)),
        ],
        out_specs=pl.BlockSpec((TM, tn), lambda i, j: (i, j)),
        out_shape=jax.ShapeDtypeStruct((MT, n), F32),
        scratch_shapes=[pltpu.VMEM((TM, D_MODEL), BF16)],
        compiler_params=_params(("parallel", "arbitrary")),
        name="norm_matmul",
    )(h, g, w)


def _zero_pad_rows(o_ref, tiles_per_batch):
    @pl.when(pl.program_id(0) % tiles_per_batch == 0)
    def _():
        o_ref[0:FIRST, :] = jnp.zeros((FIRST, o_ref.shape[1]), o_ref.dtype)


def _out_proj_kernel(h_ref, a1_ref, a2_ref, w1_ref, w2_ref, o_ref):
    acc = jnp.dot(a1_ref[...], w1_ref[...], preferred_element_type=F32)
    acc = acc + jnp.dot(a2_ref[...], w2_ref[...], preferred_element_type=F32)
    o_ref[...] = h_ref[...] + acc
    _zero_pad_rows(o_ref, LP // TS)


def out_proj(h, a1, a2, w):
    k1, k2 = a1.shape[1], a2.shape[1]
    assert k1 == k2 and w.shape[0] == k1 + k2
    w1 = w2 = w
    return pl.pallas_call(
        _out_proj_kernel,
        grid=(MT // TS,),
        in_specs=[
            pl.BlockSpec((TS, D_MODEL), lambda i: (i, 0)),
            pl.BlockSpec((TS, k1), lambda i: (i, 0)),
            pl.BlockSpec((TS, k2), lambda i: (i, 0)),
            pl.BlockSpec((k1, D_MODEL), lambda i: (0, 0)),
            pl.BlockSpec((k2, D_MODEL), lambda i: (1, 0)),
        ],
        out_specs=pl.BlockSpec((TS, D_MODEL), lambda i: (i, 0)),
        out_shape=jax.ShapeDtypeStruct((MT, D_MODEL), F32),
        compiler_params=_params(("parallel",)),
        name="out_proj",
    )(h, a1, a2, w1, w2)


def _ffn_kernel(h_ref, halo_ref, g_ref, wv_ref, wg_ref, dwv_ref, dwg_ref, wd_ref, gf_ref,
                o_ref, hn_ref, *, last_layer):
    j = pl.program_id(1)
    tm = h_ref.shape[0]
    chunk = FF_LAST_CHUNK if last_layer else NORM_CHUNK

    @pl.when(j == 0)
    def _():
        hn_ref[0:HALO_FF, :] = _rms_rows(halo_ref[...], g_ref[...]).astype(BF16)
        _norm_rows_into(hn_ref, HALO_FF, h_ref, g_ref, chunk)
        o_ref[...] = h_ref[...]

    hn = hn_ref[...]

    def conv(w_ref, dw_ref):
        u = jnp.dot(hn, w_ref[...], preferred_element_type=F32)
        u1 = pltpu.roll(u, 1, 0)
        u2 = pltpu.roll(u, 2, 0)
        c = u * dw_ref[2:3, :] + u1 * dw_ref[1:2, :] + u2 * dw_ref[0:1, :]
        return c[HALO_FF:, :]

    act = (_silu(conv(wg_ref, dwg_ref)) * conv(wv_ref, dwv_ref)).astype(BF16)
    for n in range(D_MODEL // TN_FF):
        cols = slice(n * TN_FF, (n + 1) * TN_FF)
        o_ref[:, cols] += jnp.dot(act, wd_ref[:, cols], preferred_element_type=F32)

    @pl.when(j == pl.num_programs(1) - 1)
    def _():
        if last_layer:
            def body(c, carry):
                rows = pl.ds(pl.multiple_of(c * chunk, 16), chunk)
                o_ref[rows, :] = _rms_rows(o_ref[rows, :], gf_ref[...])
                return carry

            lax.fori_loop(0, tm // chunk, body, 0)
        else:
            _zero_pad_rows(o_ref, LP // TM)


def conv_ffn(h, g, w_up, dw, w_down, g_final, layer, last_layer):
    nj = D_FF // TN_FF
    if last_layer:
        tm, rows_out = TM_LAST, BATCH * SEQ
        per_batch = SEQ // tm

        def row0(i):
            return pl.multiple_of((i // per_batch) * LP + (PAD + N_META) + (i % per_batch) * tm, HALO_FF)

        h_spec = pl.BlockSpec((pl.Element(tm), pl.Element(D_MODEL)), lambda i, j: (row0(i), 0))
        halo_spec = pl.BlockSpec((pl.Element(HALO_FF), pl.Element(D_MODEL)),
                                 lambda i, j: (pl.multiple_of(row0(i) - HALO_FF, HALO_FF), 0))
    else:
        tm, rows_out = TM, MT
        halo_blocks = tm // HALO_FF
        h_spec = pl.BlockSpec((tm, D_MODEL), lambda i, j: (i, 0))
        halo_spec = pl.BlockSpec((HALO_FF, D_MODEL), lambda i, j: (jnp.maximum(i * halo_blocks - 1, 0), 0))
    return pl.pallas_call(
        functools.partial(_ffn_kernel, last_layer=last_layer),
        grid=(rows_out // tm, nj),
        in_specs=[
            h_spec,
            halo_spec,
            pl.BlockSpec((1, D_MODEL), lambda i, j: (0, 0)),
            pl.BlockSpec((None, D_MODEL, TN_FF), lambda i, j: (layer, 0, j)),
            pl.BlockSpec((None, D_MODEL, TN_FF), lambda i, j: (layer, 0, nj + j)),
            pl.BlockSpec((None, 3, TN_FF), lambda i, j: (layer, 0, j)),
            pl.BlockSpec((None, 3, TN_FF), lambda i, j: (layer, 0, nj + j)),
            pl.BlockSpec((None, TN_FF, D_MODEL), lambda i, j: (layer, j, 0)),
            pl.BlockSpec((1, D_MODEL), lambda i, j: (0, 0)),
        ],
        out_specs=pl.BlockSpec((tm, D_MODEL), lambda i, j: (i, 0)),
        out_shape=jax.ShapeDtypeStruct((rows_out, D_MODEL), F32),
        scratch_shapes=[pltpu.VMEM((HALO_FF + tm, D_MODEL), BF16)],
        compiler_params=_params(("parallel", "arbitrary")),
        name="conv_ffn",
    )(h, h, g, w_up, w_up, dw, dw, w_down, g_final)


CONF_RB = 88


def _mix_ab_kernel(av_ref, ag_ref, sb_ref, sc_ref, sx_ref, hav_ref, hag_ref, hsc_ref, hsx_ref,
                   cw_ref, cb_ref, lg_ref, lb_ref, sw_ref, a_ref, s_ref, xs_ref, sh_ref, ac_ref, cx_ref):
    xs_ref[0:HALO_CONF, :] = hav_ref[...] * _sigmoid(hag_ref[...])
    xs_ref[HALO_CONF:HALO_CONF + TS, :] = av_ref[...] * _sigmoid(ag_ref[...])
    base = HALO_CONF - (CONF_K - 1)

    sub = SUBLANES
    sh_rows = HALO_CONF + TS
    xs_ref[sh_rows:, :] = jnp.zeros((sub, CONF_W), F32)

    def lane_chunk(c, carry):
        cols = pl.ds(pl.multiple_of(c * LANES, LANES), LANES)
        for s in range(sub):
            sh_ref[s] = xs_ref[pl.ds(s, sh_rows), cols]
        for rb in range(TS // CONF_RB):
            r0 = rb * CONF_RB
            acc = jnp.broadcast_to(cb_ref[:, cols], (CONF_RB, LANES))
            for t in range(CONF_K):
                off = base + t
                acc = acc + sh_ref[off % sub, pl.ds(r0 + off - off % sub, CONF_RB), :] * cw_ref[pl.ds(t, 1), cols]
            ac_ref[pl.ds(r0, CONF_RB), cols] = acc
        return carry

    lax.fori_loop(0, CONF_W // LANES, lane_chunk, 0)

    def norm_rows(c, carry):
        rows = pl.ds(pl.multiple_of(c * NORM_CHUNK, 16), NORM_CHUNK)
        a = ac_ref[rows, :]
        mu = jnp.mean(a, axis=-1, keepdims=True)
        d = a - mu
        var = jnp.mean(d * d, axis=-1, keepdims=True)
        y = d * lax.rsqrt(var + LN_EPS) * lg_ref[...] + lb_ref[...]
        a_ref[rows, :] = _silu(y).astype(BF16)
        return carry

    lax.fori_loop(0, TS // NORM_CHUNK, norm_rows, 0)

    cx_ref[0:HALO_SC, :] = hsc_ref[...] * hsx_ref[...]
    cx_ref[HALO_SC:, :] = sc_ref[...] * sx_ref[...]
    cv = cx_ref[pl.ds(HALO_SC - 2, TS), :] * sw_ref[0:1, :]
    cv = cv + cx_ref[pl.ds(HALO_SC - 1, TS), :] * sw_ref[1:2, :]
    cv = cv + cx_ref[pl.ds(HALO_SC, TS), :] * sw_ref[2:3, :]
    s_ref[...] = (sb_ref[...] * cv).astype(BF16)


def mix_ab(z, conf_dw, conf_b, ln_g, ln_b, sc_dw):
    w = CONF_W
    hc, hs = TS // HALO_CONF, TS // HALO_SC

    def col(k):
        return pl.BlockSpec((TS, w), lambda i, k=k: (i, k))

    def halo(rows, per_tile, k):
        return pl.BlockSpec((rows, w), lambda i, k=k: (jnp.maximum(i * per_tile - 1, 0), k))

    def full(r):
        return pl.BlockSpec((r, w), lambda i: (0, 0))

    return pl.pallas_call(
        _mix_ab_kernel,
        grid=(MT // TS,),
        in_specs=[col(0), col(1), col(2), col(3), col(4),
                  halo(HALO_CONF, hc, 0), halo(HALO_CONF, hc, 1),
                  halo(HALO_SC, hs, 3), halo(HALO_SC, hs, 4),
                  full(CONF_K), full(1), full(1), full(1), full(SC_K)],
        out_specs=[pl.BlockSpec((TS, w), lambda i: (i, 0)), pl.BlockSpec((TS, w), lambda i: (i, 0))],
        out_shape=[jax.ShapeDtypeStruct((MT, w), BF16), jax.ShapeDtypeStruct((MT, w), BF16)],
        scratch_shapes=[pltpu.VMEM((HALO_CONF + TS + SUBLANES, w), F32),
                        pltpu.VMEM((SUBLANES, HALO_CONF + TS, LANES), F32),
                        pltpu.VMEM((TS, w), F32), pltpu.VMEM((HALO_SC + TS, w), F32)],
        compiler_params=_params(("parallel",)),
        name="mix_ab",
    )(z, z, z, z, z, z, z, z, z, conf_dw, conf_b, ln_g, ln_b, sc_dw)


def _gla_kernel(q_ref, k_ref, v_ref, go_ref, glr_ref, w2_ref, b_ref, gn_ref, o_ref, st_ref):
    c_len = GLA_CHUNK
    tb = pl.program_id(1)

    @pl.when(tb == 0)
    def _():
        st_ref[...] = jnp.zeros(st_ref.shape, F32)

    ri = lax.broadcasted_iota(jnp.int32, (c_len, c_len), 0)
    ci = lax.broadcasted_iota(jnp.int32, (c_len, c_len), 1)
    tril = ri >= ci
    tril_b = jnp.where(tril, 1.0, 0.0).astype(BF16)
    row = lax.broadcasted_iota(jnp.int32, (c_len, 1), 0)
    nt_dims = (((1,), (1,)), ((), ()))

    def chunk(c, carry):
        rows = pl.ds(pl.multiple_of(c * c_len, c_len), c_len)
        real = tb * GLA_TB + c * c_len + row >= FIRST
        nb, hs = range(GLA_NB), range(GLA_HEADS)
        ch = [(n, h) for n in nb for h in hs]
        lin = [jnp.dot(glr_ref[n, rows, :], w2_ref[...], preferred_element_type=F32) + b_ref[...] for n in nb]
        log_a = [jnp.where(real, -_softplus(-lin[n]) / GLA_GATE_NORM, 0.0) for n in nb]
        cum_all = [_dot_m01_x(tril_b, log_a[n]) for n in nb]
        q_all = [q_ref[n, rows, :] * (GLA_DK ** -0.5) for n in nb]
        k_all = [k_ref[n, rows, :] for n in nb]
        v_all = [v_ref[n, rows, :] for n in nb]
        kc = [slice(h * GLA_DK, (h + 1) * GLA_DK) for h in hs]
        vc = [slice(h * GLA_DV, (h + 1) * GLA_DV) for h in hs]
        cum = [cum_all[n][:, kc[h]] for n, h in ch]
        last = [x[c_len - 1:c_len, :] for x in cum]
        v = [v_all[n][:, vc[h]] for n, h in ch]
        q_dec = [q_all[n][:, kc[h]] * jnp.exp(cum[i]) for i, (n, h) in enumerate(ch)]
        k_dec = [k_all[n][:, kc[h]] * jnp.exp(-cum[i]) for i, (n, h) in enumerate(ch)]
        k_st = [k_all[n][:, kc[h]] * jnp.exp(last[i] - cum[i]) for i, (n, h) in enumerate(ch)]
        scores = [jnp.where(tril, lax.dot_general(q_dec[i], k_dec[i], nt_dims, preferred_element_type=F32), 0.0)
                  for i in range(len(ch))]
        st = [st_ref[n, h] for n, h in ch]
        o = [jnp.dot(scores[i], v[i], preferred_element_type=F32) for i in range(len(ch))]
        o = [o[i] + lax.dot_general(q_dec[i], st[i], nt_dims, preferred_element_type=F32) for i in range(len(ch))]
        upd = [jnp.dot(v[i].T, k_st[i], preferred_element_type=F32) for i in range(len(ch))]
        for i, (n, h) in enumerate(ch):
            st_ref[n, h] = st[i] * jnp.exp(last[i]) + upd[i]
        o = [x * lax.rsqrt(jnp.mean(x * x, axis=-1, keepdims=True) + EPS) for x in o]
        for n in nb:
            go = go_ref[n, rows, :]
            outs = [o[n * GLA_HEADS + h] * gn_ref[:, vc[h]] * _silu(go[:, vc[h]]) for h in hs]
            o_ref[n, rows, :] = jnp.concatenate(outs, axis=-1).astype(BF16)
        return carry

    lax.fori_loop(0, GLA_TB // c_len, chunk, 0)


def gla(z, w2p, b, gn):
    z3 = z.reshape(BATCH, LP, CD_COLS)

    def zcol(width, k):
        return pl.BlockSpec((GLA_NB, GLA_TB, width), lambda b_, t, k=k: (b_, t, k))

    def full(shape):
        return pl.BlockSpec(shape, lambda b_, t: (0, 0))

    return pl.pallas_call(
        _gla_kernel,
        grid=(BATCH // GLA_NB, LP // GLA_TB),
        in_specs=[zcol(GLA_QK, 0), zcol(GLA_QK, 1), zcol(GLA_V, 1), zcol(GLA_V, 2),
                  zcol(GLR_PAD, (GLA_MAIN + RW_COLS) // GLR_PAD),
                  full((GLR_PAD, GLA_QK)), full((1, GLA_QK)), full((1, GLA_V))],
        out_specs=pl.BlockSpec((GLA_NB, GLA_TB, GLA_V), lambda b_, t: (b_, t, 0)),
        out_shape=jax.ShapeDtypeStruct((BATCH, LP, GLA_V), BF16),
        scratch_shapes=[pltpu.VMEM((GLA_NB, GLA_HEADS, GLA_DV, GLA_DK), F32)],
        compiler_params=_params(("parallel", "arbitrary")),
        name="gla",
    )(z3, z3, z3, z3, z3, w2p, b, gn).reshape(MT, GLA_V)


def _head_sum(x, ones_blk):
    parts = [_dot_x_m01(x[:, c * LANES:(c + 1) * LANES], ones_blk) for c in range(RW_W // LANES)]
    return jnp.concatenate(parts, axis=-1)


def _head_ones():
    r = lax.broadcasted_iota(jnp.int32, (LANES, LANES), 0) // RW_N
    c = lax.broadcasted_iota(jnp.int32, (LANES, LANES), 1) // RW_N
    return jnp.where(r == c, 1.0, 0.0).astype(BF16)


RW_PAIRS = RW_HEADS // 2
RW_C = 64
RW_NB = 2


def _rw_mixer_kernel(r_ref, k_ref, v_ref, lo_ref, mu_ref, mulo_ref, w0_ref, w2_ref, a0_ref, a2_ref, g2_ref,
                     kkp_ref, kap_ref, lg_ref, lb_ref, rk_ref, o_ref, s_ref, prev_ref):
    c = pl.program_id(1)

    @pl.when(c == 0)
    def _():
        s_ref[...] = jnp.zeros(s_ref.shape, F32)
        prev_ref[...] = jnp.zeros(prev_ref.shape, F32)

    row = lax.broadcasted_iota(jnp.int32, (RW_C, 1), 0)
    ones_blk = _head_ones()
    real = jnp.logical_or(c != 0, row >= FIRST)
    nbs = range(RW_NB)

    def shift_mix(x, n, col0, mu):
        cols = slice(col0, col0 + x.shape[1])
        prev = jnp.where(row == 0, prev_ref[n, 0:1, cols], pltpu.roll(x, 1, 0))
        prev_ref[n, 0:1, cols] = x[RW_C - 1:RW_C, :]
        return x + (prev - x) * mu

    r = [shift_mix(r_ref[n], n, 0, mu_ref[0:1, :]) for n in nbs]
    k = [shift_mix(k_ref[n], n, RW_W, mu_ref[1:2, :]) for n in nbs]
    v = [shift_mix(v_ref[n], n, 2 * RW_W, mu_ref[2:3, :]) for n in nbs]
    lo = [shift_mix(lo_ref[n], n, 3 * RW_W, mulo_ref[...]) for n in nbs]
    w_lin = [jnp.dot(jnp.tanh(lo[n][:, 0:64]), w2_ref[...], preferred_element_type=F32) for n in nbs]
    a_lin = [jnp.dot(lo[n][:, 64:128], a2_ref[...], preferred_element_type=F32) for n in nbs]
    g = [jnp.dot(_sigmoid(lo[n][:, 128:256]), g2_ref[...], preferred_element_type=F32) for n in nbs]
    lw = [-jnp.exp(-_softplus(-(w0_ref[...] + w_lin[n])) - 0.5) for n in nbs]
    a = [_sigmoid(a0_ref[...] + a_lin[n]) for n in nbs]
    k = [jnp.where(real, k[n], 0.0) for n in nbs]
    v = [jnp.where(real, v[n], 0.0) for n in nbs]
    kk = [k[n] * kkp_ref[...] for n in nbs]
    kk = [kk[n] / jnp.maximum(jnp.sqrt(_head_sum(kk[n] * kk[n], ones_blk)), 1e-12) for n in nbs]
    bb = [kk[n] * a[n] for n in nbs]
    k = [k[n] * (1.0 + (a[n] - 1.0) * kap_ref[...]) for n in nbs]

    def wide(xs):
        return jnp.concatenate(xs, axis=1)

    ri = lax.broadcasted_iota(jnp.int32, (RW_C, RW_C), 0)
    ci = lax.broadcasted_iota(jnp.int32, (RW_C, RW_C), 1)
    tril_b = jnp.where(ri >= ci, 1.0, 0.0).astype(BF16)
    lw_w = wide(lw)
    cum = _dot_m01_x(tril_b, lw_w)
    e_in, e_out, e_prev = jnp.exp(cum), jnp.exp(-cum), jnp.exp(cum - lw_w)
    e_last = jnp.exp(cum[RW_C - 1:RW_C, :])
    a_all = -wide(kk) * e_prev
    bt_all = wide(bb) * e_out
    kt_all = wide(k) * e_out
    rt_all = wide(r) * e_in
    bh_all = bt_all * e_last
    kh_all = kt_all * e_last
    v_all = wide(v)

    lane = lax.broadcasted_iota(jnp.int32, (RW_C, LANES), 1)
    first = lane < RW_N
    r2 = lax.broadcasted_iota(jnp.int32, (LANES, LANES), 0)
    c2 = lax.broadcasted_iota(jnp.int32, (LANES, LANES), 1)
    strict, incl = r2 % RW_C > c2 % RW_C, r2 % RW_C >= c2 % RW_C
    eye = jnp.where(r2 == c2, 1.0, 0.0)
    zeros = jnp.zeros((LANES, LANES), F32)
    nt = (((1,), (1,)), ((), ()))

    def bd(x):
        return jnp.concatenate([jnp.where(first, x, 0.0), jnp.where(first, 0.0, x)], axis=0)

    def mm(x, y):
        return jnp.dot(x.astype(BF16), y.astype(BF16), preferred_element_type=F32)

    def cat(xs, axis):
        return jnp.concatenate(xs, axis=axis)

    prs = range(RW_NB * RW_PAIRS)
    cols = [slice(p * LANES, (p + 1) * LANES) for p in prs]
    a, bt, kt, rt, bh, kh, vv = ([bd(x[:, cs]) for cs in cols]
                                 for x in (a_all, bt_all, kt_all, rt_all, bh_all, kh_all, v_all))
    pm = [lax.dot_general(cat([a[p], rt[p]], 0).astype(BF16), cat([bt[p], kt[p]], 0).astype(BF16), nt,
                          preferred_element_type=F32) for p in prs]
    l_ak = [jnp.where(strict, pm[p][0:LANES, LANES:], 0.0) for p in prs]
    m_rb = [jnp.where(incl, pm[p][LANES:, 0:LANES], 0.0) for p in prs]
    m_rk = [jnp.where(incl, pm[p][LANES:, LANES:], 0.0) for p in prs]
    x = [jnp.where(strict, pm[p][0:LANES, 0:LANES], 0.0) for p in prs]
    t_inv = [eye + x[p] for p in prs]
    for _ in range(RW_C.bit_length() - 2):
        x = [mm(x[p], x[p]) for p in prs]
        t_inv = [t_inv[p] + mm(t_inv[p], x[p]) for p in prs]
    lv = [mm(l_ak[p], vv[p]) for p in prs]
    w12 = [mm(t_inv[p], cat([a[p], lv[p]], 1)) for p in prs]
    qy = [mm(cat([m_rb[p], m_rk[p]], 1), cat([w12[p], cat([zeros, vv[p]], 1)], 0)) for p in prs]
    g_m = [mm(w12[p][:, 0:LANES].T, bh[p]) for p in prs]
    hm = [mm(cat([w12[p][:, LANES:], vv[p]], 0).T, cat([bh[p], kh[p]], 0)) for p in prs]
    s = [s_ref[p // RW_PAIRS, p % RW_PAIRS] for p in prs]
    y = [lax.dot_general((rt[p] + qy[p][:, 0:LANES]).astype(BF16), s[p].astype(BF16), nt,
                         preferred_element_type=F32) + qy[p][:, LANES:] for p in prs]
    ys = [y[p][0:RW_C, :] + y[p][RW_C:, :] for p in prs]
    sg = [sum(jnp.dot(t, g_m[p].astype(BF16), preferred_element_type=F32) for t in _split3(s[p])) for p in prs]
    for p in prs:
        s_ref[p // RW_PAIRS, p % RW_PAIRS] = s[p] * e_last[:, cols[p]] + sg[p] + hm[p]

    for n in nbs:
        yn = jnp.concatenate(ys[n * RW_PAIRS:(n + 1) * RW_PAIRS], axis=1)
        mean = _head_sum(yn, ones_blk) * (1.0 / RW_N)
        d = yn - mean
        var = _head_sum(d * d, ones_blk) * (1.0 / RW_N)
        yn = d * lax.rsqrt(var + RW_GN_EPS) * lg_ref[...] + lb_ref[...]
        bonus = _head_sum(r[n] * k[n] * rk_ref[...], ones_blk) * v[n]
        o_ref[n] = ((yn + bonus) * g[n]).astype(BF16)


def rw_mixer(z, mu, w0, w2, a0, a2, g2, kkp, kap, lg, lb, rk):
    w = RW_W
    c0 = GLA_MAIN // w
    lo0 = (GLA_MAIN + 3 * w) // RW_LORA
    z3 = z.reshape(BATCH, LP, CD_COLS)

    def zcol(width, k):
        return pl.BlockSpec((RW_NB, RW_C, width), lambda b, c, k=k: (b, c, k))

    def full(shape):
        return pl.BlockSpec(shape, lambda b, c: (0, 0))

    return pl.pallas_call(
        _rw_mixer_kernel,
        grid=(BATCH // RW_NB, LP // RW_C),
        in_specs=[zcol(w, c0), zcol(w, c0 + 1), zcol(w, c0 + 2), zcol(RW_LORA, lo0),
                  full((3, w)), full((1, RW_LORA)),
                  full((1, w)), full((64, w)), full((1, w)), full((64, w)), full((128, w)),
                  full((1, w)), full((1, w)), full((1, w)), full((1, w)), full((1, w))],
        out_specs=pl.BlockSpec((RW_NB, RW_C, w), lambda b, c: (b, c, 0)),
        out_shape=jax.ShapeDtypeStruct((BATCH, LP, w), BF16),
        scratch_shapes=[pltpu.VMEM((RW_NB, RW_PAIRS, LANES, LANES), F32),
                        pltpu.VMEM((RW_NB, SUBLANES, RW_COLS), F32)],
        compiler_params=_params(("parallel", "arbitrary")),
        name="rw_mixer",
    )(z3, z3, z3, z3, mu[:, :3 * w].reshape(3, w), mu[:, 3 * w:], w0, w2, a0, a2, g2, kkp, kap, lg, lb, rk
      ).reshape(MT, w)


CD_WROWS = 256


def _cd_weight_kernel(wt_ref, o_ref):
    pad = jnp.zeros((GLR_PAD - GLA_RANK, CD_WROWS), F32)
    perm = jnp.concatenate([wt_ref[0:GLA_MAIN, :], wt_ref[GLA_COLS:GLA_COLS + RW_COLS, :],
                            wt_ref[GLA_MAIN:GLA_COLS, :], pad], axis=0)
    o_ref[...] = perm.T.astype(BF16)


def cd_weight(w, layer):
    wt = jnp.swapaxes(w, 1, 2)
    cols = wt.shape[1]
    return pl.pallas_call(
        _cd_weight_kernel,
        grid=(D_MODEL // CD_WROWS,),
        in_specs=[pl.BlockSpec((None, cols, CD_WROWS), lambda i: (layer, 0, i))],
        out_specs=pl.BlockSpec((CD_WROWS, CD_COLS), lambda i: (i, 0)),
        out_shape=jax.ShapeDtypeStruct((D_MODEL, CD_COLS), BF16),
        compiler_params=_params(("parallel",)),
        name="cd_weight",
    )(wt)


def kernel(x, meta, ab_w_in, ab_conf_dw, ab_conf_dw_b, ab_conf_ln_g, ab_conf_ln_b, ab_sc_dw, ab_w_out,
           cd_w_in, cd_gla_w2, cd_gla_b, cd_gla_norm_g, cd_rw_mu, cd_rw_w0, cd_rw_w2, cd_rw_a0, cd_rw_a2,
           cd_rw_g2, cd_rw_kk, cd_rw_ka, cd_rw_rk, cd_rw_ln_g, cd_rw_ln_b, cd_w_out, norm_mix, norm_ffn,
           ffn_w_up, ffn_dw, ffn_w_down, norm_final):
    bs = x.shape[0]
    h = jnp.concatenate([jnp.zeros((bs, PAD, D_MODEL), x.dtype),
                         jnp.broadcast_to(meta[None].astype(x.dtype), (bs, N_META, D_MODEL)), x], axis=1)
    h = h.reshape(MT, D_MODEL)
    row2 = lambda p: p.reshape(1, -1)
    g_final = row2(norm_final)

    z = norm_matmul(h, row2(norm_mix[0]), ab_w_in[0].astype(BF16), TN_AB)
    a_act, s_act = mix_ab(z, ab_conf_dw[0], row2(ab_conf_dw_b[0]), row2(ab_conf_ln_g[0]),
                          row2(ab_conf_ln_b[0]), ab_sc_dw[0])
    h = out_proj(h, a_act, s_act, ab_w_out[0].astype(BF16))
    w_up, w_down = ffn_w_up.astype(BF16), ffn_w_down.astype(BF16)
    h = conv_ffn(h, row2(norm_ffn[0]), w_up, ffn_dw, w_down, g_final, 0, False)

    z = norm_matmul(h, row2(norm_mix[1]), cd_weight(cd_w_in, 0), TN_CD)
    w2p = jnp.concatenate([cd_gla_w2[0], jnp.zeros((GLR_PAD - GLA_RANK, GLA_QK), F32)], axis=0)
    o_act = gla(z, w2p, row2(cd_gla_b[0]), row2(cd_gla_norm_g[0]))
    y_act = rw_mixer(z, row2(cd_rw_mu[0]), row2(cd_rw_w0[0]), cd_rw_w2[0], row2(cd_rw_a0[0]), cd_rw_a2[0],
                     cd_rw_g2[0], row2(cd_rw_kk[0]), row2(cd_rw_ka[0]), row2(cd_rw_ln_g[0]), row2(cd_rw_ln_b[0]),
                     row2(cd_rw_rk[0]))
    h = out_proj(h, o_act, y_act, cd_w_out[0].astype(BF16))
    h = conv_ffn(h, row2(norm_ffn[1]), w_up, ffn_dw, w_down, g_final, 1, True)
    return h.reshape(bs, SEQ, D_MODEL)
```

```python
import functools

import jax
import jax.numpy as jnp
from jax import lax
from jax.experimental import pallas as pl
from jax.experimental.pallas import tpu as pltpu

D_MODEL = 2048
BATCH = 4
SEQ = 2048
N_META = 16
CONF_W = 1024
CONF_K = 31
SC_W = 1024
SC_K = 3
AB_IN = 5 * 1024
GLA_HEADS = 4
GLA_DK = 128
GLA_DV = 256
GLA_QK = GLA_HEADS * GLA_DK
GLA_V = GLA_HEADS * GLA_DV
GLA_RANK = 16
GLA_GATE_NORM = 16.0
GLA_CHUNK = 64
GLA_COLS = 2 * GLA_QK + 2 * GLA_V + GLA_RANK
RW_HEADS = 16
RW_N = 64
RW_W = RW_HEADS * RW_N
RW_LORA = 64 + 64 + 128
RW_COLS = 3 * RW_W + RW_LORA
RW_GN_EPS = 64e-5
D_FF = 5632
EPS = 1e-6
LN_EPS = 1e-5

PAD = (-N_META) % GLA_CHUNK
LP = PAD + N_META + SEQ
MT = BATCH * LP
FIRST = PAD
TM = LP // 3
TS = LP // 6
LANES = 128
SUBLANES = 8
GLR_PAD = LANES
GLA_MAIN = 2 * GLA_QK + 2 * GLA_V
CD_COLS = GLA_MAIN + RW_COLS + GLR_PAD
TN_AB = 1280
TN_CD = 2176
TN_FF = 512
TM_LAST = 512
FF_LAST_CHUNK = 128
HALO_FF = 16
HALO_CONF = 32
HALO_SC = 8
GLA_TB = TM
GLA_NB = 2
VMEM_LIMIT = 56 * 1024 * 1024

F32 = jnp.float32
BF16 = jnp.bfloat16


def _params(sem):
    return pltpu.CompilerParams(dimension_semantics=sem, vmem_limit_bytes=VMEM_LIMIT)


def _sigmoid(x):
    return 1.0 / (1.0 + jnp.exp(-x))


def _silu(x):
    return x * _sigmoid(x)


def _softplus(x):
    return jnp.maximum(x, 0.0) + jnp.log(1.0 + jnp.exp(-jnp.abs(x)))


def _split3(x):
    hi = x.astype(BF16)
    r = x - hi.astype(F32)
    mid = r.astype(BF16)
    lo = (r - mid.astype(F32)).astype(BF16)
    return hi, mid, lo


def _dot_x_m01(x, m01):
    return jnp.dot(jnp.concatenate(_split3(x), axis=1), jnp.concatenate([m01] * 3, axis=0),
                   preferred_element_type=F32)


def _dot_m01_x(m01, x):
    return jnp.dot(jnp.concatenate([m01] * 3, axis=1), jnp.concatenate(_split3(x), axis=0),
                   preferred_element_type=F32)


def _rms_rows(x, g):
    ms = jnp.mean(x * x, axis=-1, keepdims=True)
    return x * lax.rsqrt(ms + EPS) * g


NORM_CHUNK = 176


def _norm_rows_into(dst_ref, dst_row0, src_ref, g_ref, chunk=NORM_CHUNK):
    assert src_ref.shape[0] % chunk == 0 and chunk % 16 == 0

    def body(c, carry):
        r = pl.multiple_of(c * chunk, 16)
        x = src_ref[pl.ds(r, chunk), :]
        dst_ref[pl.ds(pl.multiple_of(dst_row0 + r, 16), chunk), :] = _rms_rows(x, g_ref[...]).astype(BF16)
        return carry

    lax.fori_loop(0, src_ref.shape[0] // chunk, body, 0)


def _norm_matmul_kernel(h_ref, g_ref, w_ref, o_ref, hn_ref):
    @pl.when(pl.program_id(1) == 0)
    def _():
        _norm_rows_into(hn_ref, 0, h_ref, g_ref)

    o_ref[...] = jnp.dot(hn_ref[...], w_ref[...], preferred_element_type=F32)


def norm_matmul(h, g, w, tn):
    n = w.shape[1]
    return pl.pallas_call(
        _norm_matmul_kernel,
        grid=(MT // TM, n // tn),
        in_specs=[
            pl.BlockSpec((TM, D_MODEL), lambda i, j: (i, 0)),
            pl.BlockSpec((1, D_MODEL), lambda i, j: (0, 0)),
            pl.BlockSpec((D_MODEL, tn), lambda i, j: (0, j)),
        ],
        out_specs=pl.BlockSpec((TM, tn), lambda i, j: (i, j)),
        out_shape=jax.ShapeDtypeStruct((MT, n), F32),
        scratch_shapes=[pltpu.VMEM((TM, D_MODEL), BF16)],
        compiler_params=_params(("parallel", "arbitrary")),
        name="norm_matmul",
    )(h, g, w)


def _zero_pad_rows(o_ref, tiles_per_batch):
    @pl.when(pl.program_id(0) % tiles_per_batch == 0)
    def _():
        o_ref[0:FIRST, :] = jnp.zeros((FIRST, o_ref.shape[1]), o_ref.dtype)


def _out_proj_kernel(h_ref, a1_ref, a2_ref, w1_ref, w2_ref, o_ref):
    acc = jnp.dot(a1_ref[...], w1_ref[...], preferred_element_type=F32)
    acc = acc + jnp.dot(a2_ref[...], w2_ref[...], preferred_element_type=F32)
    o_ref[...] = h_ref[...] + acc
    _zero_pad_rows(o_ref, LP // TS)


def out_proj(h, a1, a2, w):
    k1, k2 = a1.shape[1], a2.shape[1]
    assert k1 == k2 and w.shape[0] == k1 + k2
    w1 = w2 = w
    return pl.pallas_call(
        _out_proj_kernel,
        grid=(MT // TS,),
        in_specs=[
            pl.BlockSpec((TS, D_MODEL), lambda i: (i, 0)),
            pl.BlockSpec((TS, k1), lambda i: (i, 0)),
            pl.BlockSpec((TS, k2), lambda i: (i, 0)),
            pl.BlockSpec((k1, D_MODEL), lambda i: (0, 0)),
            pl.BlockSpec((k2, D_MODEL), lambda i: (1, 0)),
        ],
        out_specs=pl.BlockSpec((TS, D_MODEL), lambda i: (i, 0)),
        out_shape=jax.ShapeDtypeStruct((MT, D_MODEL), F32),
        compiler_params=_params(("parallel",)),
        name="out_proj",
    )(h, a1, a2, w1, w2)


def _ffn_kernel(h_ref, halo_ref, g_ref, wv_ref, wg_ref, dwv_ref, dwg_ref, wd_ref, gf_ref,
                o_ref, hn_ref, *, last_layer):
    j = pl.program_id(1)
    tm = h_ref.shape[0]
    chunk = FF_LAST_CHUNK if last_layer else NORM_CHUNK

    @pl.when(j == 0)
    def _():
        hn_ref[0:HALO_FF, :] = _rms_rows(halo_ref[...], g_ref[...]).astype(BF16)
        _norm_rows_into(hn_ref, HALO_FF, h_ref, g_ref, chunk)
        o_ref[...] = h_ref[...]

    hn = hn_ref[...]

    def conv(w_ref, dw_ref):
        u = jnp.dot(hn, w_ref[...], preferred_element_type=F32)
        u1 = pltpu.roll(u, 1, 0)
        u2 = pltpu.roll(u, 2, 0)
        c = u * dw_ref[2:3, :] + u1 * dw_ref[1:2, :] + u2 * dw_ref[0:1, :]
        return c[HALO_FF:, :]

    act = (_silu(conv(wg_ref, dwg_ref)) * conv(wv_ref, dwv_ref)).astype(BF16)
    for n in range(D_MODEL // TN_FF):
        cols = slice(n * TN_FF, (n + 1) * TN_FF)
        o_ref[:, cols] += jnp.dot(act, wd_ref[:, cols], preferred_element_type=F32)

    @pl.when(j == pl.num_programs(1) - 1)
    def _():
        if last_layer:
            def body(c, carry):
                rows = pl.ds(pl.multiple_of(c * chunk, 16), chunk)
                o_ref[rows, :] = _rms_rows(o_ref[rows, :], gf_ref[...])
                return carry

            lax.fori_loop(0, tm // chunk, body, 0)
        else:
            _zero_pad_rows(o_ref, LP // TM)


def conv_ffn(h, g, w_up, dw, w_down, g_final, layer, last_layer):
    nj = D_FF // TN_FF
    if last_layer:
        tm, rows_out = TM_LAST, BATCH * SEQ
        per_batch = SEQ // tm

        def row0(i):
            return pl.multiple_of((i // per_batch) * LP + (PAD + N_META) + (i % per_batch) * tm, HALO_FF)

        h_spec = pl.BlockSpec((pl.Element(tm), pl.Element(D_MODEL)), lambda i, j: (row0(i), 0))
        halo_spec = pl.BlockSpec((pl.Element(HALO_FF), pl.Element(D_MODEL)),
                                 lambda i, j: (pl.multiple_of(row0(i) - HALO_FF, HALO_FF), 0))
    else:
        tm, rows_out = TM, MT
        halo_blocks = tm // HALO_FF
        h_spec = pl.BlockSpec((tm, D_MODEL), lambda i, j: (i, 0))
        halo_spec = pl.BlockSpec((HALO_FF, D_MODEL), lambda i, j: (jnp.maximum(i * halo_blocks - 1, 0), 0))
    return pl.pallas_call(
        functools.partial(_ffn_kernel, last_layer=last_layer),
        grid=(rows_out // tm, nj),
        in_specs=[
            h_spec,
            halo_spec,
            pl.BlockSpec((1, D_MODEL), lambda i, j: (0, 0)),
            pl.BlockSpec((None, D_MODEL, TN_FF), lambda i, j: (layer, 0, j)),
            pl.BlockSpec((None, D_MODEL, TN_FF), lambda i, j: (layer, 0, nj + j)),
            pl.BlockSpec((None, 3, TN_FF), lambda i, j: (layer, 0, j)),
            pl.BlockSpec((None, 3, TN_FF), lambda i, j: (layer, 0, nj + j)),
            pl.BlockSpec((None, TN_FF, D_MODEL), lambda i, j: (layer, j, 0)),
            pl.BlockSpec((1, D_MODEL), lambda i, j: (0, 0)),
        ],
        out_specs=pl.BlockSpec((tm, D_MODEL), lambda i, j: (i, 0)),
        out_shape=jax.ShapeDtypeStruct((rows_out, D_MODEL), F32),
        scratch_shapes=[pltpu.VMEM((HALO_FF + tm, D_MODEL), BF16)],
        compiler_params=_params(("parallel", "arbitrary")),
        name="conv_ffn",
    )(h, h, g, w_up, w_up, dw, dw, w_down, g_final)


CONF_RB = 88


def _mix_ab_kernel(av_ref, ag_ref, sb_ref, sc_ref, sx_ref, hav_ref, hag_ref, hsc_ref, hsx_ref,
                   cw_ref, cb_ref, lg_ref, lb_ref, sw_ref, a_ref, s_ref, xs_ref, sh_ref, ac_ref, cx_ref):
    xs_ref[0:HALO_CONF, :] = hav_ref[...] * _sigmoid(hag_ref[...])
    xs_ref[HALO_CONF:HALO_CONF + TS, :] = av_ref[...] * _sigmoid(ag_ref[...])
    base = HALO_CONF - (CONF_K - 1)

    sub = SUBLANES
    sh_rows = HALO_CONF + TS
    xs_ref[sh_rows:, :] = jnp.zeros((sub, CONF_W), F32)

    def lane_chunk(c, carry):
        cols = pl.ds(pl.multiple_of(c * LANES, LANES), LANES)
        for s in range(sub):
            sh_ref[s] = xs_ref[pl.ds(s, sh_rows), cols]
        for rb in range(TS // CONF_RB):
            r0 = rb * CONF_RB
            acc = jnp.broadcast_to(cb_ref[:, cols], (CONF_RB, LANES))
            for t in range(CONF_K):
                off = base + t
                acc = acc + sh_ref[off % sub, pl.ds(r0 + off - off % sub, CONF_RB), :] * cw_ref[pl.ds(t, 1), cols]
            ac_ref[pl.ds(r0, CONF_RB), cols] = acc
        return carry

    lax.fori_loop(0, CONF_W // LANES, lane_chunk, 0)

    def norm_rows(c, carry):
        rows = pl.ds(pl.multiple_of(c * NORM_CHUNK, 16), NORM_CHUNK)
        a = ac_ref[rows, :]
        mu = jnp.mean(a, axis=-1, keepdims=True)
        d = a - mu
        var = jnp.mean(d * d, axis=-1, keepdims=True)
        y = d * lax.rsqrt(var + LN_EPS) * lg_ref[...] + lb_ref[...]
        a_ref[rows, :] = _silu(y).astype(BF16)
        return carry

    lax.fori_loop(0, TS // NORM_CHUNK, norm_rows, 0)

    cx_ref[0:HALO_SC, :] = hsc_ref[...] * hsx_ref[...]
    cx_ref[HALO_SC:, :] = sc_ref[...] * sx_ref[...]
    cv = cx_ref[pl.ds(HALO_SC - 2, TS), :] * sw_ref[0:1, :]
    cv = cv + cx_ref[pl.ds(HALO_SC - 1, TS), :] * sw_ref[1:2, :]
    cv = cv + cx_ref[pl.ds(HALO_SC, TS), :] * sw_ref[2:3, :]
    s_ref[...] = (sb_ref[...] * cv).astype(BF16)


def mix_ab(z, conf_dw, conf_b, ln_g, ln_b, sc_dw):
    w = CONF_W
    hc, hs = TS // HALO_CONF, TS // HALO_SC

    def col(k):
        return pl.BlockSpec((TS, w), lambda i, k=k: (i, k))

    def halo(rows, per_tile, k):
        return pl.BlockSpec((rows, w), lambda i, k=k: (jnp.maximum(i * per_tile - 1, 0), k))

    def full(r):
        return pl.BlockSpec((r, w), lambda i: (0, 0))

    return pl.pallas_call(
        _mix_ab_kernel,
        grid=(MT // TS,),
        in_specs=[col(0), col(1), col(2), col(3), col(4),
                  halo(HALO_CONF, hc, 0), halo(HALO_CONF, hc, 1),
                  halo(HALO_SC, hs, 3), halo(HALO_SC, hs, 4),
                  full(CONF_K), full(1), full(1), full(1), full(SC_K)],
        out_specs=[pl.BlockSpec((TS, w), lambda i: (i, 0)), pl.BlockSpec((TS, w), lambda i: (i, 0))],
        out_shape=[jax.ShapeDtypeStruct((MT, w), BF16), jax.ShapeDtypeStruct((MT, w), BF16)],
        scratch_shapes=[pltpu.VMEM((HALO_CONF + TS + SUBLANES, w), F32),
                        pltpu.VMEM((SUBLANES, HALO_CONF + TS, LANES), F32),
                        pltpu.VMEM((TS, w), F32), pltpu.VMEM((HALO_SC + TS, w), F32)],
        compiler_params=_params(("parallel",)),
        name="mix_ab",
    )(z, z, z, z, z, z, z, z, z, conf_dw, conf_b, ln_g, ln_b, sc_dw)


def _gla_kernel(q_ref, k_ref, v_ref, go_ref, glr_ref, w2_ref, b_ref, gn_ref, o_ref, st_ref):
    c_len = GLA_CHUNK
    tb = pl.program_id(1)

    @pl.when(tb == 0)
    def _():
        st_ref[...] = jnp.zeros(st_ref.shape, F32)

    ri = lax.broadcasted_iota(jnp.int32, (c_len, c_len), 0)
    ci = lax.broadcasted_iota(jnp.int32, (c_len, c_len), 1)
    tril = ri >= ci
    tril_b = jnp.where(tril, 1.0, 0.0).astype(BF16)
    row = lax.broadcasted_iota(jnp.int32, (c_len, 1), 0)
    nt_dims = (((1,), (1,)), ((), ()))

    def chunk(c, carry):
        rows = pl.ds(pl.multiple_of(c * c_len, c_len), c_len)
        real = tb * GLA_TB + c * c_len + row >= FIRST
        nb, hs = range(GLA_NB), range(GLA_HEADS)
        ch = [(n, h) for n in nb for h in hs]
        lin = [jnp.dot(glr_ref[n, rows, :], w2_ref[...], preferred_element_type=F32) + b_ref[...] for n in nb]
        log_a = [jnp.where(real, -_softplus(-lin[n]) / GLA_GATE_NORM, 0.0) for n in nb]
        cum_all = [_dot_m01_x(tril_b, log_a[n]) for n in nb]
        q_all = [q_ref[n, rows, :] * (GLA_DK ** -0.5) for n in nb]
        k_all = [k_ref[n, rows, :] for n in nb]
        v_all = [v_ref[n, rows, :] for n in nb]
        kc = [slice(h * GLA_DK, (h + 1) * GLA_DK) for h in hs]
        vc = [slice(h * GLA_DV, (h + 1) * GLA_DV) for h in hs]
        cum = [cum_all[n][:, kc[h]] for n, h in ch]
        last = [x[c_len - 1:c_len, :] for x in cum]
        v = [v_all[n][:, vc[h]] for n, h in ch]
        q_dec = [q_all[n][:, kc[h]] * jnp.exp(cum[i]) for i, (n, h) in enumerate(ch)]
        k_dec = [k_all[n][:, kc[h]] * jnp.exp(-cum[i]) for i, (n, h) in enumerate(ch)]
        k_st = [k_all[n][:, kc[h]] * jnp.exp(last[i] - cum[i]) for i, (n, h) in enumerate(ch)]
        scores = [jnp.where(tril, lax.dot_general(q_dec[i], k_dec[i], nt_dims, preferred_element_type=F32), 0.0)
                  for i in range(len(ch))]
        st = [st_ref[n, h] for n, h in ch]
        o = [jnp.dot(scores[i], v[i], preferred_element_type=F32) for i in range(len(ch))]
        o = [o[i] + lax.dot_general(q_dec[i], st[i], nt_dims, preferred_element_type=F32) for i in range(len(ch))]
        upd = [jnp.dot(v[i].T, k_st[i], preferred_element_type=F32) for i in range(len(ch))]
        for i, (n, h) in enumerate(ch):
            st_ref[n, h] = st[i] * jnp.exp(last[i]) + upd[i]
        o = [x * lax.rsqrt(jnp.mean(x * x, axis=-1, keepdims=True) + EPS) for x in o]
        for n in nb:
            go = go_ref[n, rows, :]
            outs = [o[n * GLA_HEADS + h] * gn_ref[:, vc[h]] * _silu(go[:, vc[h]]) for h in hs]
            o_ref[n, rows, :] = jnp.concatenate(outs, axis=-1).astype(BF16)
        return carry

    lax.fori_loop(0, GLA_TB // c_len, chunk, 0)


def gla(z, w2p, b, gn):
    z3 = z.reshape(BATCH, LP, CD_COLS)

    def zcol(width, k):
        return pl.BlockSpec((GLA_NB, GLA_TB, width), lambda b_, t, k=k: (b_, t, k))

    def full(shape):
        return pl.BlockSpec(shape, lambda b_, t: (0, 0))

    return pl.pallas_call(
        _gla_kernel,
        grid=(BATCH // GLA_NB, LP // GLA_TB),
        in_specs=[zcol(GLA_QK, 0), zcol(GLA_QK, 1), zcol(GLA_V, 1), zcol(GLA_V, 2),
                  zcol(GLR_PAD, (GLA_MAIN + RW_COLS) // GLR_PAD),
                  full((GLR_PAD, GLA_QK)), full((1, GLA_QK)), full((1, GLA_V))],
        out_specs=pl.BlockSpec((GLA_NB, GLA_TB, GLA_V), lambda b_, t: (b_, t, 0)),
        out_shape=jax.ShapeDtypeStruct((BATCH, LP, GLA_V), BF16),
        scratch_shapes=[pltpu.VMEM((GLA_NB, GLA_HEADS, GLA_DV, GLA_DK), F32)],
        compiler_params=_params(("parallel", "arbitrary")),
        name="gla",
    )(z3, z3, z3, z3, z3, w2p, b, gn).reshape(MT, GLA_V)


def _head_sum(x, ones_blk):
    n = x.shape[0]
    stacked = jnp.concatenate([x[:, c * LANES:(c + 1) * LANES] for c in range(RW_W // LANES)], axis=0)
    out = _dot_x_m01(stacked, ones_blk)
    return jnp.concatenate([out[c * n:(c + 1) * n, :] for c in range(RW_W // LANES)], axis=-1)


def _head_ones():
    r = lax.broadcasted_iota(jnp.int32, (LANES, LANES), 0) // RW_N
    c = lax.broadcasted_iota(jnp.int32, (LANES, LANES), 1) // RW_N
    return jnp.where(r == c, 1.0, 0.0).astype(BF16)


RW_PAIRS = RW_HEADS // 2
RW_C = 64
RW_NB = 2


def _rw_mixer_kernel(r_ref, k_ref, v_ref, lo_ref, mu_ref, mulo_ref, w0_ref, w2_ref, a0_ref, a2_ref, g2_ref,
                     kkp_ref, kap_ref, lg_ref, lb_ref, rk_ref, o_ref, s_ref, prev_ref):
    c = pl.program_id(1)

    @pl.when(c == 0)
    def _():
        s_ref[...] = jnp.zeros(s_ref.shape, F32)
        prev_ref[...] = jnp.zeros(prev_ref.shape, F32)

    row = lax.broadcasted_iota(jnp.int32, (RW_C, 1), 0)
    ones_blk = _head_ones()
    real = jnp.logical_or(c != 0, row >= FIRST)
    nbs = range(RW_NB)

    def shift_mix(x, n, col0, mu):
        cols = slice(col0, col0 + x.shape[1])
        prev = jnp.where(row == 0, prev_ref[n, 0:1, cols], pltpu.roll(x, 1, 0))
        prev_ref[n, 0:1, cols] = x[RW_C - 1:RW_C, :]
        return x + (prev - x) * mu

    r = [shift_mix(r_ref[n], n, 0, mu_ref[0:1, :]) for n in nbs]
    k = [shift_mix(k_ref[n], n, RW_W, mu_ref[1:2, :]) for n in nbs]
    v = [shift_mix(v_ref[n], n, 2 * RW_W, mu_ref[2:3, :]) for n in nbs]
    lo = [shift_mix(lo_ref[n], n, 3 * RW_W, mulo_ref[...]) for n in nbs]
    w_lin = [jnp.dot(jnp.tanh(lo[n][:, 0:64]), w2_ref[...], preferred_element_type=F32) for n in nbs]
    a_lin = [jnp.dot(lo[n][:, 64:128], a2_ref[...], preferred_element_type=F32) for n in nbs]
    g = [jnp.dot(_sigmoid(lo[n][:, 128:256]), g2_ref[...], preferred_element_type=F32) for n in nbs]
    lw = [-jnp.exp(-_softplus(-(w0_ref[...] + w_lin[n])) - 0.5) for n in nbs]
    a = [_sigmoid(a0_ref[...] + a_lin[n]) for n in nbs]
    k = [jnp.where(real, k[n], 0.0) for n in nbs]
    v = [jnp.where(real, v[n], 0.0) for n in nbs]
    kk = [k[n] * kkp_ref[...] for n in nbs]
    kk = [kk[n] / jnp.maximum(jnp.sqrt(_head_sum(kk[n] * kk[n], ones_blk)), 1e-12) for n in nbs]
    bb = [kk[n] * a[n] for n in nbs]
    k = [k[n] * (1.0 + (a[n] - 1.0) * kap_ref[...]) for n in nbs]

    def wide(xs):
        return jnp.concatenate(xs, axis=1)

    ri = lax.broadcasted_iota(jnp.int32, (RW_C, RW_C), 0)
    ci = lax.broadcasted_iota(jnp.int32, (RW_C, RW_C), 1)
    tril_b = jnp.where(ri >= ci, 1.0, 0.0).astype(BF16)
    lw_w = wide(lw)
    cum = _dot_m01_x(tril_b, lw_w)
    e_in, e_out, e_prev = jnp.exp(cum), jnp.exp(-cum), jnp.exp(cum - lw_w)
    e_last = jnp.exp(cum[RW_C - 1:RW_C, :])
    a_all = -wide(kk) * e_prev
    bt_all = wide(bb) * e_out
    kt_all = wide(k) * e_out
    rt_all = wide(r) * e_in
    bh_all = bt_all * e_last
    kh_all = kt_all * e_last
    v_all = wide(v)

    lane = lax.broadcasted_iota(jnp.int32, (RW_C, LANES), 1)
    first = lane < RW_N
    r2 = lax.broadcasted_iota(jnp.int32, (LANES, LANES), 0)
    c2 = lax.broadcasted_iota(jnp.int32, (LANES, LANES), 1)
    strict, incl = r2 % RW_C > c2 % RW_C, r2 % RW_C >= c2 % RW_C
    eye = jnp.where(r2 == c2, 1.0, 0.0)
    zeros = jnp.zeros((LANES, LANES), F32)
    nt = (((1,), (1,)), ((), ()))

    def bd(x):
        return jnp.concatenate([jnp.where(first, x, 0.0), jnp.where(first, 0.0, x)], axis=0)

    def mm(x, y):
        return jnp.dot(x.astype(BF16), y.astype(BF16), preferred_element_type=F32)

    def cat(xs, axis):
        return jnp.concatenate(xs, axis=axis)

    prs = range(RW_NB * RW_PAIRS)
    cols = [slice(p * LANES, (p + 1) * LANES) for p in prs]
    a, bt, kt, rt, bh, kh, vv = ([bd(x[:, cs]) for cs in cols]
                                 for x in (a_all, bt_all, kt_all, rt_all, bh_all, kh_all, v_all))
    pm = [lax.dot_general(cat([a[p], rt[p]], 0).astype(BF16), cat([bt[p], kt[p]], 0).astype(BF16), nt,
                          preferred_element_type=F32) for p in prs]
    l_ak = [jnp.where(strict, pm[p][0:LANES, LANES:], 0.0) for p in prs]
    m_rb = [jnp.where(incl, pm[p][LANES:, 0:LANES], 0.0) for p in prs]
    m_rk = [jnp.where(incl, pm[p][LANES:, LANES:], 0.0) for p in prs]
    x = [jnp.where(strict, pm[p][0:LANES, 0:LANES], 0.0) for p in prs]
    t_inv = [eye + x[p] for p in prs]
    for _ in range(RW_C.bit_length() - 2):
        x = [mm(x[p], x[p]) for p in prs]
        t_inv = [t_inv[p] + mm(t_inv[p], x[p]) for p in prs]
    lv = [mm(l_ak[p], vv[p]) for p in prs]
    w12 = [mm(t_inv[p], cat([a[p], lv[p]], 1)) for p in prs]
    qy = [mm(cat([m_rb[p], m_rk[p]], 1), cat([w12[p], cat([zeros, vv[p]], 1)], 0)) for p in prs]
    g_m = [mm(w12[p][:, 0:LANES].T, bh[p]) for p in prs]
    hm = [mm(cat([w12[p][:, LANES:], vv[p]], 0).T, cat([bh[p], kh[p]], 0)) for p in prs]
    s = [s_ref[p // RW_PAIRS, p % RW_PAIRS] for p in prs]
    y = [lax.dot_general((rt[p] + qy[p][:, 0:LANES]).astype(BF16), s[p].astype(BF16), nt,
                         preferred_element_type=F32) + qy[p][:, LANES:] for p in prs]
    ys = [y[p][0:RW_C, :] + y[p][RW_C:, :] for p in prs]
    sg = [jnp.dot(cat(_split3(s[p]), 1), cat([g_m[p].astype(BF16)] * 3, 0), preferred_element_type=F32)
          for p in prs]
    for p in prs:
        s_ref[p // RW_PAIRS, p % RW_PAIRS] = s[p] * e_last[:, cols[p]] + sg[p] + hm[p]

    for n in nbs:
        yn = jnp.concatenate(ys[n * RW_PAIRS:(n + 1) * RW_PAIRS], axis=1)
        mean = _head_sum(yn, ones_blk) * (1.0 / RW_N)
        d = yn - mean
        var = _head_sum(d * d, ones_blk) * (1.0 / RW_N)
        yn = d * lax.rsqrt(var + RW_GN_EPS) * lg_ref[...] + lb_ref[...]
        bonus = _head_sum(r[n] * k[n] * rk_ref[...], ones_blk) * v[n]
        o_ref[n] = ((yn + bonus) * g[n]).astype(BF16)


def rw_mixer(z, mu, w0, w2, a0, a2, g2, kkp, kap, lg, lb, rk):
    w = RW_W
    c0 = GLA_MAIN // w
    lo0 = (GLA_MAIN + 3 * w) // RW_LORA
    z3 = z.reshape(BATCH, LP, CD_COLS)

    def zcol(width, k):
        return pl.BlockSpec((RW_NB, RW_C, width), lambda b, c, k=k: (b, c, k))

    def full(shape):
        return pl.BlockSpec(shape, lambda b, c: (0, 0))

    return pl.pallas_call(
        _rw_mixer_kernel,
        grid=(BATCH // RW_NB, LP // RW_C),
        in_specs=[zcol(w, c0), zcol(w, c0 + 1), zcol(w, c0 + 2), zcol(RW_LORA, lo0),
                  full((3, w)), full((1, RW_LORA)),
                  full((1, w)), full((64, w)), full((1, w)), full((64, w)), full((128, w)),
                  full((1, w)), full((1, w)), full((1, w)), full((1, w)), full((1, w))],
        out_specs=pl.BlockSpec((RW_NB, RW_C, w), lambda b, c: (b, c, 0)),
        out_shape=jax.ShapeDtypeStruct((BATCH, LP, w), BF16),
        scratch_shapes=[pltpu.VMEM((RW_NB, RW_PAIRS, LANES, LANES), F32),
                        pltpu.VMEM((RW_NB, SUBLANES, RW_COLS), F32)],
        compiler_params=_params(("parallel", "arbitrary")),
        name="rw_mixer",
    )(z3, z3, z3, z3, mu[:, :3 * w].reshape(3, w), mu[:, 3 * w:], w0, w2, a0, a2, g2, kkp, kap, lg, lb, rk
      ).reshape(MT, w)


CD_WROWS = 256


def _cd_weight_kernel(wt_ref, o_ref):
    pad = jnp.zeros((GLR_PAD - GLA_RANK, CD_WROWS), F32)
    perm = jnp.concatenate([wt_ref[0:GLA_MAIN, :], wt_ref[GLA_COLS:GLA_COLS + RW_COLS, :],
                            wt_ref[GLA_MAIN:GLA_COLS, :], pad], axis=0)
    o_ref[...] = perm.T.astype(BF16)


def cd_weight(w, layer):
    wt = jnp.swapaxes(w, 1, 2)
    cols = wt.shape[1]
    return pl.pallas_call(
        _cd_weight_kernel,
        grid=(D_MODEL // CD_WROWS,),
        in_specs=[pl.BlockSpec((None, cols, CD_WROWS), lambda i: (layer, 0, i))],
        out_specs=pl.BlockSpec((CD_WROWS, CD_COLS), lambda i: (i, 0)),
        out_shape=jax.ShapeDtypeStruct((D_MODEL, CD_COLS), BF16),
        compiler_params=_params(("parallel",)),
        name="cd_weight",
    )(wt)


def kernel(x, meta, ab_w_in, ab_conf_dw, ab_conf_dw_b, ab_conf_ln_g, ab_conf_ln_b, ab_sc_dw, ab_w_out,
           cd_w_in, cd_gla_w2, cd_gla_b, cd_gla_norm_g, cd_rw_mu, cd_rw_w0, cd_rw_w2, cd_rw_a0, cd_rw_a2,
           cd_rw_g2, cd_rw_kk, cd_rw_ka, cd_rw_rk, cd_rw_ln_g, cd_rw_ln_b, cd_w_out, norm_mix, norm_ffn,
           ffn_w_up, ffn_dw, ffn_w_down, norm_final):
    bs = x.shape[0]
    h = jnp.concatenate([jnp.zeros((bs, PAD, D_MODEL), x.dtype),
                         jnp.broadcast_to(meta[None].astype(x.dtype), (bs, N_META, D_MODEL)), x], axis=1)
    h = h.reshape(MT, D_MODEL)
    row2 = lambda p: p.reshape(1, -1)
    g_final = row2(norm_final)

    z = norm_matmul(h, row2(norm_mix[0]), ab_w_in[0].astype(BF16), TN_AB)
    a_act, s_act = mix_ab(z, ab_conf_dw[0], row2(ab_conf_dw_b[0]), row2(ab_conf_ln_g[0]),
                          row2(ab_conf_ln_b[0]), ab_sc_dw[0])
    h = out_proj(h, a_act, s_act, ab_w_out[0].astype(BF16))
    w_up, w_down = ffn_w_up.astype(BF16), ffn_w_down.astype(BF16)
    h = conv_ffn(h, row2(norm_ffn[0]), w_up, ffn_dw, w_down, g_final, 0, False)

    z = norm_matmul(h, row2(norm_mix[1]), cd_weight(cd_w_in, 0), TN_CD)
    w2p = jnp.concatenate([cd_gla_w2[0], jnp.zeros((GLR_PAD - GLA_RANK, GLA_QK), F32)], axis=0)
    o_act = gla(z, w2p, row2(cd_gla_b[0]), row2(cd_gla_norm_g[0]))
    y_act = rw_mixer(z, row2(cd_rw_mu[0]), row2(cd_rw_w0[0]), cd_rw_w2[0], row2(cd_rw_a0[0]), cd_rw_a2[0],
                     cd_rw_g2[0], row2(cd_rw_kk[0]), row2(cd_rw_ka[0]), row2(cd_rw_ln_g[0]), row2(cd_rw_ln_b[0]),
                     row2(cd_rw_rk[0]))
    h = out_proj(h, o_act, y_act, cd_w_out[0].astype(BF16))
    h = conv_ffn(h, row2(norm_ffn[1]), w_up, ffn_dw, w_down, g_final, 1, True)
    return h.reshape(bs, SEQ, D_MODEL)
```

```python
import functools

import jax
import jax.numpy as jnp
from jax import lax
from jax.experimental import pallas as pl
from jax.experimental.pallas import tpu as pltpu

D_MODEL = 2048
BATCH = 4
SEQ = 2048
N_META = 16
CONF_W = 1024
CONF_K = 31
SC_W = 1024
SC_K = 3
AB_IN = 5 * 1024
GLA_HEADS = 4
GLA_DK = 128
GLA_DV = 256
GLA_QK = GLA_HEADS * GLA_DK
GLA_V = GLA_HEADS * GLA_DV
GLA_RANK = 16
GLA_GATE_NORM = 16.0
GLA_CHUNK = 64
GLA_COLS = 2 * GLA_QK + 2 * GLA_V + GLA_RANK
RW_HEADS = 16
RW_N = 64
RW_W = RW_HEADS * RW_N
RW_LORA = 64 + 64 + 128
RW_COLS = 3 * RW_W + RW_LORA
RW_GN_EPS = 64e-5
D_FF = 5632
EPS = 1e-6
LN_EPS = 1e-5

PAD = (-N_META) % GLA_CHUNK
LP = PAD + N_META + SEQ
MT = BATCH * LP
FIRST = PAD
TM = LP // 3
TS = LP // 6
LANES = 128
SUBLANES = 8
GLR_PAD = LANES
GLA_MAIN = 2 * GLA_QK + 2 * GLA_V
CD_COLS = GLA_MAIN + RW_COLS + GLR_PAD
TN_AB = 1280
TN_CD = 2176
TN_FF = 512
TM_LAST = 512
FF_LAST_CHUNK = 128
HALO_FF = 16
HALO_CONF = 32
HALO_SC = 8
GLA_TB = TM
GLA_NB = 2
VMEM_LIMIT = 56 * 1024 * 1024

F32 = jnp.float32
BF16 = jnp.bfloat16


def _params(sem):
    return pltpu.CompilerParams(dimension_semantics=sem, vmem_limit_bytes=VMEM_LIMIT)


def _sigmoid(x):
    return 1.0 / (1.0 + jnp.exp(-x))


def _silu(x):
    return x * _sigmoid(x)


def _softplus(x):
    return jnp.maximum(x, 0.0) + jnp.log(1.0 + jnp.exp(-jnp.abs(x)))


def _split3(x):
    hi = x.astype(BF16)
    r = x - hi.astype(F32)
    mid = r.astype(BF16)
    lo = (r - mid.astype(F32)).astype(BF16)
    return hi, mid, lo


def _dot_x_m01(x, m01):
    return jnp.dot(jnp.concatenate(_split3(x), axis=1), jnp.concatenate([m01] * 3, axis=0),
                   preferred_element_type=F32)


def _dot_m01_x(m01, x):
    return jnp.dot(jnp.concatenate([m01] * 3, axis=1), jnp.concatenate(_split3(x), axis=0),
                   preferred_element_type=F32)


def _rms_rows(x, g):
    ms = jnp.mean(x * x, axis=-1, keepdims=True)
    return x * lax.rsqrt(ms + EPS) * g


NORM_CHUNK = 176


def _norm_rows_into(dst_ref, dst_row0, src_ref, g_ref, chunk=NORM_CHUNK):
    assert src_ref.shape[0] % chunk == 0 and chunk % 16 == 0

    def body(c, carry):
        r = pl.multiple_of(c * chunk, 16)
        x = src_ref[pl.ds(r, chunk), :]
        dst_ref[pl.ds(pl.multiple_of(dst_row0 + r, 16), chunk), :] = _rms_rows(x, g_ref[...]).astype(BF16)
        return carry

    lax.fori_loop(0, src_ref.shape[0] // chunk, body, 0)


def _norm_matmul_kernel(h_ref, g_ref, w_ref, o_ref, hn_ref):
    @pl.when(pl.program_id(1) == 0)
    def _():
        _norm_rows_into(hn_ref, 0, h_ref, g_ref)

    o_ref[...] = jnp.dot(hn_ref[...], w_ref[...], preferred_element_type=F32)


def norm_matmul(h, g, w, tn):
    n = w.shape[1]
    return pl.pallas_call(
        _norm_matmul_kernel,
        grid=(MT // TM, n // tn),
        in_specs=[
            pl.BlockSpec((TM, D_MODEL), lambda i, j: (i, 0)),
            pl.BlockSpec((1, D_MODEL), lambda i, j: (0, 0)),
            pl.BlockSpec((D_MODEL, tn), lambda i, j: (0, j)),
        ],
        out_specs=pl.BlockSpec((TM, tn), lambda i, j: (i, j)),
        out_shape=jax.ShapeDtypeStruct((MT, n), F32),
        scratch_shapes=[pltpu.VMEM((TM, D_MODEL), BF16)],
        compiler_params=_params(("parallel", "arbitrary")),
        name="norm_matmul",
    )(h, g, w)


def _zero_pad_rows(o_ref, tiles_per_batch):
    @pl.when(pl.program_id(0) % tiles_per_batch == 0)
    def _():
        o_ref[0:FIRST, :] = jnp.zeros((FIRST, o_ref.shape[1]), o_ref.dtype)


def _out_proj_kernel(h_ref, a1_ref, a2_ref, w1_ref, w2_ref, o_ref):
    acc = jnp.dot(a1_ref[...], w1_ref[...], preferred_element_type=F32)
    acc = acc + jnp.dot(a2_ref[...], w2_ref[...], preferred_element_type=F32)
    o_ref[...] = h_ref[...] + acc
    _zero_pad_rows(o_ref, LP // TS)


def out_proj(h, a1, a2, w):
    k1, k2 = a1.shape[1], a2.shape[1]
    assert k1 == k2 and w.shape[0] == k1 + k2
    w1 = w2 = w
    return pl.pallas_call(
        _out_proj_kernel,
        grid=(MT // TS,),
        in_specs=[
            pl.BlockSpec((TS, D_MODEL), lambda i: (i, 0)),
            pl.BlockSpec((TS, k1), lambda i: (i, 0)),
            pl.BlockSpec((TS, k2), lambda i: (i, 0)),
            pl.BlockSpec((k1, D_MODEL), lambda i: (0, 0)),
            pl.BlockSpec((k2, D_MODEL), lambda i: (1, 0)),
        ],
        out_specs=pl.BlockSpec((TS, D_MODEL), lambda i: (i, 0)),
        out_shape=jax.ShapeDtypeStruct((MT, D_MODEL), F32),
        compiler_params=_params(("parallel",)),
        name="out_proj",
    )(h, a1, a2, w1, w2)


def _ffn_kernel(h_ref, halo_ref, g_ref, wv_ref, wg_ref, dwv_ref, dwg_ref, wd_ref, gf_ref,
                o_ref, hn_ref, *, last_layer):
    j = pl.program_id(1)
    tm = h_ref.shape[0]
    chunk = FF_LAST_CHUNK if last_layer else NORM_CHUNK

    @pl.when(j == 0)
    def _():
        hn_ref[0:HALO_FF, :] = _rms_rows(halo_ref[...], g_ref[...]).astype(BF16)
        _norm_rows_into(hn_ref, HALO_FF, h_ref, g_ref, chunk)
        o_ref[...] = h_ref[...]

    hn = hn_ref[...]

    def conv(w_ref, dw_ref):
        u = jnp.dot(hn, w_ref[...], preferred_element_type=F32)
        u1 = pltpu.roll(u, 1, 0)
        u2 = pltpu.roll(u, 2, 0)
        c = u * dw_ref[2:3, :] + u1 * dw_ref[1:2, :] + u2 * dw_ref[0:1, :]
        return c[HALO_FF:, :]

    act = (_silu(conv(wg_ref, dwg_ref)) * conv(wv_ref, dwv_ref)).astype(BF16)
    for n in range(D_MODEL // TN_FF):
        cols = slice(n * TN_FF, (n + 1) * TN_FF)
        o_ref[:, cols] += jnp.dot(act, wd_ref[:, cols], preferred_element_type=F32)

    @pl.when(j == pl.num_programs(1) - 1)
    def _():
        if last_layer:
            def body(c, carry):
                rows = pl.ds(pl.multiple_of(c * chunk, 16), chunk)
                o_ref[rows, :] = _rms_rows(o_ref[rows, :], gf_ref[...])
                return carry

            lax.fori_loop(0, tm // chunk, body, 0)
        else:
            _zero_pad_rows(o_ref, LP // TM)


def conv_ffn(h, g, w_up, dw, w_down, g_final, layer, last_layer):
    nj = D_FF // TN_FF
    if last_layer:
        tm, rows_out = TM_LAST, BATCH * SEQ
        per_batch = SEQ // tm

        def row0(i):
            return pl.multiple_of((i // per_batch) * LP + (PAD + N_META) + (i % per_batch) * tm, HALO_FF)

        h_spec = pl.BlockSpec((pl.Element(tm), pl.Element(D_MODEL)), lambda i, j: (row0(i), 0))
        halo_spec = pl.BlockSpec((pl.Element(HALO_FF), pl.Element(D_MODEL)),
                                 lambda i, j: (pl.multiple_of(row0(i) - HALO_FF, HALO_FF), 0))
    else:
        tm, rows_out = TM, MT
        halo_blocks = tm // HALO_FF
        h_spec = pl.BlockSpec((tm, D_MODEL), lambda i, j: (i, 0))
        halo_spec = pl.BlockSpec((HALO_FF, D_MODEL), lambda i, j: (jnp.maximum(i * halo_blocks - 1, 0), 0))
    return pl.pallas_call(
        functools.partial(_ffn_kernel, last_layer=last_layer),
        grid=(rows_out // tm, nj),
        in_specs=[
            h_spec,
            halo_spec,
            pl.BlockSpec((1, D_MODEL), lambda i, j: (0, 0)),
            pl.BlockSpec((None, D_MODEL, TN_FF), lambda i, j: (layer, 0, j)),
            pl.BlockSpec((None, D_MODEL, TN_FF), lambda i, j: (layer, 0, nj + j)),
            pl.BlockSpec((None, 3, TN_FF), lambda i, j: (layer, 0, j)),
            pl.BlockSpec((None, 3, TN_FF), lambda i, j: (layer, 0, nj + j)),
            pl.BlockSpec((None, TN_FF, D_MODEL), lambda i, j: (layer, j, 0)),
            pl.BlockSpec((1, D_MODEL), lambda i, j: (0, 0)),
        ],
        out_specs=pl.BlockSpec((tm, D_MODEL), lambda i, j: (i, 0)),
        out_shape=jax.ShapeDtypeStruct((rows_out, D_MODEL), F32),
        scratch_shapes=[pltpu.VMEM((HALO_FF + tm, D_MODEL), BF16)],
        compiler_params=_params(("parallel", "arbitrary")),
        name="conv_ffn",
    )(h, h, g, w_up, w_up, dw, dw, w_down, g_final)


CONF_RB = 88


def _mix_ab_kernel(av_ref, ag_ref, sb_ref, sc_ref, sx_ref, hav_ref, hag_ref, hsc_ref, hsx_ref,
                   cw_ref, cb_ref, lg_ref, lb_ref, sw_ref, a_ref, s_ref, xs_ref, sh_ref, ac_ref, cx_ref):
    xs_ref[0:HALO_CONF, :] = hav_ref[...] * _sigmoid(hag_ref[...])
    xs_ref[HALO_CONF:HALO_CONF + TS, :] = av_ref[...] * _sigmoid(ag_ref[...])
    base = HALO_CONF - (CONF_K - 1)

    sub = SUBLANES
    sh_rows = HALO_CONF + TS
    xs_ref[sh_rows:, :] = jnp.zeros((sub, CONF_W), F32)

    def lane_chunk(c, carry):
        cols = pl.ds(pl.multiple_of(c * LANES, LANES), LANES)
        for s in range(sub):
            sh_ref[s] = xs_ref[pl.ds(s, sh_rows), cols]
        for rb in range(TS // CONF_RB):
            r0 = rb * CONF_RB
            acc = jnp.broadcast_to(cb_ref[:, cols], (CONF_RB, LANES))
            for t in range(CONF_K):
                off = base + t
                acc = acc + sh_ref[off % sub, pl.ds(r0 + off - off % sub, CONF_RB), :] * cw_ref[pl.ds(t, 1), cols]
            ac_ref[pl.ds(r0, CONF_RB), cols] = acc
        return carry

    lax.fori_loop(0, CONF_W // LANES, lane_chunk, 0)

    def norm_rows(c, carry):
        rows = pl.ds(pl.multiple_of(c * NORM_CHUNK, 16), NORM_CHUNK)
        a = ac_ref[rows, :]
        mu = jnp.mean(a, axis=-1, keepdims=True)
        d = a - mu
        var = jnp.mean(d * d, axis=-1, keepdims=True)
        y = d * lax.rsqrt(var + LN_EPS) * lg_ref[...] + lb_ref[...]
        a_ref[rows, :] = _silu(y).astype(BF16)
        return carry

    lax.fori_loop(0, TS // NORM_CHUNK, norm_rows, 0)

    cx_ref[0:HALO_SC, :] = hsc_ref[...] * hsx_ref[...]
    cx_ref[HALO_SC:, :] = sc_ref[...] * sx_ref[...]
    cv = cx_ref[pl.ds(HALO_SC - 2, TS), :] * sw_ref[0:1, :]
    cv = cv + cx_ref[pl.ds(HALO_SC - 1, TS), :] * sw_ref[1:2, :]
    cv = cv + cx_ref[pl.ds(HALO_SC, TS), :] * sw_ref[2:3, :]
    s_ref[...] = (sb_ref[...] * cv).astype(BF16)


def mix_ab(z, conf_dw, conf_b, ln_g, ln_b, sc_dw):
    w = CONF_W
    hc, hs = TS // HALO_CONF, TS // HALO_SC

    def col(k):
        return pl.BlockSpec((TS, w), lambda i, k=k: (i, k))

    def halo(rows, per_tile, k):
        return pl.BlockSpec((rows, w), lambda i, k=k: (jnp.maximum(i * per_tile - 1, 0), k))

    def full(r):
        return pl.BlockSpec((r, w), lambda i: (0, 0))

    return pl.pallas_call(
        _mix_ab_kernel,
        grid=(MT // TS,),
        in_specs=[col(0), col(1), col(2), col(3), col(4),
                  halo(HALO_CONF, hc, 0), halo(HALO_CONF, hc, 1),
                  halo(HALO_SC, hs, 3), halo(HALO_SC, hs, 4),
                  full(CONF_K), full(1), full(1), full(1), full(SC_K)],
        out_specs=[pl.BlockSpec((TS, w), lambda i: (i, 0)), pl.BlockSpec((TS, w), lambda i: (i, 0))],
        out_shape=[jax.ShapeDtypeStruct((MT, w), BF16), jax.ShapeDtypeStruct((MT, w), BF16)],
        scratch_shapes=[pltpu.VMEM((HALO_CONF + TS + SUBLANES, w), F32),
                        pltpu.VMEM((SUBLANES, HALO_CONF + TS, LANES), F32),
                        pltpu.VMEM((TS, w), F32), pltpu.VMEM((HALO_SC + TS, w), F32)],
        compiler_params=_params(("parallel",)),
        name="mix_ab",
    )(z, z, z, z, z, z, z, z, z, conf_dw, conf_b, ln_g, ln_b, sc_dw)


def _gla_kernel(q_ref, k_ref, v_ref, go_ref, glr_ref, w2_ref, b_ref, gn_ref, o_ref, st_ref):
    c_len = GLA_CHUNK
    tb = pl.program_id(1)

    @pl.when(tb == 0)
    def _():
        st_ref[...] = jnp.zeros(st_ref.shape, F32)

    ri = lax.broadcasted_iota(jnp.int32, (c_len, c_len), 0)
    ci = lax.broadcasted_iota(jnp.int32, (c_len, c_len), 1)
    tril = ri >= ci
    tril_b = jnp.where(tril, 1.0, 0.0).astype(BF16)
    row = lax.broadcasted_iota(jnp.int32, (c_len, 1), 0)
    nt_dims = (((1,), (1,)), ((), ()))

    def chunk(c, carry):
        rows = pl.ds(pl.multiple_of(c * c_len, c_len), c_len)
        real = tb * GLA_TB + c * c_len + row >= FIRST
        nb, hs = range(GLA_NB), range(GLA_HEADS)
        ch = [(n, h) for n in nb for h in hs]
        lin = [jnp.dot(glr_ref[n, rows, :], w2_ref[...], preferred_element_type=F32) + b_ref[...] for n in nb]
        log_a = [jnp.where(real, -_softplus(-lin[n]) / GLA_GATE_NORM, 0.0) for n in nb]
        cum_all = [_dot_m01_x(tril_b, log_a[n]) for n in nb]
        q_all = [q_ref[n, rows, :] * (GLA_DK ** -0.5) for n in nb]
        k_all = [k_ref[n, rows, :] for n in nb]
        v_all = [v_ref[n, rows, :] for n in nb]
        kc = [slice(h * GLA_DK, (h + 1) * GLA_DK) for h in hs]
        vc = [slice(h * GLA_DV, (h + 1) * GLA_DV) for h in hs]
        cum = [cum_all[n][:, kc[h]] for n, h in ch]
        last = [x[c_len - 1:c_len, :] for x in cum]
        v = [v_all[n][:, vc[h]] for n, h in ch]
        q_dec = [q_all[n][:, kc[h]] * jnp.exp(cum[i]) for i, (n, h) in enumerate(ch)]
        k_dec = [k_all[n][:, kc[h]] * jnp.exp(-cum[i]) for i, (n, h) in enumerate(ch)]
        k_st = [k_all[n][:, kc[h]] * jnp.exp(last[i] - cum[i]) for i, (n, h) in enumerate(ch)]
        scores = [jnp.where(tril, lax.dot_general(q_dec[i], k_dec[i], nt_dims, preferred_element_type=F32), 0.0)
                  for i in range(len(ch))]
        st = [st_ref[n, h] for n, h in ch]
        o = [jnp.dot(scores[i], v[i], preferred_element_type=F32) for i in range(len(ch))]
        o = [o[i] + lax.dot_general(q_dec[i], st[i], nt_dims, preferred_element_type=F32) for i in range(len(ch))]
        upd = [jnp.dot(v[i].T, k_st[i], preferred_element_type=F32) for i in range(len(ch))]
        for i, (n, h) in enumerate(ch):
            st_ref[n, h] = st[i] * jnp.exp(last[i]) + upd[i]
        o = [x * lax.rsqrt(jnp.mean(x * x, axis=-1, keepdims=True) + EPS) for x in o]
        for n in nb:
            go = go_ref[n, rows, :]
            outs = [o[n * GLA_HEADS + h] * gn_ref[:, vc[h]] * _silu(go[:, vc[h]]) for h in hs]
            o_ref[n, rows, :] = jnp.concatenate(outs, axis=-1).astype(BF16)
        return carry

    lax.fori_loop(0, GLA_TB // c_len, chunk, 0)


def gla(z, w2p, b, gn):
    z3 = z.reshape(BATCH, LP, CD_COLS)

    def zcol(width, k):
        return pl.BlockSpec((GLA_NB, GLA_TB, width), lambda b_, t, k=k: (b_, t, k))

    def full(shape):
        return pl.BlockSpec(shape, lambda b_, t: (0, 0))

    return pl.pallas_call(
        _gla_kernel,
        grid=(BATCH // GLA_NB, LP // GLA_TB),
        in_specs=[zcol(GLA_QK, 0), zcol(GLA_QK, 1), zcol(GLA_V, 1), zcol(GLA_V, 2),
                  zcol(GLR_PAD, (GLA_MAIN + RW_COLS) // GLR_PAD),
                  full((GLR_PAD, GLA_QK)), full((1, GLA_QK)), full((1, GLA_V))],
        out_specs=pl.BlockSpec((GLA_NB, GLA_TB, GLA_V), lambda b_, t: (b_, t, 0)),
        out_shape=jax.ShapeDtypeStruct((BATCH, LP, GLA_V), BF16),
        scratch_shapes=[pltpu.VMEM((GLA_NB, GLA_HEADS, GLA_DV, GLA_DK), F32)],
        compiler_params=_params(("parallel", "arbitrary")),
        name="gla",
    )(z3, z3, z3, z3, z3, w2p, b, gn).reshape(MT, GLA_V)


def _head_sum(x, ones_blk):
    n = x.shape[0]
    stacked = jnp.concatenate([x[:, c * LANES:(c + 1) * LANES] for c in range(RW_W // LANES)], axis=0)
    out = _dot_x_m01(stacked, ones_blk)
    return jnp.concatenate([out[c * n:(c + 1) * n, :] for c in range(RW_W // LANES)], axis=-1)


def _head_ones():
    r = lax.broadcasted_iota(jnp.int32, (LANES, LANES), 0) // RW_N
    c = lax.broadcasted_iota(jnp.int32, (LANES, LANES), 1) // RW_N
    return jnp.where(r == c, 1.0, 0.0).astype(BF16)


RW_PAIRS = RW_HEADS // 2
RW_C = 64
RW_NB = 2


def _rw_mixer_kernel(r_ref, k_ref, v_ref, lo_ref, mu_ref, mulo_ref, w0_ref, w2_ref, a0_ref, a2_ref, g2_ref,
                     kkp_ref, kap_ref, lg_ref, lb_ref, rk_ref, o_ref, s_ref, prev_ref):
    c = pl.program_id(1)

    @pl.when(c == 0)
    def _():
        s_ref[...] = jnp.zeros(s_ref.shape, F32)
        prev_ref[...] = jnp.zeros(prev_ref.shape, F32)

    row = lax.broadcasted_iota(jnp.int32, (RW_C, 1), 0)
    ones_blk = _head_ones()
    real = jnp.logical_or(c != 0, row >= FIRST)
    nbs = range(RW_NB)

    def shift_mix(x, n, col0, mu):
        cols = slice(col0, col0 + x.shape[1])
        prev = jnp.where(row == 0, prev_ref[n, 0:1, cols], pltpu.roll(x, 1, 0))
        prev_ref[n, 0:1, cols] = x[RW_C - 1:RW_C, :]
        return x + (prev - x) * mu

    r = [shift_mix(r_ref[n], n, 0, mu_ref[0:1, :]) for n in nbs]
    k = [shift_mix(k_ref[n], n, RW_W, mu_ref[1:2, :]) for n in nbs]
    v = [shift_mix(v_ref[n], n, 2 * RW_W, mu_ref[2:3, :]) for n in nbs]
    lo = [shift_mix(lo_ref[n], n, 3 * RW_W, mulo_ref[...]) for n in nbs]
    w_lin = [jnp.dot(jnp.tanh(lo[n][:, 0:64]), w2_ref[...], preferred_element_type=F32) for n in nbs]
    a_lin = [jnp.dot(lo[n][:, 64:128], a2_ref[...], preferred_element_type=F32) for n in nbs]
    g = [jnp.dot(_sigmoid(lo[n][:, 128:256]), g2_ref[...], preferred_element_type=F32) for n in nbs]
    lw = [-jnp.exp(-_softplus(-(w0_ref[...] + w_lin[n])) - 0.5) for n in nbs]
    a = [_sigmoid(a0_ref[...] + a_lin[n]) for n in nbs]
    k = [jnp.where(real, k[n], 0.0) for n in nbs]
    v = [jnp.where(real, v[n], 0.0) for n in nbs]
    kk = [k[n] * kkp_ref[...] for n in nbs]
    kk = [kk[n] / jnp.maximum(jnp.sqrt(_head_sum(kk[n] * kk[n], ones_blk)), 1e-12) for n in nbs]
    bb = [kk[n] * a[n] for n in nbs]
    k = [k[n] * (1.0 + (a[n] - 1.0) * kap_ref[...]) for n in nbs]

    def wide(xs):
        return jnp.concatenate(xs, axis=1)

    ri = lax.broadcasted_iota(jnp.int32, (RW_C, RW_C), 0)
    ci = lax.broadcasted_iota(jnp.int32, (RW_C, RW_C), 1)
    tril_b = jnp.where(ri >= ci, 1.0, 0.0).astype(BF16)
    lw_w = wide(lw)
    cum = _dot_m01_x(tril_b, lw_w)
    e_in, e_out, e_prev = jnp.exp(cum), jnp.exp(-cum), jnp.exp(cum - lw_w)
    e_last = jnp.exp(cum[RW_C - 1:RW_C, :])
    a_all = -wide(kk) * e_prev
    bt_all = wide(bb) * e_out
    kt_all = wide(k) * e_out
    rt_all = wide(r) * e_in
    bh_all = bt_all * e_last
    kh_all = kt_all * e_last
    v_all = wide(v)

    lane = lax.broadcasted_iota(jnp.int32, (RW_C, LANES), 1)
    t_idx = lax.broadcasted_iota(jnp.int32, (RW_C, LANES), 0)
    first = lane < RW_N
    strict, incl = t_idx > lane % RW_C, t_idx >= lane % RW_C
    eye = jnp.where(t_idx == lane % RW_C, 1.0, 0.0)
    r2 = lax.broadcasted_iota(jnp.int32, (2 * LANES, LANES), 0)
    c2 = lax.broadcasted_iota(jnp.int32, (2 * LANES, LANES), 1)
    same_head = (r2 % LANES) // RW_N == c2 // RW_N
    zeros = jnp.zeros((LANES, LANES), F32)
    nt = (((1,), (1,)), ((), ()))

    def bd(x):
        return jnp.concatenate([jnp.where(first, x, 0.0), jnp.where(first, 0.0, x)], axis=0)

    def mm(x, y):
        return jnp.dot(x.astype(BF16), y.astype(BF16), preferred_element_type=F32)

    def cat(xs, axis):
        return jnp.concatenate(xs, axis=axis)

    prs = range(RW_NB * RW_PAIRS)
    cols = [slice(p * LANES, (p + 1) * LANES) for p in prs]
    a, bt, kt, rt, bh, kh, vv = ([x[:, cs] for cs in cols]
                                 for x in (a_all, bt_all, kt_all, rt_all, bh_all, kh_all, v_all))
    v_bd = [bd(vv[p]) for p in prs]
    pm = [lax.dot_general(cat([a[p], rt[p]], 0).astype(BF16), cat([bd(bt[p]), bd(kt[p])], 0).astype(BF16), nt,
                          preferred_element_type=F32) for p in prs]
    l_ak = [jnp.where(strict, pm[p][0:RW_C, LANES:], 0.0) for p in prs]
    m_rb = [jnp.where(incl, pm[p][RW_C:, 0:LANES], 0.0) for p in prs]
    m_rk = [jnp.where(incl, pm[p][RW_C:, LANES:], 0.0) for p in prs]
    x = [jnp.where(strict, pm[p][0:RW_C, 0:LANES], 0.0) for p in prs]
    t_inv = [eye + x[p] for p in prs]
    x_bd = [bd(x[p]) for p in prs]
    for _ in range(RW_C.bit_length() - 2):
        x = [mm(x[p], x_bd[p]) for p in prs]
        x_bd = [bd(x[p]) for p in prs]
        t_inv = [t_inv[p] + mm(t_inv[p], x_bd[p]) for p in prs]
    lv = [mm(l_ak[p], v_bd[p]) for p in prs]
    w12 = [mm(t_inv[p], cat([bd(a[p]), bd(lv[p])], 1)) for p in prs]
    wv = [cat([w12[p], cat([jnp.zeros((RW_C, LANES), F32), vv[p]], 1)], 0) for p in prs]
    wv_bd = [cat([cat([bd(w12[p][:, 0:LANES]), bd(w12[p][:, LANES:])], 1), cat([zeros, v_bd[p]], 1)], 0)
             for p in prs]
    qy = [mm(cat([m_rb[p], m_rk[p]], 1), wv_bd[p]) for p in prs]
    gh = [jnp.where(same_head, mm(wv[p].T, cat([bh[p], kh[p]], 0)), 0.0) for p in prs]
    s = [s_ref[p // RW_PAIRS, p % RW_PAIRS] for p in prs]
    ys = [lax.dot_general((rt[p] + qy[p][:, 0:LANES]).astype(BF16), s[p].astype(BF16), nt,
                          preferred_element_type=F32) + qy[p][:, LANES:] for p in prs]
    sg = [jnp.dot(cat(_split3(s[p]), 1), cat([gh[p][0:LANES, :].astype(BF16)] * 3, 0), preferred_element_type=F32)
          for p in prs]
    for p in prs:
        s_ref[p // RW_PAIRS, p % RW_PAIRS] = s[p] * e_last[:, cols[p]] + sg[p] + gh[p][LANES:, :]

    for n in nbs:
        yn = jnp.concatenate(ys[n * RW_PAIRS:(n + 1) * RW_PAIRS], axis=1)
        mean = _head_sum(yn, ones_blk) * (1.0 / RW_N)
        d = yn - mean
        var = _head_sum(d * d, ones_blk) * (1.0 / RW_N)
        yn = d * lax.rsqrt(var + RW_GN_EPS) * lg_ref[...] + lb_ref[...]
        bonus = _head_sum(r[n] * k[n] * rk_ref[...], ones_blk) * v[n]
        o_ref[n] = ((yn + bonus) * g[n]).astype(BF16)


def rw_mixer(z, mu, w0, w2, a0, a2, g2, kkp, kap, lg, lb, rk):
    w = RW_W
    c0 = GLA_MAIN // w
    lo0 = (GLA_MAIN + 3 * w) // RW_LORA
    z3 = z.reshape(BATCH, LP, CD_COLS)

    def zcol(width, k):
        return pl.BlockSpec((RW_NB, RW_C, width), lambda b, c, k=k: (b, c, k))

    def full(shape):
        return pl.BlockSpec(shape, lambda b, c: (0, 0))

    return pl.pallas_call(
        _rw_mixer_kernel,
        grid=(BATCH // RW_NB, LP // RW_C),
        in_specs=[zcol(w, c0), zcol(w, c0 + 1), zcol(w, c0 + 2), zcol(RW_LORA, lo0),
                  full((3, w)), full((1, RW_LORA)),
                  full((1, w)), full((64, w)), full((1, w)), full((64, w)), full((128, w)),
                  full((1, w)), full((1, w)), full((1, w)), full((1, w)), full((1, w))],
        out_specs=pl.BlockSpec((RW_NB, RW_C, w), lambda b, c: (b, c, 0)),
        out_shape=jax.ShapeDtypeStruct((BATCH, LP, w), BF16),
        scratch_shapes=[pltpu.VMEM((RW_NB, RW_PAIRS, LANES, LANES), F32),
                        pltpu.VMEM((RW_NB, SUBLANES, RW_COLS), F32)],
        compiler_params=_params(("parallel", "arbitrary")),
        name="rw_mixer",
    )(z3, z3, z3, z3, mu[:, :3 * w].reshape(3, w), mu[:, 3 * w:], w0, w2, a0, a2, g2, kkp, kap, lg, lb, rk
      ).reshape(MT, w)


CD_WROWS = 256


def _cd_weight_kernel(wt_ref, o_ref):
    pad = jnp.zeros((GLR_PAD - GLA_RANK, CD_WROWS), F32)
    perm = jnp.concatenate([wt_ref[0:GLA_MAIN, :], wt_ref[GLA_COLS:GLA_COLS + RW_COLS, :],
                            wt_ref[GLA_MAIN:GLA_COLS, :], pad], axis=0)
    o_ref[...] = perm.T.astype(BF16)


def cd_weight(w, layer):
    wt = jnp.swapaxes(w, 1, 2)
    cols = wt.shape[1]
    return pl.pallas_call(
        _cd_weight_kernel,
        grid=(D_MODEL // CD_WROWS,),
        in_specs=[pl.BlockSpec((None, cols, CD_WROWS), lambda i: (layer, 0, i))],
        out_specs=pl.BlockSpec((CD_WROWS, CD_COLS), lambda i: (i, 0)),
        out_shape=jax.ShapeDtypeStruct((D_MODEL, CD_COLS), BF16),
        compiler_params=_params(("parallel",)),
        name="cd_weight",
    )(wt)


def kernel(x, meta, ab_w_in, ab_conf_dw, ab_conf_dw_b, ab_conf_ln_g, ab_conf_ln_b, ab_sc_dw, ab_w_out,
           cd_w_in, cd_gla_w2, cd_gla_b, cd_gla_norm_g, cd_rw_mu, cd_rw_w0, cd_rw_w2, cd_rw_a0, cd_rw_a2,
           cd_rw_g2, cd_rw_kk, cd_rw_ka, cd_rw_rk, cd_rw_ln_g, cd_rw_ln_b, cd_w_out, norm_mix, norm_ffn,
           ffn_w_up, ffn_dw, ffn_w_down, norm_final):
    bs = x.shape[0]
    h = jnp.concatenate([jnp.zeros((bs, PAD, D_MODEL), x.dtype),
                         jnp.broadcast_to(meta[None].astype(x.dtype), (bs, N_META, D_MODEL)), x], axis=1)
    h = h.reshape(MT, D_MODEL)
    row2 = lambda p: p.reshape(1, -1)
    g_final = row2(norm_final)

    z = norm_matmul(h, row2(norm_mix[0]), ab_w_in[0].astype(BF16), TN_AB)
    a_act, s_act = mix_ab(z, ab_conf_dw[0], row2(ab_conf_dw_b[0]), row2(ab_conf_ln_g[0]),
                          row2(ab_conf_ln_b[0]), ab_sc_dw[0])
    h = out_proj(h, a_act, s_act, ab_w_out[0].astype(BF16))
    w_up, w_down = ffn_w_up.astype(BF16), ffn_w_down.astype(BF16)
    h = conv_ffn(h, row2(norm_ffn[0]), w_up, ffn_dw, w_down, g_final, 0, False)

    z = norm_matmul(h, row2(norm_mix[1]), cd_weight(cd_w_in, 0), TN_CD)
    w2p = jnp.concatenate([cd_gla_w2[0], jnp.zeros((GLR_PAD - GLA_RANK, GLA_QK), F32)], axis=0)
    o_act = gla(z, w2p, row2(cd_gla_b[0]), row2(cd_gla_norm_g[0]))
    y_act = rw_mixer(z, row2(cd_rw_mu[0]), row2(cd_rw_w0[0]), cd_rw_w2[0], row2(cd_rw_a0[0]), cd_rw_a2[0],
                     cd_rw_g2[0], row2(cd_rw_kk[0]), row2(cd_rw_ka[0]), row2(cd_rw_ln_g[0]), row2(cd_rw_ln_b[0]),
                     row2(cd_rw_rk[0]))
    h = out_proj(h, o_act, y_act, cd_w_out[0].astype(BF16))
    h = conv_ffn(h, row2(norm_ffn[1]), w_up, ffn_dw, w_down, g_final, 1, True)
    return h.reshape(bs, SEQ, D_MODEL)
```

```python
import functools

import jax
import jax.numpy as jnp
from jax import lax
from jax.experimental import pallas as pl
from jax.experimental.pallas import tpu as pltpu

D_MODEL = 2048
BATCH = 4
SEQ = 2048
N_META = 16
CONF_W = 1024
CONF_K = 31
SC_W = 1024
SC_K = 3
AB_IN = 5 * 1024
GLA_HEADS = 4
GLA_DK = 128
GLA_DV = 256
GLA_QK = GLA_HEADS * GLA_DK
GLA_V = GLA_HEADS * GLA_DV
GLA_RANK = 16
GLA_GATE_NORM = 16.0
GLA_CHUNK = 64
GLA_COLS = 2 * GLA_QK + 2 * GLA_V + GLA_RANK
RW_HEADS = 16
RW_N = 64
RW_W = RW_HEADS * RW_N
RW_LORA = 64 + 64 + 128
RW_COLS = 3 * RW_W + RW_LORA
RW_GN_EPS = 64e-5
D_FF = 5632
EPS = 1e-6
LN_EPS = 1e-5

PAD = (-N_META) % GLA_CHUNK
LP = PAD + N_META + SEQ
MT = BATCH * LP
FIRST = PAD
TM = LP // 3
TS = LP // 6
LANES = 128
SUBLANES = 8
GLR_PAD = LANES
GLA_MAIN = 2 * GLA_QK + 2 * GLA_V
CD_COLS = GLA_MAIN + RW_COLS + GLR_PAD
TN_AB = 1280
TN_CD = 2176
TN_FF = 512
TM_LAST = 512
FF_LAST_CHUNK = 128
HALO_FF = 16
HALO_CONF = 32
HALO_SC = 8
GLA_TB = TM
GLA_NB = 2
VMEM_LIMIT = 56 * 1024 * 1024

F32 = jnp.float32
BF16 = jnp.bfloat16


def _params(sem):
    return pltpu.CompilerParams(dimension_semantics=sem, vmem_limit_bytes=VMEM_LIMIT)


def _sigmoid(x):
    return 1.0 / (1.0 + jnp.exp(-x))


def _silu(x):
    return x * _sigmoid(x)


def _softplus(x):
    return jnp.maximum(x, 0.0) + jnp.log(1.0 + jnp.exp(-jnp.abs(x)))


def _split3(x):
    hi = x.astype(BF16)
    r = x - hi.astype(F32)
    mid = r.astype(BF16)
    lo = (r - mid.astype(F32)).astype(BF16)
    return hi, mid, lo


def _dot_x_m01(x, m01):
    return jnp.dot(jnp.concatenate(_split3(x), axis=1), jnp.concatenate([m01] * 3, axis=0),
                   preferred_element_type=F32)


def _dot_m01_x(m01, x):
    return jnp.dot(jnp.concatenate([m01] * 3, axis=1), jnp.concatenate(_split3(x), axis=0),
                   preferred_element_type=F32)


def _rms_rows(x, g):
    ms = jnp.mean(x * x, axis=-1, keepdims=True)
    return x * lax.rsqrt(ms + EPS) * g


NORM_CHUNK = 176


def _norm_rows_into(dst_ref, dst_row0, src_ref, g_ref, chunk=NORM_CHUNK):
    assert src_ref.shape[0] % chunk == 0 and chunk % 16 == 0

    def body(c, carry):
        r = pl.multiple_of(c * chunk, 16)
        x = src_ref[pl.ds(r, chunk), :]
        dst_ref[pl.ds(pl.multiple_of(dst_row0 + r, 16), chunk), :] = _rms_rows(x, g_ref[...]).astype(BF16)
        return carry

    lax.fori_loop(0, src_ref.shape[0] // chunk, body, 0)


def _norm_matmul_kernel(h_ref, g_ref, w_ref, o_ref, hn_ref):
    @pl.when(pl.program_id(1) == 0)
    def _():
        _norm_rows_into(hn_ref, 0, h_ref, g_ref)

    o_ref[...] = jnp.dot(hn_ref[...], w_ref[...], preferred_element_type=F32)


def norm_matmul(h, g, w, tn):
    n = w.shape[1]
    return pl.pallas_call(
        _norm_matmul_kernel,
        grid=(MT // TM, n // tn),
        in_specs=[
            pl.BlockSpec((TM, D_MODEL), lambda i, j: (i, 0)),
            pl.BlockSpec((1, D_MODEL), lambda i, j: (0, 0)),
            pl.BlockSpec((D_MODEL, tn), lambda i, j: (0, j)),
        ],
        out_specs=pl.BlockSpec((TM, tn), lambda i, j: (i, j)),
        out_shape=jax.ShapeDtypeStruct((MT, n), F32),
        scratch_shapes=[pltpu.VMEM((TM, D_MODEL), BF16)],
        compiler_params=_params(("parallel", "arbitrary")),
        name="norm_matmul",
    )(h, g, w)


def _zero_pad_rows(o_ref, tiles_per_batch):
    @pl.when(pl.program_id(0) % tiles_per_batch == 0)
    def _():
        o_ref[0:FIRST, :] = jnp.zeros((FIRST, o_ref.shape[1]), o_ref.dtype)


def _out_proj_kernel(h_ref, a1_ref, a2_ref, w1_ref, w2_ref, o_ref):
    acc = jnp.dot(a1_ref[...], w1_ref[...], preferred_element_type=F32)
    acc = acc + jnp.dot(a2_ref[...], w2_ref[...], preferred_element_type=F32)
    o_ref[...] = h_ref[...] + acc
    _zero_pad_rows(o_ref, LP // TS)


def out_proj(h, a1, a2, w):
    k1, k2 = a1.shape[1], a2.shape[1]
    assert k1 == k2 and w.shape[0] == k1 + k2
    w1 = w2 = w
    return pl.pallas_call(
        _out_proj_kernel,
        grid=(MT // TS,),
        in_specs=[
            pl.BlockSpec((TS, D_MODEL), lambda i: (i, 0)),
            pl.BlockSpec((TS, k1), lambda i: (i, 0)),
            pl.BlockSpec((TS, k2), lambda i: (i, 0)),
            pl.BlockSpec((k1, D_MODEL), lambda i: (0, 0)),
            pl.BlockSpec((k2, D_MODEL), lambda i: (1, 0)),
        ],
        out_specs=pl.BlockSpec((TS, D_MODEL), lambda i: (i, 0)),
        out_shape=jax.ShapeDtypeStruct((MT, D_MODEL), F32),
        compiler_params=_params(("parallel",)),
        name="out_proj",
    )(h, a1, a2, w1, w2)


def _ffn_kernel(h_ref, halo_ref, g_ref, wv_ref, wg_ref, dwv_ref, dwg_ref, wd_ref, gf_ref,
                o_ref, hn_ref, *, last_layer):
    j = pl.program_id(1)
    tm = h_ref.shape[0]
    chunk = FF_LAST_CHUNK if last_layer else NORM_CHUNK

    @pl.when(j == 0)
    def _():
        hn_ref[0:HALO_FF, :] = _rms_rows(halo_ref[...], g_ref[...]).astype(BF16)
        _norm_rows_into(hn_ref, HALO_FF, h_ref, g_ref, chunk)
        o_ref[...] = h_ref[...]

    hn = hn_ref[...]

    def conv(w_ref, dw_ref):
        u = jnp.dot(hn, w_ref[...], preferred_element_type=F32)
        u1 = pltpu.roll(u, 1, 0)
        u2 = pltpu.roll(u, 2, 0)
        c = u * dw_ref[2:3, :] + u1 * dw_ref[1:2, :] + u2 * dw_ref[0:1, :]
        return c[HALO_FF:, :]

    act = (_silu(conv(wg_ref, dwg_ref)) * conv(wv_ref, dwv_ref)).astype(BF16)
    for n in range(D_MODEL // TN_FF):
        cols = slice(n * TN_FF, (n + 1) * TN_FF)
        o_ref[:, cols] += jnp.dot(act, wd_ref[:, cols], preferred_element_type=F32)

    @pl.when(j == pl.num_programs(1) - 1)
    def _():
        if last_layer:
            def body(c, carry):
                rows = pl.ds(pl.multiple_of(c * chunk, 16), chunk)
                o_ref[rows, :] = _rms_rows(o_ref[rows, :], gf_ref[...])
                return carry

            lax.fori_loop(0, tm // chunk, body, 0)
        else:
            _zero_pad_rows(o_ref, LP // TM)


def conv_ffn(h, g, w_up, dw, w_down, g_final, layer, last_layer):
    nj = D_FF // TN_FF
    if last_layer:
        tm, rows_out = TM_LAST, BATCH * SEQ
        per_batch = SEQ // tm

        def row0(i):
            return pl.multiple_of((i // per_batch) * LP + (PAD + N_META) + (i % per_batch) * tm, HALO_FF)

        h_spec = pl.BlockSpec((pl.Element(tm), pl.Element(D_MODEL)), lambda i, j: (row0(i), 0))
        halo_spec = pl.BlockSpec((pl.Element(HALO_FF), pl.Element(D_MODEL)),
                                 lambda i, j: (pl.multiple_of(row0(i) - HALO_FF, HALO_FF), 0))
    else:
        tm, rows_out = TM, MT
        halo_blocks = tm // HALO_FF
        h_spec = pl.BlockSpec((tm, D_MODEL), lambda i, j: (i, 0))
        halo_spec = pl.BlockSpec((HALO_FF, D_MODEL), lambda i, j: (jnp.maximum(i * halo_blocks - 1, 0), 0))
    return pl.pallas_call(
        functools.partial(_ffn_kernel, last_layer=last_layer),
        grid=(rows_out // tm, nj),
        in_specs=[
            h_spec,
            halo_spec,
            pl.BlockSpec((1, D_MODEL), lambda i, j: (0, 0)),
            pl.BlockSpec((None, D_MODEL, TN_FF), lambda i, j: (layer, 0, j)),
            pl.BlockSpec((None, D_MODEL, TN_FF), lambda i, j: (layer, 0, nj + j)),
            pl.BlockSpec((None, 3, TN_FF), lambda i, j: (layer, 0, j)),
            pl.BlockSpec((None, 3, TN_FF), lambda i, j: (layer, 0, nj + j)),
            pl.BlockSpec((None, TN_FF, D_MODEL), lambda i, j: (layer, j, 0)),
            pl.BlockSpec((1, D_MODEL), lambda i, j: (0, 0)),
        ],
        out_specs=pl.BlockSpec((tm, D_MODEL), lambda i, j: (i, 0)),
        out_shape=jax.ShapeDtypeStruct((rows_out, D_MODEL), F32),
        scratch_shapes=[pltpu.VMEM((HALO_FF + tm, D_MODEL), BF16)],
        compiler_params=_params(("parallel", "arbitrary")),
        name="conv_ffn",
    )(h, h, g, w_up, w_up, dw, dw, w_down, g_final)


CONF_RB = 88


def _mix_ab_kernel(av_ref, ag_ref, sb_ref, sc_ref, sx_ref, hav_ref, hag_ref, hsc_ref, hsx_ref,
                   cw_ref, cb_ref, lg_ref, lb_ref, sw_ref, a_ref, s_ref, xs_ref, sh_ref, ac_ref, cx_ref):
    xs_ref[0:HALO_CONF, :] = hav_ref[...] * _sigmoid(hag_ref[...])
    xs_ref[HALO_CONF:HALO_CONF + TS, :] = av_ref[...] * _sigmoid(ag_ref[...])
    base = HALO_CONF - (CONF_K - 1)

    sub = SUBLANES
    sh_rows = HALO_CONF + TS
    xs_ref[sh_rows:, :] = jnp.zeros((sub, CONF_W), F32)

    def lane_chunk(c, carry):
        cols = pl.ds(pl.multiple_of(c * LANES, LANES), LANES)
        for s in range(sub):
            sh_ref[s] = xs_ref[pl.ds(s, sh_rows), cols]
        for rb in range(TS // CONF_RB):
            r0 = rb * CONF_RB
            acc = jnp.broadcast_to(cb_ref[:, cols], (CONF_RB, LANES))
            for t in range(CONF_K):
                off = base + t
                acc = acc + sh_ref[off % sub, pl.ds(r0 + off - off % sub, CONF_RB), :] * cw_ref[pl.ds(t, 1), cols]
            ac_ref[pl.ds(r0, CONF_RB), cols] = acc
        return carry

    lax.fori_loop(0, CONF_W // LANES, lane_chunk, 0)

    def norm_rows(c, carry):
        rows = pl.ds(pl.multiple_of(c * NORM_CHUNK, 16), NORM_CHUNK)
        a = ac_ref[rows, :]
        mu = jnp.mean(a, axis=-1, keepdims=True)
        d = a - mu
        var = jnp.mean(d * d, axis=-1, keepdims=True)
        y = d * lax.rsqrt(var + LN_EPS) * lg_ref[...] + lb_ref[...]
        a_ref[rows, :] = _silu(y).astype(BF16)
        return carry

    lax.fori_loop(0, TS // NORM_CHUNK, norm_rows, 0)

    cx_ref[0:HALO_SC, :] = hsc_ref[...] * hsx_ref[...]
    cx_ref[HALO_SC:, :] = sc_ref[...] * sx_ref[...]
    cv = cx_ref[pl.ds(HALO_SC - 2, TS), :] * sw_ref[0:1, :]
    cv = cv + cx_ref[pl.ds(HALO_SC - 1, TS), :] * sw_ref[1:2, :]
    cv = cv + cx_ref[pl.ds(HALO_SC, TS), :] * sw_ref[2:3, :]
    s_ref[...] = (sb_ref[...] * cv).astype(BF16)


def mix_ab(z, conf_dw, conf_b, ln_g, ln_b, sc_dw):
    w = CONF_W
    hc, hs = TS // HALO_CONF, TS // HALO_SC

    def col(k):
        return pl.BlockSpec((TS, w), lambda i, k=k: (i, k))

    def halo(rows, per_tile, k):
        return pl.BlockSpec((rows, w), lambda i, k=k: (jnp.maximum(i * per_tile - 1, 0), k))

    def full(r):
        return pl.BlockSpec((r, w), lambda i: (0, 0))

    return pl.pallas_call(
        _mix_ab_kernel,
        grid=(MT // TS,),
        in_specs=[col(0), col(1), col(2), col(3), col(4),
                  halo(HALO_CONF, hc, 0), halo(HALO_CONF, hc, 1),
                  halo(HALO_SC, hs, 3), halo(HALO_SC, hs, 4),
                  full(CONF_K), full(1), full(1), full(1), full(SC_K)],
        out_specs=[pl.BlockSpec((TS, w), lambda i: (i, 0)), pl.BlockSpec((TS, w), lambda i: (i, 0))],
        out_shape=[jax.ShapeDtypeStruct((MT, w), BF16), jax.ShapeDtypeStruct((MT, w), BF16)],
        scratch_shapes=[pltpu.VMEM((HALO_CONF + TS + SUBLANES, w), F32),
                        pltpu.VMEM((SUBLANES, HALO_CONF + TS, LANES), F32),
                        pltpu.VMEM((TS, w), F32), pltpu.VMEM((HALO_SC + TS, w), F32)],
        compiler_params=_params(("parallel",)),
        name="mix_ab",
    )(z, z, z, z, z, z, z, z, z, conf_dw, conf_b, ln_g, ln_b, sc_dw)


def _gla_kernel(q_ref, k_ref, v_ref, go_ref, glr_ref, w2_ref, b_ref, gn_ref, o_ref, st_ref):
    c_len = GLA_CHUNK
    tb = pl.program_id(1)

    @pl.when(tb == 0)
    def _():
        st_ref[...] = jnp.zeros(st_ref.shape, F32)

    ri = lax.broadcasted_iota(jnp.int32, (c_len, c_len), 0)
    ci = lax.broadcasted_iota(jnp.int32, (c_len, c_len), 1)
    tril = ri >= ci
    tril_b = jnp.where(tril, 1.0, 0.0).astype(BF16)
    row = lax.broadcasted_iota(jnp.int32, (c_len, 1), 0)
    nt_dims = (((1,), (1,)), ((), ()))

    def chunk(c, carry):
        rows = pl.ds(pl.multiple_of(c * c_len, c_len), c_len)
        real = tb * GLA_TB + c * c_len + row >= FIRST
        nb, hs = range(GLA_NB), range(GLA_HEADS)
        ch = [(n, h) for n in nb for h in hs]
        lin = [jnp.dot(glr_ref[n, rows, :], w2_ref[...], preferred_element_type=F32) + b_ref[...] for n in nb]
        log_a = [jnp.where(real, -_softplus(-lin[n]) / GLA_GATE_NORM, 0.0) for n in nb]
        cum_all = [_dot_m01_x(tril_b, log_a[n]) for n in nb]
        q_all = [q_ref[n, rows, :] * (GLA_DK ** -0.5) for n in nb]
        k_all = [k_ref[n, rows, :] for n in nb]
        v_all = [v_ref[n, rows, :] for n in nb]
        kc = [slice(h * GLA_DK, (h + 1) * GLA_DK) for h in hs]
        vc = [slice(h * GLA_DV, (h + 1) * GLA_DV) for h in hs]
        cum = [cum_all[n][:, kc[h]] for n, h in ch]
        last = [x[c_len - 1:c_len, :] for x in cum]
        v = [v_all[n][:, vc[h]] for n, h in ch]
        q_dec = [q_all[n][:, kc[h]] * jnp.exp(cum[i]) for i, (n, h) in enumerate(ch)]
        k_dec = [k_all[n][:, kc[h]] * jnp.exp(-cum[i]) for i, (n, h) in enumerate(ch)]
        k_st = [k_all[n][:, kc[h]] * jnp.exp(last[i] - cum[i]) for i, (n, h) in enumerate(ch)]
        scores = [jnp.where(tril, lax.dot_general(q_dec[i], k_dec[i], nt_dims, preferred_element_type=F32), 0.0)
                  for i in range(len(ch))]
        st = [st_ref[n, h] for n, h in ch]
        o = [jnp.dot(scores[i], v[i], preferred_element_type=F32) for i in range(len(ch))]
        o = [o[i] + lax.dot_general(q_dec[i], st[i], nt_dims, preferred_element_type=F32) for i in range(len(ch))]
        upd = [jnp.dot(v[i].T, k_st[i], preferred_element_type=F32) for i in range(len(ch))]
        for i, (n, h) in enumerate(ch):
            st_ref[n, h] = st[i] * jnp.exp(last[i]) + upd[i]
        o = [x * lax.rsqrt(jnp.mean(x * x, axis=-1, keepdims=True) + EPS) for x in o]
        for n in nb:
            go = go_ref[n, rows, :]
            outs = [o[n * GLA_HEADS + h] * gn_ref[:, vc[h]] * _silu(go[:, vc[h]]) for h in hs]
            o_ref[n, rows, :] = jnp.concatenate(outs, axis=-1).astype(BF16)
        return carry

    lax.fori_loop(0, GLA_TB // c_len, chunk, 0)


def gla(z, w2p, b, gn):
    z3 = z.reshape(BATCH, LP, CD_COLS)

    def zcol(width, k):
        return pl.BlockSpec((GLA_NB, GLA_TB, width), lambda b_, t, k=k: (b_, t, k))

    def full(shape):
        return pl.BlockSpec(shape, lambda b_, t: (0, 0))

    return pl.pallas_call(
        _gla_kernel,
        grid=(BATCH // GLA_NB, LP // GLA_TB),
        in_specs=[zcol(GLA_QK, 0), zcol(GLA_QK, 1), zcol(GLA_V, 1), zcol(GLA_V, 2),
                  zcol(GLR_PAD, (GLA_MAIN + RW_COLS) // GLR_PAD),
                  full((GLR_PAD, GLA_QK)), full((1, GLA_QK)), full((1, GLA_V))],
        out_specs=pl.BlockSpec((GLA_NB, GLA_TB, GLA_V), lambda b_, t: (b_, t, 0)),
        out_shape=jax.ShapeDtypeStruct((BATCH, LP, GLA_V), BF16),
        scratch_shapes=[pltpu.VMEM((GLA_NB, GLA_HEADS, GLA_DV, GLA_DK), F32)],
        compiler_params=_params(("parallel", "arbitrary")),
        name="gla",
    )(z3, z3, z3, z3, z3, w2p, b, gn).reshape(MT, GLA_V)


def _head_sum(x, ones_blk):
    n = x.shape[0]
    stacked = jnp.concatenate([x[:, c * LANES:(c + 1) * LANES] for c in range(RW_W // LANES)], axis=0)
    out = _dot_x_m01(stacked, ones_blk)
    return jnp.concatenate([out[c * n:(c + 1) * n, :] for c in range(RW_W // LANES)], axis=-1)


def _head_ones():
    r = lax.broadcasted_iota(jnp.int32, (LANES, LANES), 0) // RW_N
    c = lax.broadcasted_iota(jnp.int32, (LANES, LANES), 1) // RW_N
    return jnp.where(r == c, 1.0, 0.0).astype(BF16)


RW_PAIRS = RW_HEADS // 2
RW_C = 64
RW_NB = 2


def _rw_mixer_kernel(r_ref, k_ref, v_ref, lo_ref, mu_ref, mulo_ref, w0_ref, w2_ref, a0_ref, a2_ref, g2_ref,
                     kkp_ref, kap_ref, lg_ref, lb_ref, rk_ref, o_ref, s_ref, prev_ref):
    c = pl.program_id(1)

    @pl.when(c == 0)
    def _():
        s_ref[...] = jnp.zeros(s_ref.shape, F32)
        prev_ref[...] = jnp.zeros(prev_ref.shape, F32)

    row = lax.broadcasted_iota(jnp.int32, (RW_C, 1), 0)
    ones_blk = _head_ones()
    real = jnp.logical_or(c != 0, row >= FIRST)
    nbs = range(RW_NB)

    def shift_mix(x, n, col0, mu):
        cols = slice(col0, col0 + x.shape[1])
        prev = jnp.where(row == 0, prev_ref[n, 0:1, cols], pltpu.roll(x, 1, 0))
        prev_ref[n, 0:1, cols] = x[RW_C - 1:RW_C, :]
        return x + (prev - x) * mu

    r = [shift_mix(r_ref[n], n, 0, mu_ref[0:1, :]) for n in nbs]
    k = [shift_mix(k_ref[n], n, RW_W, mu_ref[1:2, :]) for n in nbs]
    v = [shift_mix(v_ref[n], n, 2 * RW_W, mu_ref[2:3, :]) for n in nbs]
    lo = [shift_mix(lo_ref[n], n, 3 * RW_W, mulo_ref[...]) for n in nbs]
    w_lin = [jnp.dot(jnp.tanh(lo[n][:, 0:64]), w2_ref[...], preferred_element_type=F32) for n in nbs]
    a_lin = [jnp.dot(lo[n][:, 64:128], a2_ref[...], preferred_element_type=F32) for n in nbs]
    g = [jnp.dot(_sigmoid(lo[n][:, 128:256]), g2_ref[...], preferred_element_type=F32) for n in nbs]
    lw = [-jnp.exp(-_softplus(-(w0_ref[...] + w_lin[n])) - 0.5) for n in nbs]
    a = [_sigmoid(a0_ref[...] + a_lin[n]) for n in nbs]
    k = [jnp.where(real, k[n], 0.0) for n in nbs]
    v = [jnp.where(real, v[n], 0.0) for n in nbs]
    kk = [k[n] * kkp_ref[...] for n in nbs]
    kk = [kk[n] / jnp.maximum(jnp.sqrt(_head_sum(kk[n] * kk[n], ones_blk)), 1e-12) for n in nbs]
    bb = [kk[n] * a[n] for n in nbs]
    k = [k[n] * (1.0 + (a[n] - 1.0) * kap_ref[...]) for n in nbs]

    def wide(xs):
        return jnp.concatenate(xs, axis=1)

    ri = lax.broadcasted_iota(jnp.int32, (RW_C, RW_C), 0)
    ci = lax.broadcasted_iota(jnp.int32, (RW_C, RW_C), 1)
    tril_b = jnp.where(ri >= ci, 1.0, 0.0).astype(BF16)
    lw_w = wide(lw)
    cum = _dot_m01_x(tril_b, lw_w)
    e_in, e_out, e_prev = jnp.exp(cum), jnp.exp(-cum), jnp.exp(cum - lw_w)
    e_last = jnp.exp(cum[RW_C - 1:RW_C, :])
    a_all = -wide(kk) * e_prev
    bt_all = wide(bb) * e_out
    kt_all = wide(k) * e_out
    rt_all = wide(r) * e_in
    bh_all = bt_all * e_last
    kh_all = kt_all * e_last
    v_all = wide(v)

    lane = lax.broadcasted_iota(jnp.int32, (RW_C, LANES), 1)
    t_idx = lax.broadcasted_iota(jnp.int32, (RW_C, LANES), 0)
    first = lane < RW_N
    strict, incl = t_idx > lane % RW_C, t_idx >= lane % RW_C
    eye = jnp.where(t_idx == lane % RW_C, 1.0, 0.0)
    r2 = lax.broadcasted_iota(jnp.int32, (2 * LANES, LANES), 0)
    c2 = lax.broadcasted_iota(jnp.int32, (2 * LANES, LANES), 1)
    same_head = (r2 % LANES) // RW_N == c2 // RW_N
    zeros = jnp.zeros((LANES, LANES), F32)
    nt = (((1,), (1,)), ((), ()))

    def bd(x):
        return jnp.concatenate([jnp.where(first, x, 0.0), jnp.where(first, 0.0, x)], axis=0)

    def mm(x, y):
        return jnp.dot(x.astype(BF16), y.astype(BF16), preferred_element_type=F32)

    def cat(xs, axis):
        return jnp.concatenate(xs, axis=axis)

    prs = range(RW_NB * RW_PAIRS)
    cols = [slice(p * LANES, (p + 1) * LANES) for p in prs]
    a, bt, kt, rt, bh, kh, vv = ([x[:, cs] for cs in cols]
                                 for x in (a_all, bt_all, kt_all, rt_all, bh_all, kh_all, v_all))
    v_bd = [bd(vv[p]) for p in prs]
    pm = [lax.dot_general(cat([a[p], rt[p]], 0).astype(BF16), cat([bd(bt[p]), bd(kt[p])], 0).astype(BF16), nt,
                          preferred_element_type=F32) for p in prs]
    l_ak = [jnp.where(strict, pm[p][0:RW_C, LANES:], 0.0) for p in prs]
    m_rb = [jnp.where(incl, pm[p][RW_C:, 0:LANES], 0.0) for p in prs]
    m_rk = [jnp.where(incl, pm[p][RW_C:, LANES:], 0.0) for p in prs]
    x = [jnp.where(strict, pm[p][0:RW_C, 0:LANES], 0.0) for p in prs]
    t_inv = [eye + x[p] for p in prs]
    x = [mm(x[p], bd(x[p])) for p in prs]
    for _ in range(RW_C.bit_length() - 3):
        z = [mm(x[p], cat([bd(t_inv[p]), bd(x[p])], 1)) for p in prs]
        t_inv = [t_inv[p] + z[p][:, 0:LANES] for p in prs]
        x = [z[p][:, LANES:] for p in prs]
    t_inv = [t_inv[p] + mm(x[p], bd(t_inv[p])) for p in prs]
    lv = [mm(l_ak[p], v_bd[p]) for p in prs]
    w12 = [mm(t_inv[p], cat([bd(a[p]), bd(lv[p])], 1)) for p in prs]
    wv = [cat([w12[p], cat([jnp.zeros((RW_C, LANES), F32), vv[p]], 1)], 0) for p in prs]
    wv_bd = [cat([cat([bd(w12[p][:, 0:LANES]), bd(w12[p][:, LANES:])], 1), cat([zeros, v_bd[p]], 1)], 0)
             for p in prs]
    qy = [mm(cat([m_rb[p], m_rk[p]], 1), wv_bd[p]) for p in prs]
    gh = [jnp.where(same_head, mm(wv[p].T, cat([bh[p], kh[p]], 0)), 0.0) for p in prs]
    s = [s_ref[p // RW_PAIRS, p % RW_PAIRS] for p in prs]
    ys = [lax.dot_general((rt[p] + qy[p][:, 0:LANES]).astype(BF16), s[p].astype(BF16), nt,
                          preferred_element_type=F32) + qy[p][:, LANES:] for p in prs]
    sg = [jnp.dot(cat(_split3(s[p]), 1), cat([gh[p][0:LANES, :].astype(BF16)] * 3, 0), preferred_element_type=F32)
          for p in prs]
    for p in prs:
        s_ref[p // RW_PAIRS, p % RW_PAIRS] = s[p] * e_last[:, cols[p]] + sg[p] + gh[p][LANES:, :]

    for n in nbs:
        yn = jnp.concatenate(ys[n * RW_PAIRS:(n + 1) * RW_PAIRS], axis=1)
        mean = _head_sum(yn, ones_blk) * (1.0 / RW_N)
        d = yn - mean
        var = _head_sum(d * d, ones_blk) * (1.0 / RW_N)
        yn = d * lax.rsqrt(var + RW_GN_EPS) * lg_ref[...] + lb_ref[...]
        bonus = _head_sum(r[n] * k[n] * rk_ref[...], ones_blk) * v[n]
        o_ref[n] = ((yn + bonus) * g[n]).astype(BF16)


def rw_mixer(z, mu, w0, w2, a0, a2, g2, kkp, kap, lg, lb, rk):
    w = RW_W
    c0 = GLA_MAIN // w
    lo0 = (GLA_MAIN + 3 * w) // RW_LORA
    z3 = z.reshape(BATCH, LP, CD_COLS)

    def zcol(width, k):
        return pl.BlockSpec((RW_NB, RW_C, width), lambda b, c, k=k: (b, c, k))

    def full(shape):
        return pl.BlockSpec(shape, lambda b, c: (0, 0))

    return pl.pallas_call(
        _rw_mixer_kernel,
        grid=(BATCH // RW_NB, LP // RW_C),
        in_specs=[zcol(w, c0), zcol(w, c0 + 1), zcol(w, c0 + 2), zcol(RW_LORA, lo0),
                  full((3, w)), full((1, RW_LORA)),
                  full((1, w)), full((64, w)), full((1, w)), full((64, w)), full((128, w)),
                  full((1, w)), full((1, w)), full((1, w)), full((1, w)), full((1, w))],
        out_specs=pl.BlockSpec((RW_NB, RW_C, w), lambda b, c: (b, c, 0)),
        out_shape=jax.ShapeDtypeStruct((BATCH, LP, w), BF16),
        scratch_shapes=[pltpu.VMEM((RW_NB, RW_PAIRS, LANES, LANES), F32),
                        pltpu.VMEM((RW_NB, SUBLANES, RW_COLS), F32)],
        compiler_params=_params(("parallel", "arbitrary")),
        name="rw_mixer",
    )(z3, z3, z3, z3, mu[:, :3 * w].reshape(3, w), mu[:, 3 * w:], w0, w2, a0, a2, g2, kkp, kap, lg, lb, rk
      ).reshape(MT, w)


CD_WROWS = 256


def _cd_weight_kernel(wt_ref, o_ref):
    pad = jnp.zeros((GLR_PAD - GLA_RANK, CD_WROWS), F32)
    perm = jnp.concatenate([wt_ref[0:GLA_MAIN, :], wt_ref[GLA_COLS:GLA_COLS + RW_COLS, :],
                            wt_ref[GLA_MAIN:GLA_COLS, :], pad], axis=0)
    o_ref[...] = perm.T.astype(BF16)


def cd_weight(w, layer):
    wt = jnp.swapaxes(w, 1, 2)
    cols = wt.shape[1]
    return pl.pallas_call(
        _cd_weight_kernel,
        grid=(D_MODEL // CD_WROWS,),
        in_specs=[pl.BlockSpec((None, cols, CD_WROWS), lambda i: (layer, 0, i))],
        out_specs=pl.BlockSpec((CD_WROWS, CD_COLS), lambda i: (i, 0)),
        out_shape=jax.ShapeDtypeStruct((D_MODEL, CD_COLS), BF16),
        compiler_params=_params(("parallel",)),
        name="cd_weight",
    )(wt)


def kernel(x, meta, ab_w_in, ab_conf_dw, ab_conf_dw_b, ab_conf_ln_g, ab_conf_ln_b, ab_sc_dw, ab_w_out,
           cd_w_in, cd_gla_w2, cd_gla_b, cd_gla_norm_g, cd_rw_mu, cd_rw_w0, cd_rw_w2, cd_rw_a0, cd_rw_a2,
           cd_rw_g2, cd_rw_kk, cd_rw_ka, cd_rw_rk, cd_rw_ln_g, cd_rw_ln_b, cd_w_out, norm_mix, norm_ffn,
           ffn_w_up, ffn_dw, ffn_w_down, norm_final):
    bs = x.shape[0]
    h = jnp.concatenate([jnp.zeros((bs, PAD, D_MODEL), x.dtype),
                         jnp.broadcast_to(meta[None].astype(x.dtype), (bs, N_META, D_MODEL)), x], axis=1)
    h = h.reshape(MT, D_MODEL)
    row2 = lambda p: p.reshape(1, -1)
    g_final = row2(norm_final)

    z = norm_matmul(h, row2(norm_mix[0]), ab_w_in[0].astype(BF16), TN_AB)
    a_act, s_act = mix_ab(z, ab_conf_dw[0], row2(ab_conf_dw_b[0]), row2(ab_conf_ln_g[0]),
                          row2(ab_conf_ln_b[0]), ab_sc_dw[0])
    h = out_proj(h, a_act, s_act, ab_w_out[0].astype(BF16))
    w_up, w_down = ffn_w_up.astype(BF16), ffn_w_down.astype(BF16)
    h = conv_ffn(h, row2(norm_ffn[0]), w_up, ffn_dw, w_down, g_final, 0, False)

    z = norm_matmul(h, row2(norm_mix[1]), cd_weight(cd_w_in, 0), TN_CD)
    w2p = jnp.concatenate([cd_gla_w2[0], jnp.zeros((GLR_PAD - GLA_RANK, GLA_QK), F32)], axis=0)
    o_act = gla(z, w2p, row2(cd_gla_b[0]), row2(cd_gla_norm_g[0]))
    y_act = rw_mixer(z, row2(cd_rw_mu[0]), row2(cd_rw_w0[0]), cd_rw_w2[0], row2(cd_rw_a0[0]), cd_rw_a2[0],
                     cd_rw_g2[0], row2(cd_rw_kk[0]), row2(cd_rw_ka[0]), row2(cd_rw_ln_g[0]), row2(cd_rw_ln_b[0]),
                     row2(cd_rw_rk[0]))
    h = out_proj(h, o_act, y_act, cd_w_out[0].astype(BF16))
    h = conv_ffn(h, row2(norm_ffn[1]), w_up, ffn_dw, w_down, g_final, 1, True)
    return h.reshape(bs, SEQ, D_MODEL)
```

```python
import functools

import jax
import jax.numpy as jnp
from jax import lax
from jax.experimental import pallas as pl
from jax.experimental.pallas import tpu as pltpu

D_MODEL = 2048
BATCH = 4
SEQ = 2048
N_META = 16
CONF_W = 1024
CONF_K = 31
SC_K = 3
GLA_HEADS = 4
GLA_DK = 128
GLA_DV = 256
GLA_QK = GLA_HEADS * GLA_DK
GLA_V = GLA_HEADS * GLA_DV
GLA_RANK = 16
GLA_GATE_NORM = 16.0
GLA_CHUNK = 64
GLA_COLS = 2 * GLA_QK + 2 * GLA_V + GLA_RANK
RW_HEADS = 16
RW_N = 64
RW_W = RW_HEADS * RW_N
RW_DECAY_RANK = 64
RW_A_RANK = 64
RW_G_RANK = 128
RW_LORA = RW_DECAY_RANK + RW_A_RANK + RW_G_RANK
RW_COLS = 3 * RW_W + RW_LORA
RW_GN_EPS = 64e-5
D_FF = 5632
EPS = 1e-6
LN_EPS = 1e-5

PAD = (-N_META) % GLA_CHUNK
LP = PAD + N_META + SEQ
MT = BATCH * LP
FIRST = PAD
TM = LP // 3
TS = LP // 6
LANES = 128
SUBLANES = 8
GLR_PAD = LANES
GLA_MAIN = 2 * GLA_QK + 2 * GLA_V
CD_COLS = GLA_MAIN + RW_COLS + GLR_PAD
TN_AB = 1280
TN_CD = 2176
TN_FF = 512
TM_LAST = 512
FF_LAST_CHUNK = 128
HALO_FF = 16
HALO_CONF = 32
HALO_SC = 8
GLA_TB = TM
GLA_NB = 2
VMEM_LIMIT = 56 * 1024 * 1024

F32 = jnp.float32
BF16 = jnp.bfloat16


def _params(sem):
    return pltpu.CompilerParams(dimension_semantics=sem, vmem_limit_bytes=VMEM_LIMIT)


def _sigmoid(x):
    return 1.0 / (1.0 + jnp.exp(-x))


def _silu(x):
    return x * _sigmoid(x)


def _softplus(x):
    return jnp.maximum(x, 0.0) + jnp.log(1.0 + jnp.exp(-jnp.abs(x)))


def _split3(x):
    hi = x.astype(BF16)
    r = x - hi.astype(F32)
    mid = r.astype(BF16)
    lo = (r - mid.astype(F32)).astype(BF16)
    return hi, mid, lo


def _dot_x_m01(x, m01):
    return jnp.dot(jnp.concatenate(_split3(x), axis=1), jnp.concatenate([m01] * 3, axis=0),
                   preferred_element_type=F32)


def _dot_m01_x(m01, x):
    return jnp.dot(jnp.concatenate([m01] * 3, axis=1), jnp.concatenate(_split3(x), axis=0),
                   preferred_element_type=F32)


def _rms_rows(x, g):
    ms = jnp.mean(x * x, axis=-1, keepdims=True)
    return x * lax.rsqrt(ms + EPS) * g


NORM_CHUNK = 176


def _norm_rows_into(dst_ref, dst_row0, src_ref, g_ref, chunk=NORM_CHUNK):
    assert src_ref.shape[0] % chunk == 0 and chunk % 16 == 0

    def body(c, carry):
        r = pl.multiple_of(c * chunk, 16)
        x = src_ref[pl.ds(r, chunk), :]
        dst_ref[pl.ds(pl.multiple_of(dst_row0 + r, 16), chunk), :] = _rms_rows(x, g_ref[...]).astype(BF16)
        return carry

    lax.fori_loop(0, src_ref.shape[0] // chunk, body, 0)


def _norm_matmul_kernel(h_ref, g_ref, w_ref, o_ref, hn_ref):
    @pl.when(pl.program_id(1) == 0)
    def _():
        _norm_rows_into(hn_ref, 0, h_ref, g_ref)

    o_ref[...] = jnp.dot(hn_ref[...], w_ref[...], preferred_element_type=F32)


def norm_matmul(h, g, w, tn):
    n = w.shape[1]
    return pl.pallas_call(
        _norm_matmul_kernel,
        grid=(MT // TM, n // tn),
        in_specs=[
            pl.BlockSpec((TM, D_MODEL), lambda i, j: (i, 0)),
            pl.BlockSpec((1, D_MODEL), lambda i, j: (0, 0)),
            pl.BlockSpec((D_MODEL, tn), lambda i, j: (0, j)),
        ],
        out_specs=pl.BlockSpec((TM, tn), lambda i, j: (i, j)),
        out_shape=jax.ShapeDtypeStruct((MT, n), F32),
        scratch_shapes=[pltpu.VMEM((TM, D_MODEL), BF16)],
        compiler_params=_params(("parallel", "arbitrary")),
        name="norm_matmul",
    )(h, g, w)


def _zero_pad_rows(o_ref, tiles_per_batch):
    @pl.when(pl.program_id(0) % tiles_per_batch == 0)
    def _():
        o_ref[0:FIRST, :] = jnp.zeros((FIRST, o_ref.shape[1]), o_ref.dtype)


def _out_proj_kernel(h_ref, a1_ref, a2_ref, w1_ref, w2_ref, o_ref):
    acc = jnp.dot(a1_ref[...], w1_ref[...], preferred_element_type=F32)
    acc = acc + jnp.dot(a2_ref[...], w2_ref[...], preferred_element_type=F32)
    o_ref[...] = h_ref[...] + acc
    _zero_pad_rows(o_ref, LP // TS)


def out_proj(h, a1, a2, w):
    k1, k2 = a1.shape[1], a2.shape[1]
    assert k1 == k2 and w.shape[0] == k1 + k2
    w1 = w2 = w
    return pl.pallas_call(
        _out_proj_kernel,
        grid=(MT // TS,),
        in_specs=[
            pl.BlockSpec((TS, D_MODEL), lambda i: (i, 0)),
            pl.BlockSpec((TS, k1), lambda i: (i, 0)),
            pl.BlockSpec((TS, k2), lambda i: (i, 0)),
            pl.BlockSpec((k1, D_MODEL), lambda i: (0, 0)),
            pl.BlockSpec((k2, D_MODEL), lambda i: (1, 0)),
        ],
        out_specs=pl.BlockSpec((TS, D_MODEL), lambda i: (i, 0)),
        out_shape=jax.ShapeDtypeStruct((MT, D_MODEL), F32),
        compiler_params=_params(("parallel",)),
        name="out_proj",
    )(h, a1, a2, w1, w2)


def _ffn_kernel(h_ref, halo_ref, g_ref, wv_ref, wg_ref, dwv_ref, dwg_ref, wd_ref, gf_ref,
                o_ref, hn_ref, *, last_layer):
    j = pl.program_id(1)
    tm = h_ref.shape[0]
    chunk = FF_LAST_CHUNK if last_layer else NORM_CHUNK

    @pl.when(j == 0)
    def _():
        hn_ref[0:HALO_FF, :] = _rms_rows(halo_ref[...], g_ref[...]).astype(BF16)
        _norm_rows_into(hn_ref, HALO_FF, h_ref, g_ref, chunk)
        o_ref[...] = h_ref[...]

    hn = hn_ref[...]

    def conv(w_ref, dw_ref):
        u = jnp.dot(hn, w_ref[...], preferred_element_type=F32)
        u1 = pltpu.roll(u, 1, 0)
        u2 = pltpu.roll(u, 2, 0)
        c = u * dw_ref[2:3, :] + u1 * dw_ref[1:2, :] + u2 * dw_ref[0:1, :]
        return c[HALO_FF:, :]

    act = (_silu(conv(wg_ref, dwg_ref)) * conv(wv_ref, dwv_ref)).astype(BF16)
    for n in range(D_MODEL // TN_FF):
        cols = slice(n * TN_FF, (n + 1) * TN_FF)
        o_ref[:, cols] += jnp.dot(act, wd_ref[:, cols], preferred_element_type=F32)

    @pl.when(j == pl.num_programs(1) - 1)
    def _():
        if last_layer:
            def body(c, carry):
                rows = pl.ds(pl.multiple_of(c * chunk, 16), chunk)
                o_ref[rows, :] = _rms_rows(o_ref[rows, :], gf_ref[...])
                return carry

            lax.fori_loop(0, tm // chunk, body, 0)
        else:
            _zero_pad_rows(o_ref, LP // TM)


def conv_ffn(h, g, w_up, dw, w_down, g_final, layer, last_layer):
    nj = D_FF // TN_FF
    if last_layer:
        tm, rows_out = TM_LAST, BATCH * SEQ
        per_batch = SEQ // tm

        def row0(i):
            return pl.multiple_of((i // per_batch) * LP + (PAD + N_META) + (i % per_batch) * tm, HALO_FF)

        h_spec = pl.BlockSpec((pl.Element(tm), pl.Element(D_MODEL)), lambda i, j: (row0(i), 0))
        halo_spec = pl.BlockSpec((pl.Element(HALO_FF), pl.Element(D_MODEL)),
                                 lambda i, j: (pl.multiple_of(row0(i) - HALO_FF, HALO_FF), 0))
    else:
        tm, rows_out = TM, MT
        halo_blocks = tm // HALO_FF
        h_spec = pl.BlockSpec((tm, D_MODEL), lambda i, j: (i, 0))
        halo_spec = pl.BlockSpec((HALO_FF, D_MODEL), lambda i, j: (jnp.maximum(i * halo_blocks - 1, 0), 0))
    return pl.pallas_call(
        functools.partial(_ffn_kernel, last_layer=last_layer),
        grid=(rows_out // tm, nj),
        in_specs=[
            h_spec,
            halo_spec,
            pl.BlockSpec((1, D_MODEL), lambda i, j: (0, 0)),
            pl.BlockSpec((None, D_MODEL, TN_FF), lambda i, j: (layer, 0, j)),
            pl.BlockSpec((None, D_MODEL, TN_FF), lambda i, j: (layer, 0, nj + j)),
            pl.BlockSpec((None, 3, TN_FF), lambda i, j: (layer, 0, j)),
            pl.BlockSpec((None, 3, TN_FF), lambda i, j: (layer, 0, nj + j)),
            pl.BlockSpec((None, TN_FF, D_MODEL), lambda i, j: (layer, j, 0)),
            pl.BlockSpec((1, D_MODEL), lambda i, j: (0, 0)),
        ],
        out_specs=pl.BlockSpec((tm, D_MODEL), lambda i, j: (i, 0)),
        out_shape=jax.ShapeDtypeStruct((rows_out, D_MODEL), F32),
        scratch_shapes=[pltpu.VMEM((HALO_FF + tm, D_MODEL), BF16)],
        compiler_params=_params(("parallel", "arbitrary")),
        name="conv_ffn",
    )(h, h, g, w_up, w_up, dw, dw, w_down, g_final)


CONF_RB = 88


def _mix_ab_kernel(av_ref, ag_ref, sb_ref, sc_ref, sx_ref, hav_ref, hag_ref, hsc_ref, hsx_ref,
                   cw_ref, cb_ref, lg_ref, lb_ref, sw_ref, a_ref, s_ref, xs_ref, sh_ref, ac_ref, cx_ref):
    xs_ref[0:HALO_CONF, :] = hav_ref[...] * _sigmoid(hag_ref[...])
    xs_ref[HALO_CONF:HALO_CONF + TS, :] = av_ref[...] * _sigmoid(ag_ref[...])
    base = HALO_CONF - (CONF_K - 1)

    sub = SUBLANES
    sh_rows = HALO_CONF + TS
    xs_ref[sh_rows:, :] = jnp.zeros((sub, CONF_W), F32)

    def lane_chunk(c, carry):
        cols = pl.ds(pl.multiple_of(c * LANES, LANES), LANES)
        for s in range(sub):
            sh_ref[s] = xs_ref[pl.ds(s, sh_rows), cols]
        for rb in range(TS // CONF_RB):
            r0 = rb * CONF_RB
            acc = jnp.broadcast_to(cb_ref[:, cols], (CONF_RB, LANES))
            for t in range(CONF_K):
                off = base + t
                acc = acc + sh_ref[off % sub, pl.ds(r0 + off - off % sub, CONF_RB), :] * cw_ref[pl.ds(t, 1), cols]
            ac_ref[pl.ds(r0, CONF_RB), cols] = acc
        return carry

    lax.fori_loop(0, CONF_W // LANES, lane_chunk, 0)

    def norm_rows(c, carry):
        rows = pl.ds(pl.multiple_of(c * NORM_CHUNK, 16), NORM_CHUNK)
        a = ac_ref[rows, :]
        mu = jnp.mean(a, axis=-1, keepdims=True)
        d = a - mu
        var = jnp.mean(d * d, axis=-1, keepdims=True)
        y = d * lax.rsqrt(var + LN_EPS) * lg_ref[...] + lb_ref[...]
        a_ref[rows, :] = _silu(y).astype(BF16)
        return carry

    lax.fori_loop(0, TS // NORM_CHUNK, norm_rows, 0)

    cx_ref[0:HALO_SC, :] = hsc_ref[...] * hsx_ref[...]
    cx_ref[HALO_SC:, :] = sc_ref[...] * sx_ref[...]
    cv = cx_ref[pl.ds(HALO_SC - 2, TS), :] * sw_ref[0:1, :]
    cv = cv + cx_ref[pl.ds(HALO_SC - 1, TS), :] * sw_ref[1:2, :]
    cv = cv + cx_ref[pl.ds(HALO_SC, TS), :] * sw_ref[2:3, :]
    s_ref[...] = (sb_ref[...] * cv).astype(BF16)


def mix_ab(z, conf_dw, conf_b, ln_g, ln_b, sc_dw):
    w = CONF_W
    hc, hs = TS // HALO_CONF, TS // HALO_SC

    def col(k):
        return pl.BlockSpec((TS, w), lambda i, k=k: (i, k))

    def halo(rows, per_tile, k):
        return pl.BlockSpec((rows, w), lambda i, k=k: (jnp.maximum(i * per_tile - 1, 0), k))

    def full(r):
        return pl.BlockSpec((r, w), lambda i: (0, 0))

    return pl.pallas_call(
        _mix_ab_kernel,
        grid=(MT // TS,),
        in_specs=[col(0), col(1), col(2), col(3), col(4),
                  halo(HALO_CONF, hc, 0), halo(HALO_CONF, hc, 1),
                  halo(HALO_SC, hs, 3), halo(HALO_SC, hs, 4),
                  full(CONF_K), full(1), full(1), full(1), full(SC_K)],
        out_specs=[pl.BlockSpec((TS, w), lambda i: (i, 0)), pl.BlockSpec((TS, w), lambda i: (i, 0))],
        out_shape=[jax.ShapeDtypeStruct((MT, w), BF16), jax.ShapeDtypeStruct((MT, w), BF16)],
        scratch_shapes=[pltpu.VMEM((HALO_CONF + TS + SUBLANES, w), F32),
                        pltpu.VMEM((SUBLANES, HALO_CONF + TS, LANES), F32),
                        pltpu.VMEM((TS, w), F32), pltpu.VMEM((HALO_SC + TS, w), F32)],
        compiler_params=_params(("parallel",)),
        name="mix_ab",
    )(z, z, z, z, z, z, z, z, z, conf_dw, conf_b, ln_g, ln_b, sc_dw)


def _gla_kernel(q_ref, k_ref, v_ref, go_ref, glr_ref, w2_ref, b_ref, gn_ref, o_ref, st_ref):
    c_len = GLA_CHUNK
    tb = pl.program_id(1)

    @pl.when(tb == 0)
    def _():
        st_ref[...] = jnp.zeros(st_ref.shape, F32)

    ri = lax.broadcasted_iota(jnp.int32, (c_len, c_len), 0)
    ci = lax.broadcasted_iota(jnp.int32, (c_len, c_len), 1)
    tril = ri >= ci
    tril_b = jnp.where(tril, 1.0, 0.0).astype(BF16)
    row = lax.broadcasted_iota(jnp.int32, (c_len, 1), 0)
    nt_dims = (((1,), (1,)), ((), ()))

    def chunk(c, carry):
        rows = pl.ds(pl.multiple_of(c * c_len, c_len), c_len)
        real = tb * GLA_TB + c * c_len + row >= FIRST
        nb, hs = range(GLA_NB), range(GLA_HEADS)
        ch = [(n, h) for n in nb for h in hs]
        lin = [jnp.dot(glr_ref[n, rows, :], w2_ref[...], preferred_element_type=F32) + b_ref[...] for n in nb]
        log_a = [jnp.where(real, -_softplus(-lin[n]) / GLA_GATE_NORM, 0.0) for n in nb]
        cum_all = [_dot_m01_x(tril_b, log_a[n]) for n in nb]
        q_all = [q_ref[n, rows, :] * (GLA_DK ** -0.5) for n in nb]
        k_all = [k_ref[n, rows, :] for n in nb]
        v_all = [v_ref[n, rows, :] for n in nb]
        kc = [slice(h * GLA_DK, (h + 1) * GLA_DK) for h in hs]
        vc = [slice(h * GLA_DV, (h + 1) * GLA_DV) for h in hs]
        cum = [cum_all[n][:, kc[h]] for n, h in ch]
        last = [x[c_len - 1:c_len, :] for x in cum]
        v = [v_all[n][:, vc[h]] for n, h in ch]
        q_dec = [q_all[n][:, kc[h]] * jnp.exp(cum[i]) for i, (n, h) in enumerate(ch)]
        k_dec = [k_all[n][:, kc[h]] * jnp.exp(-cum[i]) for i, (n, h) in enumerate(ch)]
        k_st = [k_all[n][:, kc[h]] * jnp.exp(last[i] - cum[i]) for i, (n, h) in enumerate(ch)]
        scores = [jnp.where(tril, lax.dot_general(q_dec[i], k_dec[i], nt_dims, preferred_element_type=F32), 0.0)
                  for i in range(len(ch))]
        st = [st_ref[n, h] for n, h in ch]
        o = [jnp.dot(scores[i], v[i], preferred_element_type=F32) for i in range(len(ch))]
        o = [o[i] + lax.dot_general(q_dec[i], st[i], nt_dims, preferred_element_type=F32) for i in range(len(ch))]
        upd = [jnp.dot(v[i].T, k_st[i], preferred_element_type=F32) for i in range(len(ch))]
        for i, (n, h) in enumerate(ch):
            st_ref[n, h] = st[i] * jnp.exp(last[i]) + upd[i]
        o = [x * lax.rsqrt(jnp.mean(x * x, axis=-1, keepdims=True) + EPS) for x in o]
        for n in nb:
            go = go_ref[n, rows, :]
            outs = [o[n * GLA_HEADS + h] * gn_ref[:, vc[h]] * _silu(go[:, vc[h]]) for h in hs]
            o_ref[n, rows, :] = jnp.concatenate(outs, axis=-1).astype(BF16)
        return carry

    lax.fori_loop(0, GLA_TB // c_len, chunk, 0)


def gla(z, w2p, b, gn):
    z3 = z.reshape(BATCH, LP, CD_COLS)

    def zcol(width, k):
        return pl.BlockSpec((GLA_NB, GLA_TB, width), lambda b_, t, k=k: (b_, t, k))

    def full(shape):
        return pl.BlockSpec(shape, lambda b_, t: (0, 0))

    return pl.pallas_call(
        _gla_kernel,
        grid=(BATCH // GLA_NB, LP // GLA_TB),
        in_specs=[zcol(GLA_QK, 0), zcol(GLA_QK, 1), zcol(GLA_V, 1), zcol(GLA_V, 2),
                  zcol(GLR_PAD, (GLA_MAIN + RW_COLS) // GLR_PAD),
                  full((GLR_PAD, GLA_QK)), full((1, GLA_QK)), full((1, GLA_V))],
        out_specs=pl.BlockSpec((GLA_NB, GLA_TB, GLA_V), lambda b_, t: (b_, t, 0)),
        out_shape=jax.ShapeDtypeStruct((BATCH, LP, GLA_V), BF16),
        scratch_shapes=[pltpu.VMEM((GLA_NB, GLA_HEADS, GLA_DV, GLA_DK), F32)],
        compiler_params=_params(("parallel", "arbitrary")),
        name="gla",
    )(z3, z3, z3, z3, z3, w2p, b, gn).reshape(MT, GLA_V)


def _head_sum(x, ones_blk):
    n = x.shape[0]
    stacked = jnp.concatenate([x[:, c * LANES:(c + 1) * LANES] for c in range(RW_W // LANES)], axis=0)
    out = _dot_x_m01(stacked, ones_blk)
    return jnp.concatenate([out[c * n:(c + 1) * n, :] for c in range(RW_W // LANES)], axis=-1)


def _head_ones():
    r = lax.broadcasted_iota(jnp.int32, (LANES, LANES), 0) // RW_N
    c = lax.broadcasted_iota(jnp.int32, (LANES, LANES), 1) // RW_N
    return jnp.where(r == c, 1.0, 0.0).astype(BF16)


RW_PAIRS = RW_HEADS // 2
RW_C = 64
RW_NB = 2


def _rw_mixer_kernel(r_ref, k_ref, v_ref, lo_ref, mu_ref, mulo_ref, w0_ref, w2_ref, a0_ref, a2_ref, g2_ref,
                     kkp_ref, kap_ref, lg_ref, lb_ref, rk_ref, o_ref, s_ref, prev_ref):
    c = pl.program_id(1)

    @pl.when(c == 0)
    def _():
        s_ref[...] = jnp.zeros(s_ref.shape, F32)
        prev_ref[...] = jnp.zeros(prev_ref.shape, F32)

    row = lax.broadcasted_iota(jnp.int32, (RW_C, 1), 0)
    ones_blk = _head_ones()
    real = jnp.logical_or(c != 0, row >= FIRST)
    nbs = range(RW_NB)

    def shift_mix(x, n, col0, mu):
        cols = slice(col0, col0 + x.shape[1])
        prev = jnp.where(row == 0, prev_ref[n, 0:1, cols], pltpu.roll(x, 1, 0))
        prev_ref[n, 0:1, cols] = x[RW_C - 1:RW_C, :]
        return x + (prev - x) * mu

    r = [shift_mix(r_ref[n], n, 0, mu_ref[0:1, :]) for n in nbs]
    k = [shift_mix(k_ref[n], n, RW_W, mu_ref[1:2, :]) for n in nbs]
    v = [shift_mix(v_ref[n], n, 2 * RW_W, mu_ref[2:3, :]) for n in nbs]
    lo = [shift_mix(lo_ref[n], n, 3 * RW_W, mulo_ref[...]) for n in nbs]
    xa0, xg0 = RW_DECAY_RANK, RW_DECAY_RANK + RW_A_RANK
    w_lin = [jnp.dot(jnp.tanh(lo[n][:, 0:xa0]), w2_ref[...], preferred_element_type=F32) for n in nbs]
    a_lin = [jnp.dot(lo[n][:, xa0:xg0], a2_ref[...], preferred_element_type=F32) for n in nbs]
    g = [jnp.dot(_sigmoid(lo[n][:, xg0:]), g2_ref[...], preferred_element_type=F32) for n in nbs]
    lw = [-jnp.exp(-_softplus(-(w0_ref[...] + w_lin[n])) - 0.5) for n in nbs]
    a = [_sigmoid(a0_ref[...] + a_lin[n]) for n in nbs]
    k = [jnp.where(real, k[n], 0.0) for n in nbs]
    v = [jnp.where(real, v[n], 0.0) for n in nbs]
    kk = [k[n] * kkp_ref[...] for n in nbs]
    kk = [kk[n] / jnp.maximum(jnp.sqrt(_head_sum(kk[n] * kk[n], ones_blk)), 1e-12) for n in nbs]
    bb = [kk[n] * a[n] for n in nbs]
    k = [k[n] * (1.0 + (a[n] - 1.0) * kap_ref[...]) for n in nbs]

    def wide(xs):
        return jnp.concatenate(xs, axis=1)

    ri = lax.broadcasted_iota(jnp.int32, (RW_C, RW_C), 0)
    ci = lax.broadcasted_iota(jnp.int32, (RW_C, RW_C), 1)
    tril_b = jnp.where(ri >= ci, 1.0, 0.0).astype(BF16)
    lw_w = wide(lw)
    cum = _dot_m01_x(tril_b, lw_w)
    e_in, e_out, e_prev = jnp.exp(cum), jnp.exp(-cum), jnp.exp(cum - lw_w)
    e_last = jnp.exp(cum[RW_C - 1:RW_C, :])
    a_all = -wide(kk) * e_prev
    bt_all = wide(bb) * e_out
    kt_all = wide(k) * e_out
    rt_all = wide(r) * e_in
    bh_all = bt_all * e_last
    kh_all = kt_all * e_last
    v_all = wide(v)

    lane = lax.broadcasted_iota(jnp.int32, (RW_C, LANES), 1)
    t_idx = lax.broadcasted_iota(jnp.int32, (RW_C, LANES), 0)
    first = lane < RW_N
    strict, incl = t_idx > lane % RW_C, t_idx >= lane % RW_C
    eye = jnp.where(t_idx == lane % RW_C, 1.0, 0.0)
    r2 = lax.broadcasted_iota(jnp.int32, (2 * LANES, LANES), 0)
    c2 = lax.broadcasted_iota(jnp.int32, (2 * LANES, LANES), 1)
    same_head = (r2 % LANES) // RW_N == c2 // RW_N
    zeros = jnp.zeros((LANES, LANES), F32)
    nt = (((1,), (1,)), ((), ()))

    def bd(x):
        return jnp.concatenate([jnp.where(first, x, 0.0), jnp.where(first, 0.0, x)], axis=0)

    def mm(x, y):
        return jnp.dot(x.astype(BF16), y.astype(BF16), preferred_element_type=F32)

    def cat(xs, axis):
        return jnp.concatenate(xs, axis=axis)

    prs = range(RW_NB * RW_PAIRS)
    cols = [slice(p * LANES, (p + 1) * LANES) for p in prs]
    a, bt, kt, rt, bh, kh, vv = ([x[:, cs] for cs in cols]
                                 for x in (a_all, bt_all, kt_all, rt_all, bh_all, kh_all, v_all))
    v_bd = [bd(vv[p]) for p in prs]
    pm = [lax.dot_general(cat([a[p], rt[p]], 0).astype(BF16), cat([bd(bt[p]), bd(kt[p])], 0).astype(BF16), nt,
                          preferred_element_type=F32) for p in prs]
    l_ak = [jnp.where(strict, pm[p][0:RW_C, LANES:], 0.0) for p in prs]
    m_rb = [jnp.where(incl, pm[p][RW_C:, 0:LANES], 0.0) for p in prs]
    m_rk = [jnp.where(incl, pm[p][RW_C:, LANES:], 0.0) for p in prs]
    x = [jnp.where(strict, pm[p][0:RW_C, 0:LANES], 0.0) for p in prs]
    t_inv = [eye + x[p] for p in prs]
    x = [mm(x[p], bd(x[p])) for p in prs]
    for _ in range(RW_C.bit_length() - 3):
        z = [mm(x[p], cat([bd(t_inv[p]), bd(x[p])], 1)) for p in prs]
        t_inv = [t_inv[p] + z[p][:, 0:LANES] for p in prs]
        x = [z[p][:, LANES:] for p in prs]
    t_inv = [t_inv[p] + mm(x[p], bd(t_inv[p])) for p in prs]
    lv = [mm(l_ak[p], v_bd[p]) for p in prs]
    w12 = [mm(t_inv[p], cat([bd(a[p]), bd(lv[p])], 1)) for p in prs]
    wv = [cat([w12[p], cat([jnp.zeros((RW_C, LANES), F32), vv[p]], 1)], 0) for p in prs]
    wv_bd = [cat([cat([bd(w12[p][:, 0:LANES]), bd(w12[p][:, LANES:])], 1), cat([zeros, v_bd[p]], 1)], 0)
             for p in prs]
    qy = [mm(cat([m_rb[p], m_rk[p]], 1), wv_bd[p]) for p in prs]
    gh = [jnp.where(same_head, mm(wv[p].T, cat([bh[p], kh[p]], 0)), 0.0) for p in prs]
    s = [s_ref[p // RW_PAIRS, p % RW_PAIRS] for p in prs]
    ys = [lax.dot_general((rt[p] + qy[p][:, 0:LANES]).astype(BF16), s[p].astype(BF16), nt,
                          preferred_element_type=F32) + qy[p][:, LANES:] for p in prs]
    sg = [jnp.dot(cat(_split3(s[p]), 1), cat([gh[p][0:LANES, :].astype(BF16)] * 3, 0), preferred_element_type=F32)
          for p in prs]
    for p in prs:
        s_ref[p // RW_PAIRS, p % RW_PAIRS] = s[p] * e_last[:, cols[p]] + sg[p] + gh[p][LANES:, :]

    for n in nbs:
        yn = jnp.concatenate(ys[n * RW_PAIRS:(n + 1) * RW_PAIRS], axis=1)
        mean = _head_sum(yn, ones_blk) * (1.0 / RW_N)
        d = yn - mean
        var = _head_sum(d * d, ones_blk) * (1.0 / RW_N)
        yn = d * lax.rsqrt(var + RW_GN_EPS) * lg_ref[...] + lb_ref[...]
        bonus = _head_sum(r[n] * k[n] * rk_ref[...], ones_blk) * v[n]
        o_ref[n] = ((yn + bonus) * g[n]).astype(BF16)


def rw_mixer(z, mu, w0, w2, a0, a2, g2, kkp, kap, lg, lb, rk):
    w = RW_W
    c0 = GLA_MAIN // w
    lo0 = (GLA_MAIN + 3 * w) // RW_LORA
    z3 = z.reshape(BATCH, LP, CD_COLS)

    def zcol(width, k):
        return pl.BlockSpec((RW_NB, RW_C, width), lambda b, c, k=k: (b, c, k))

    def full(shape):
        return pl.BlockSpec(shape, lambda b, c: (0, 0))

    return pl.pallas_call(
        _rw_mixer_kernel,
        grid=(BATCH // RW_NB, LP // RW_C),
        in_specs=[zcol(w, c0), zcol(w, c0 + 1), zcol(w, c0 + 2), zcol(RW_LORA, lo0),
                  full((3, w)), full((1, RW_LORA)),
                  full((1, w)), full((RW_DECAY_RANK, w)), full((1, w)), full((RW_A_RANK, w)), full((RW_G_RANK, w)),
                  full((1, w)), full((1, w)), full((1, w)), full((1, w)), full((1, w))],
        out_specs=pl.BlockSpec((RW_NB, RW_C, w), lambda b, c: (b, c, 0)),
        out_shape=jax.ShapeDtypeStruct((BATCH, LP, w), BF16),
        scratch_shapes=[pltpu.VMEM((RW_NB, RW_PAIRS, LANES, LANES), F32),
                        pltpu.VMEM((RW_NB, SUBLANES, RW_COLS), F32)],
        compiler_params=_params(("parallel", "arbitrary")),
        name="rw_mixer",
    )(z3, z3, z3, z3, mu[:, :3 * w].reshape(3, w), mu[:, 3 * w:], w0, w2, a0, a2, g2, kkp, kap, lg, lb, rk
      ).reshape(MT, w)


CD_WROWS = 256


def _cd_weight_kernel(wt_ref, o_ref):
    pad = jnp.zeros((GLR_PAD - GLA_RANK, CD_WROWS), F32)
    perm = jnp.concatenate([wt_ref[0:GLA_MAIN, :], wt_ref[GLA_COLS:GLA_COLS + RW_COLS, :],
                            wt_ref[GLA_MAIN:GLA_COLS, :], pad], axis=0)
    o_ref[...] = perm.T.astype(BF16)


def cd_weight(w, layer):
    wt = jnp.swapaxes(w, 1, 2)
    cols = wt.shape[1]
    return pl.pallas_call(
        _cd_weight_kernel,
        grid=(D_MODEL // CD_WROWS,),
        in_specs=[pl.BlockSpec((None, cols, CD_WROWS), lambda i: (layer, 0, i))],
        out_specs=pl.BlockSpec((CD_WROWS, CD_COLS), lambda i: (i, 0)),
        out_shape=jax.ShapeDtypeStruct((D_MODEL, CD_COLS), BF16),
        compiler_params=_params(("parallel",)),
        name="cd_weight",
    )(wt)


def kernel(x, meta, ab_w_in, ab_conf_dw, ab_conf_dw_b, ab_conf_ln_g, ab_conf_ln_b, ab_sc_dw, ab_w_out,
           cd_w_in, cd_gla_w2, cd_gla_b, cd_gla_norm_g, cd_rw_mu, cd_rw_w0, cd_rw_w2, cd_rw_a0, cd_rw_a2,
           cd_rw_g2, cd_rw_kk, cd_rw_ka, cd_rw_rk, cd_rw_ln_g, cd_rw_ln_b, cd_w_out, norm_mix, norm_ffn,
           ffn_w_up, ffn_dw, ffn_w_down, norm_final):
    bs = x.shape[0]
    h = jnp.concatenate([jnp.zeros((bs, PAD, D_MODEL), x.dtype),
                         jnp.broadcast_to(meta[None].astype(x.dtype), (bs, N_META, D_MODEL)), x], axis=1)
    h = h.reshape(MT, D_MODEL)
    row2 = lambda p: p.reshape(1, -1)
    g_final = row2(norm_final)

    z = norm_matmul(h, row2(norm_mix[0]), ab_w_in[0].astype(BF16), TN_AB)
    a_act, s_act = mix_ab(z, ab_conf_dw[0], row2(ab_conf_dw_b[0]), row2(ab_conf_ln_g[0]),
                          row2(ab_conf_ln_b[0]), ab_sc_dw[0])
    h = out_proj(h, a_act, s_act, ab_w_out[0].astype(BF16))
    w_up, w_down = ffn_w_up.astype(BF16), ffn_w_down.astype(BF16)
    h = conv_ffn(h, row2(norm_ffn[0]), w_up, ffn_dw, w_down, g_final, 0, False)

    z = norm_matmul(h, row2(norm_mix[1]), cd_weight(cd_w_in, 0), TN_CD)
    w2p = jnp.concatenate([cd_gla_w2[0], jnp.zeros((GLR_PAD - GLA_RANK, GLA_QK), F32)], axis=0)
    o_act = gla(z, w2p, row2(cd_gla_b[0]), row2(cd_gla_norm_g[0]))
    y_act = rw_mixer(z, row2(cd_rw_mu[0]), row2(cd_rw_w0[0]), cd_rw_w2[0], row2(cd_rw_a0[0]), cd_rw_a2[0],
                     cd_rw_g2[0], row2(cd_rw_kk[0]), row2(cd_rw_ka[0]), row2(cd_rw_ln_g[0]), row2(cd_rw_ln_b[0]),
                     row2(cd_rw_rk[0]))
    h = out_proj(h, o_act, y_act, cd_w_out[0].astype(BF16))
    h = conv_ffn(h, row2(norm_ffn[1]), w_up, ffn_dw, w_down, g_final, 1, True)
    return h.reshape(bs, SEQ, D_MODEL)
```

```python
import functools

import jax
import jax.numpy as jnp
from jax import lax
from jax.experimental import pallas as pl
from jax.experimental.pallas import tpu as pltpu

D_MODEL = 2048
BATCH = 4
SEQ = 2048
N_META = 16
CONF_W = 1024
CONF_K = 31
SC_K = 3
GLA_HEADS = 4
GLA_DK = 128
GLA_DV = 256
GLA_QK = GLA_HEADS * GLA_DK
GLA_V = GLA_HEADS * GLA_DV
GLA_RANK = 16
GLA_GATE_NORM = 16.0
GLA_CHUNK = 64
GLA_COLS = 2 * GLA_QK + 2 * GLA_V + GLA_RANK
RW_HEADS = 16
RW_N = 64
RW_W = RW_HEADS * RW_N
RW_DECAY_RANK = 64
RW_A_RANK = 64
RW_G_RANK = 128
RW_LORA = RW_DECAY_RANK + RW_A_RANK + RW_G_RANK
RW_COLS = 3 * RW_W + RW_LORA
RW_GN_EPS = 64e-5
D_FF = 5632
EPS = 1e-6
LN_EPS = 1e-5

PAD = (-N_META) % GLA_CHUNK
LP = PAD + N_META + SEQ
MT = BATCH * LP
FIRST = PAD
TM = LP // 3
TS = LP // 6
LANES = 128
SUBLANES = 8
GLR_PAD = LANES
GLA_MAIN = 2 * GLA_QK + 2 * GLA_V
CD_COLS = GLA_MAIN + RW_COLS + GLR_PAD
TN_AB = 1280
TN_CD = 2176
TN_FF = 512
TM_LAST = 512
FF_LAST_CHUNK = 128
HALO_FF = 16
HALO_CONF = 32
HALO_SC = 8
GLA_TB = TM
GLA_NB = 2
VMEM_LIMIT = 56 * 1024 * 1024

F32 = jnp.float32
BF16 = jnp.bfloat16


def _params(sem):
    return pltpu.CompilerParams(dimension_semantics=sem, vmem_limit_bytes=VMEM_LIMIT)


def _sigmoid(x):
    return 1.0 / (1.0 + jnp.exp(-x))


def _silu(x):
    return x * _sigmoid(x)


def _softplus(x):
    return jnp.maximum(x, 0.0) + jnp.log(1.0 + jnp.exp(-jnp.abs(x)))


def _split3(x):
    hi = x.astype(BF16)
    r = x - hi.astype(F32)
    mid = r.astype(BF16)
    lo = (r - mid.astype(F32)).astype(BF16)
    return hi, mid, lo


def _dot_x_m01(x, m01):
    return jnp.dot(jnp.concatenate(_split3(x), axis=1), jnp.concatenate([m01] * 3, axis=0),
                   preferred_element_type=F32)


def _dot_m01_x(m01, x):
    return jnp.dot(jnp.concatenate([m01] * 3, axis=1), jnp.concatenate(_split3(x), axis=0),
                   preferred_element_type=F32)


def _rms_rows(x, g):
    ms = jnp.mean(x * x, axis=-1, keepdims=True)
    return x * lax.rsqrt(ms + EPS) * g


NORM_CHUNK = 176


def _norm_rows_into(dst_ref, dst_row0, src_ref, g_ref, chunk=NORM_CHUNK):
    assert src_ref.shape[0] % chunk == 0 and chunk % 16 == 0

    def body(c, carry):
        r = pl.multiple_of(c * chunk, 16)
        x = src_ref[pl.ds(r, chunk), :]
        dst_ref[pl.ds(pl.multiple_of(dst_row0 + r, 16), chunk), :] = _rms_rows(x, g_ref[...]).astype(BF16)
        return carry

    lax.fori_loop(0, src_ref.shape[0] // chunk, body, 0)


def _norm_matmul_kernel(h_ref, g_ref, w_ref, o_ref, hn_ref):
    @pl.when(pl.program_id(1) == 0)
    def _():
        _norm_rows_into(hn_ref, 0, h_ref, g_ref)

    o_ref[...] = jnp.dot(hn_ref[...], w_ref[...], preferred_element_type=F32)


def norm_matmul(h, g, w, tn):
    n = w.shape[1]
    return pl.pallas_call(
        _norm_matmul_kernel,
        grid=(MT // TM, n // tn),
        in_specs=[
            pl.BlockSpec((TM, D_MODEL), lambda i, j: (i, 0)),
            pl.BlockSpec((1, D_MODEL), lambda i, j: (0, 0)),
            pl.BlockSpec((D_MODEL, tn), lambda i, j: (0, j)),
        ],
        out_specs=pl.BlockSpec((TM, tn), lambda i, j: (i, j)),
        out_shape=jax.ShapeDtypeStruct((MT, n), F32),
        scratch_shapes=[pltpu.VMEM((TM, D_MODEL), BF16)],
        compiler_params=_params(("parallel", "arbitrary")),
        name="norm_matmul",
    )(h, g, w)


def _zero_pad_rows(o_ref, tiles_per_batch):
    @pl.when(pl.program_id(0) % tiles_per_batch == 0)
    def _():
        o_ref[0:FIRST, :] = jnp.zeros((FIRST, o_ref.shape[1]), o_ref.dtype)


def _out_proj_kernel(h_ref, a1_ref, a2_ref, w1_ref, w2_ref, o_ref):
    acc = jnp.dot(a1_ref[...], w1_ref[...], preferred_element_type=F32)
    acc = acc + jnp.dot(a2_ref[...], w2_ref[...], preferred_element_type=F32)
    o_ref[...] = h_ref[...] + acc
    _zero_pad_rows(o_ref, LP // TS)


def out_proj(h, a1, a2, w):
    k1, k2 = a1.shape[1], a2.shape[1]
    assert k1 == k2 and w.shape[0] == k1 + k2
    w1 = w2 = w
    return pl.pallas_call(
        _out_proj_kernel,
        grid=(MT // TS,),
        in_specs=[
            pl.BlockSpec((TS, D_MODEL), lambda i: (i, 0)),
            pl.BlockSpec((TS, k1), lambda i: (i, 0)),
            pl.BlockSpec((TS, k2), lambda i: (i, 0)),
            pl.BlockSpec((k1, D_MODEL), lambda i: (0, 0)),
            pl.BlockSpec((k2, D_MODEL), lambda i: (1, 0)),
        ],
        out_specs=pl.BlockSpec((TS, D_MODEL), lambda i: (i, 0)),
        out_shape=jax.ShapeDtypeStruct((MT, D_MODEL), F32),
        compiler_params=_params(("parallel",)),
        name="out_proj",
    )(h, a1, a2, w1, w2)


def _ffn_kernel(h_ref, halo_ref, g_ref, wv_ref, wg_ref, dwv_ref, dwg_ref, wd_ref, gf_ref,
                o_ref, hn_ref, *, last_layer):
    j = pl.program_id(1)
    tm = h_ref.shape[0]
    chunk = FF_LAST_CHUNK if last_layer else NORM_CHUNK

    @pl.when(j == 0)
    def _():
        hn_ref[0:HALO_FF, :] = _rms_rows(halo_ref[...], g_ref[...]).astype(BF16)
        _norm_rows_into(hn_ref, HALO_FF, h_ref, g_ref, chunk)
        o_ref[...] = h_ref[...]

    hn = hn_ref[...]

    def conv(w_ref, dw_ref):
        u = jnp.dot(hn, w_ref[...], preferred_element_type=F32)
        u1 = pltpu.roll(u, 1, 0)
        u2 = pltpu.roll(u, 2, 0)
        c = u * dw_ref[2:3, :] + u1 * dw_ref[1:2, :] + u2 * dw_ref[0:1, :]
        return c[HALO_FF:, :]

    act = (_silu(conv(wg_ref, dwg_ref)) * conv(wv_ref, dwv_ref)).astype(BF16)
    for n in range(D_MODEL // TN_FF):
        cols = slice(n * TN_FF, (n + 1) * TN_FF)
        o_ref[:, cols] += jnp.dot(act, wd_ref[:, cols], preferred_element_type=F32)

    @pl.when(j == pl.num_programs(1) - 1)
    def _():
        if last_layer:
            def body(c, carry):
                rows = pl.ds(pl.multiple_of(c * chunk, 16), chunk)
                o_ref[rows, :] = _rms_rows(o_ref[rows, :], gf_ref[...])
                return carry

            lax.fori_loop(0, tm // chunk, body, 0)
        else:
            _zero_pad_rows(o_ref, LP // TM)


def conv_ffn(h, g, w_up, dw, w_down, g_final, layer, last_layer, w_layer=None):
    nj = D_FF // TN_FF
    wl = layer if w_layer is None else w_layer
    if last_layer:
        tm, rows_out = TM_LAST, BATCH * SEQ
        per_batch = SEQ // tm

        def row0(i):
            return pl.multiple_of((i // per_batch) * LP + (PAD + N_META) + (i % per_batch) * tm, HALO_FF)

        h_spec = pl.BlockSpec((pl.Element(tm), pl.Element(D_MODEL)), lambda i, j: (row0(i), 0))
        halo_spec = pl.BlockSpec((pl.Element(HALO_FF), pl.Element(D_MODEL)),
                                 lambda i, j: (pl.multiple_of(row0(i) - HALO_FF, HALO_FF), 0))
    else:
        tm, rows_out = TM, MT
        halo_blocks = tm // HALO_FF
        h_spec = pl.BlockSpec((tm, D_MODEL), lambda i, j: (i, 0))
        halo_spec = pl.BlockSpec((HALO_FF, D_MODEL), lambda i, j: (jnp.maximum(i * halo_blocks - 1, 0), 0))
    return pl.pallas_call(
        functools.partial(_ffn_kernel, last_layer=last_layer),
        grid=(rows_out // tm, nj),
        in_specs=[
            h_spec,
            halo_spec,
            pl.BlockSpec((1, D_MODEL), lambda i, j: (0, 0)),
            pl.BlockSpec((None, D_MODEL, TN_FF), lambda i, j: (wl, 0, j)),
            pl.BlockSpec((None, D_MODEL, TN_FF), lambda i, j: (wl, 0, nj + j)),
            pl.BlockSpec((None, 3, TN_FF), lambda i, j: (layer, 0, j)),
            pl.BlockSpec((None, 3, TN_FF), lambda i, j: (layer, 0, nj + j)),
            pl.BlockSpec((None, TN_FF, D_MODEL), lambda i, j: (wl, j, 0)),
            pl.BlockSpec((1, D_MODEL), lambda i, j: (0, 0)),
        ],
        out_specs=pl.BlockSpec((tm, D_MODEL), lambda i, j: (i, 0)),
        out_shape=jax.ShapeDtypeStruct((rows_out, D_MODEL), F32),
        scratch_shapes=[pltpu.VMEM((HALO_FF + tm, D_MODEL), BF16)],
        compiler_params=_params(("parallel", "arbitrary")),
        name="conv_ffn",
    )(h, h, g, w_up, w_up, dw, dw, w_down, g_final)


CONF_RB = 88


def _mix_ab_kernel(av_ref, ag_ref, sb_ref, sc_ref, sx_ref, hav_ref, hag_ref, hsc_ref, hsx_ref,
                   cw_ref, cb_ref, lg_ref, lb_ref, sw_ref, a_ref, s_ref, xs_ref, sh_ref, ac_ref, cx_ref):
    xs_ref[0:HALO_CONF, :] = hav_ref[...] * _sigmoid(hag_ref[...])
    xs_ref[HALO_CONF:HALO_CONF + TS, :] = av_ref[...] * _sigmoid(ag_ref[...])
    base = HALO_CONF - (CONF_K - 1)

    sub = SUBLANES
    sh_rows = HALO_CONF + TS
    xs_ref[sh_rows:, :] = jnp.zeros((sub, CONF_W), F32)

    def lane_chunk(c, carry):
        cols = pl.ds(pl.multiple_of(c * LANES, LANES), LANES)
        for s in range(sub):
            sh_ref[s] = xs_ref[pl.ds(s, sh_rows), cols]
        for rb in range(TS // CONF_RB):
            r0 = rb * CONF_RB
            acc = jnp.broadcast_to(cb_ref[:, cols], (CONF_RB, LANES))
            for t in range(CONF_K):
                off = base + t
                acc = acc + sh_ref[off % sub, pl.ds(r0 + off - off % sub, CONF_RB), :] * cw_ref[pl.ds(t, 1), cols]
            ac_ref[pl.ds(r0, CONF_RB), cols] = acc
        return carry

    lax.fori_loop(0, CONF_W // LANES, lane_chunk, 0)

    def norm_rows(c, carry):
        rows = pl.ds(pl.multiple_of(c * NORM_CHUNK, 16), NORM_CHUNK)
        a = ac_ref[rows, :]
        mu = jnp.mean(a, axis=-1, keepdims=True)
        d = a - mu
        var = jnp.mean(d * d, axis=-1, keepdims=True)
        y = d * lax.rsqrt(var + LN_EPS) * lg_ref[...] + lb_ref[...]
        a_ref[rows, :] = _silu(y).astype(BF16)
        return carry

    lax.fori_loop(0, TS // NORM_CHUNK, norm_rows, 0)

    cx_ref[0:HALO_SC, :] = hsc_ref[...] * hsx_ref[...]
    cx_ref[HALO_SC:, :] = sc_ref[...] * sx_ref[...]
    cv = cx_ref[pl.ds(HALO_SC - 2, TS), :] * sw_ref[0:1, :]
    cv = cv + cx_ref[pl.ds(HALO_SC - 1, TS), :] * sw_ref[1:2, :]
    cv = cv + cx_ref[pl.ds(HALO_SC, TS), :] * sw_ref[2:3, :]
    s_ref[...] = (sb_ref[...] * cv).astype(BF16)


def mix_ab(z, conf_dw, conf_b, ln_g, ln_b, sc_dw):
    w = CONF_W
    hc, hs = TS // HALO_CONF, TS // HALO_SC

    def col(k):
        return pl.BlockSpec((TS, w), lambda i, k=k: (i, k))

    def halo(rows, per_tile, k):
        return pl.BlockSpec((rows, w), lambda i, k=k: (jnp.maximum(i * per_tile - 1, 0), k))

    def full(r):
        return pl.BlockSpec((r, w), lambda i: (0, 0))

    return pl.pallas_call(
        _mix_ab_kernel,
        grid=(MT // TS,),
        in_specs=[col(0), col(1), col(2), col(3), col(4),
                  halo(HALO_CONF, hc, 0), halo(HALO_CONF, hc, 1),
                  halo(HALO_SC, hs, 3), halo(HALO_SC, hs, 4),
                  full(CONF_K), full(1), full(1), full(1), full(SC_K)],
        out_specs=[pl.BlockSpec((TS, w), lambda i: (i, 0)), pl.BlockSpec((TS, w), lambda i: (i, 0))],
        out_shape=[jax.ShapeDtypeStruct((MT, w), BF16), jax.ShapeDtypeStruct((MT, w), BF16)],
        scratch_shapes=[pltpu.VMEM((HALO_CONF + TS + SUBLANES, w), F32),
                        pltpu.VMEM((SUBLANES, HALO_CONF + TS, LANES), F32),
                        pltpu.VMEM((TS, w), F32), pltpu.VMEM((HALO_SC + TS, w), F32)],
        compiler_params=_params(("parallel",)),
        name="mix_ab",
    )(z, z, z, z, z, z, z, z, z, conf_dw, conf_b, ln_g, ln_b, sc_dw)


def _gla_kernel(q_ref, k_ref, v_ref, go_ref, glr_ref, w2_ref, b_ref, gn_ref, o_ref, st_ref):
    c_len = GLA_CHUNK
    tb = pl.program_id(1)

    @pl.when(tb == 0)
    def _():
        st_ref[...] = jnp.zeros(st_ref.shape, F32)

    ri = lax.broadcasted_iota(jnp.int32, (c_len, c_len), 0)
    ci = lax.broadcasted_iota(jnp.int32, (c_len, c_len), 1)
    tril = ri >= ci
    tril_b = jnp.where(tril, 1.0, 0.0).astype(BF16)
    row = lax.broadcasted_iota(jnp.int32, (c_len, 1), 0)
    nt_dims = (((1,), (1,)), ((), ()))

    def chunk(c, carry):
        rows = pl.ds(pl.multiple_of(c * c_len, c_len), c_len)
        real = tb * GLA_TB + c * c_len + row >= FIRST
        nb, hs = range(GLA_NB), range(GLA_HEADS)
        ch = [(n, h) for n in nb for h in hs]
        lin = [jnp.dot(glr_ref[n, rows, :], w2_ref[...], preferred_element_type=F32) + b_ref[...] for n in nb]
        log_a = [jnp.where(real, -_softplus(-lin[n]) / GLA_GATE_NORM, 0.0) for n in nb]
        cum_all = [_dot_m01_x(tril_b, log_a[n]) for n in nb]
        q_all = [q_ref[n, rows, :] * (GLA_DK ** -0.5) for n in nb]
        k_all = [k_ref[n, rows, :] for n in nb]
        v_all = [v_ref[n, rows, :] for n in nb]
        kc = [slice(h * GLA_DK, (h + 1) * GLA_DK) for h in hs]
        vc = [slice(h * GLA_DV, (h + 1) * GLA_DV) for h in hs]
        cum = [cum_all[n][:, kc[h]] for n, h in ch]
        last = [x[c_len - 1:c_len, :] for x in cum]
        v = [v_all[n][:, vc[h]] for n, h in ch]
        q_dec = [q_all[n][:, kc[h]] * jnp.exp(cum[i]) for i, (n, h) in enumerate(ch)]
        k_dec = [k_all[n][:, kc[h]] * jnp.exp(-cum[i]) for i, (n, h) in enumerate(ch)]
        k_st = [k_all[n][:, kc[h]] * jnp.exp(last[i] - cum[i]) for i, (n, h) in enumerate(ch)]
        scores = [jnp.where(tril, lax.dot_general(q_dec[i], k_dec[i], nt_dims, preferred_element_type=F32), 0.0)
                  for i in range(len(ch))]
        st = [st_ref[n, h] for n, h in ch]
        o = [jnp.dot(scores[i], v[i], preferred_element_type=F32) for i in range(len(ch))]
        o = [o[i] + lax.dot_general(q_dec[i], st[i], nt_dims, preferred_element_type=F32) for i in range(len(ch))]
        upd = [jnp.dot(v[i].T, k_st[i], preferred_element_type=F32) for i in range(len(ch))]
        for i, (n, h) in enumerate(ch):
            st_ref[n, h] = st[i] * jnp.exp(last[i]) + upd[i]
        o = [x * lax.rsqrt(jnp.mean(x * x, axis=-1, keepdims=True) + EPS) for x in o]
        for n in nb:
            go = go_ref[n, rows, :]
            outs = [o[n * GLA_HEADS + h] * gn_ref[:, vc[h]] * _silu(go[:, vc[h]]) for h in hs]
            o_ref[n, rows, :] = jnp.concatenate(outs, axis=-1).astype(BF16)
        return carry

    lax.fori_loop(0, GLA_TB // c_len, chunk, 0)


def gla(z, w2p, b, gn):
    z3 = z.reshape(BATCH, LP, CD_COLS)

    def zcol(width, k):
        return pl.BlockSpec((GLA_NB, GLA_TB, width), lambda b_, t, k=k: (b_, t, k))

    def full(shape):
        return pl.BlockSpec(shape, lambda b_, t: (0, 0))

    return pl.pallas_call(
        _gla_kernel,
        grid=(BATCH // GLA_NB, LP // GLA_TB),
        in_specs=[zcol(GLA_QK, 0), zcol(GLA_QK, 1), zcol(GLA_V, 1), zcol(GLA_V, 2),
                  zcol(GLR_PAD, (GLA_MAIN + RW_COLS) // GLR_PAD),
                  full((GLR_PAD, GLA_QK)), full((1, GLA_QK)), full((1, GLA_V))],
        out_specs=pl.BlockSpec((GLA_NB, GLA_TB, GLA_V), lambda b_, t: (b_, t, 0)),
        out_shape=jax.ShapeDtypeStruct((BATCH, LP, GLA_V), BF16),
        scratch_shapes=[pltpu.VMEM((GLA_NB, GLA_HEADS, GLA_DV, GLA_DK), F32)],
        compiler_params=_params(("parallel", "arbitrary")),
        name="gla",
    )(z3, z3, z3, z3, z3, w2p, b, gn).reshape(MT, GLA_V)


def _head_sum(x, ones_blk):
    n = x.shape[0]
    stacked = jnp.concatenate([x[:, c * LANES:(c + 1) * LANES] for c in range(RW_W // LANES)], axis=0)
    out = _dot_x_m01(stacked, ones_blk)
    return jnp.concatenate([out[c * n:(c + 1) * n, :] for c in range(RW_W // LANES)], axis=-1)


def _head_ones():
    r = lax.broadcasted_iota(jnp.int32, (LANES, LANES), 0) // RW_N
    c = lax.broadcasted_iota(jnp.int32, (LANES, LANES), 1) // RW_N
    return jnp.where(r == c, 1.0, 0.0).astype(BF16)


RW_PAIRS = RW_HEADS // 2
RW_C = 64
RW_NB = 2
FF_CAST_STEPS_UP = 64
FF_CAST_STEPS_DOWN = 44


def _rw_mixer_kernel(r_ref, k_ref, v_ref, lo_ref, mu_ref, mulo_ref, w0_ref, w2_ref, a0_ref, a2_ref, g2_ref,
                     kkp_ref, kap_ref, lg_ref, lb_ref, rk_ref, wup_ref, wdn_ref, o_ref, wup_o_ref, wdn_o_ref,
                     s_ref, prev_ref):
    c = pl.program_id(1)

    wup_o_ref[...] = wup_ref[...].astype(BF16)
    wdn_o_ref[...] = wdn_ref[...].astype(BF16)

    @pl.when(c == 0)
    def _():
        s_ref[...] = jnp.zeros(s_ref.shape, F32)
        prev_ref[...] = jnp.zeros(prev_ref.shape, F32)

    row = lax.broadcasted_iota(jnp.int32, (RW_C, 1), 0)
    ones_blk = _head_ones()
    real = jnp.logical_or(c != 0, row >= FIRST)
    nbs = range(RW_NB)

    def shift_mix(x, n, col0, mu):
        cols = slice(col0, col0 + x.shape[1])
        prev = jnp.where(row == 0, prev_ref[n, 0:1, cols], pltpu.roll(x, 1, 0))
        prev_ref[n, 0:1, cols] = x[RW_C - 1:RW_C, :]
        return x + (prev - x) * mu

    r = [shift_mix(r_ref[n], n, 0, mu_ref[0:1, :]) for n in nbs]
    k = [shift_mix(k_ref[n], n, RW_W, mu_ref[1:2, :]) for n in nbs]
    v = [shift_mix(v_ref[n], n, 2 * RW_W, mu_ref[2:3, :]) for n in nbs]
    lo = [shift_mix(lo_ref[n], n, 3 * RW_W, mulo_ref[...]) for n in nbs]
    xa0, xg0 = RW_DECAY_RANK, RW_DECAY_RANK + RW_A_RANK
    w_lin = [jnp.dot(jnp.tanh(lo[n][:, 0:xa0]), w2_ref[...], preferred_element_type=F32) for n in nbs]
    a_lin = [jnp.dot(lo[n][:, xa0:xg0], a2_ref[...], preferred_element_type=F32) for n in nbs]
    g = [jnp.dot(_sigmoid(lo[n][:, xg0:]), g2_ref[...], preferred_element_type=F32) for n in nbs]
    lw = [-jnp.exp(-_softplus(-(w0_ref[...] + w_lin[n])) - 0.5) for n in nbs]
    a = [_sigmoid(a0_ref[...] + a_lin[n]) for n in nbs]
    k = [jnp.where(real, k[n], 0.0) for n in nbs]
    v = [jnp.where(real, v[n], 0.0) for n in nbs]
    kk = [k[n] * kkp_ref[...] for n in nbs]
    kk = [kk[n] / jnp.maximum(jnp.sqrt(_head_sum(kk[n] * kk[n], ones_blk)), 1e-12) for n in nbs]
    bb = [kk[n] * a[n] for n in nbs]
    k = [k[n] * (1.0 + (a[n] - 1.0) * kap_ref[...]) for n in nbs]

    def wide(xs):
        return jnp.concatenate(xs, axis=1)

    ri = lax.broadcasted_iota(jnp.int32, (RW_C, RW_C), 0)
    ci = lax.broadcasted_iota(jnp.int32, (RW_C, RW_C), 1)
    tril_b = jnp.where(ri >= ci, 1.0, 0.0).astype(BF16)
    lw_w = wide(lw)
    cum = _dot_m01_x(tril_b, lw_w)
    e_in, e_out, e_prev = jnp.exp(cum), jnp.exp(-cum), jnp.exp(cum - lw_w)
    e_last = jnp.exp(cum[RW_C - 1:RW_C, :])
    a_all = -wide(kk) * e_prev
    bt_all = wide(bb) * e_out
    kt_all = wide(k) * e_out
    rt_all = wide(r) * e_in
    bh_all = bt_all * e_last
    kh_all = kt_all * e_last
    v_all = wide(v)

    lane = lax.broadcasted_iota(jnp.int32, (RW_C, LANES), 1)
    t_idx = lax.broadcasted_iota(jnp.int32, (RW_C, LANES), 0)
    first = lane < RW_N
    strict, incl = t_idx > lane % RW_C, t_idx >= lane % RW_C
    eye = jnp.where(t_idx == lane % RW_C, 1.0, 0.0)
    r2 = lax.broadcasted_iota(jnp.int32, (2 * LANES, LANES), 0)
    c2 = lax.broadcasted_iota(jnp.int32, (2 * LANES, LANES), 1)
    same_head = (r2 % LANES) // RW_N == c2 // RW_N
    zeros = jnp.zeros((LANES, LANES), F32)
    nt = (((1,), (1,)), ((), ()))

    def bd(x):
        return jnp.concatenate([jnp.where(first, x, 0.0), jnp.where(first, 0.0, x)], axis=0)

    def mm(x, y):
        return jnp.dot(x.astype(BF16), y.astype(BF16), preferred_element_type=F32)

    def cat(xs, axis):
        return jnp.concatenate(xs, axis=axis)

    prs = range(RW_NB * RW_PAIRS)
    cols = [slice(p * LANES, (p + 1) * LANES) for p in prs]
    a, bt, kt, rt, bh, kh, vv = ([x[:, cs] for cs in cols]
                                 for x in (a_all, bt_all, kt_all, rt_all, bh_all, kh_all, v_all))
    v_bd = [bd(vv[p]) for p in prs]
    pm = [lax.dot_general(cat([a[p], rt[p]], 0).astype(BF16), cat([bd(bt[p]), bd(kt[p])], 0).astype(BF16), nt,
                          preferred_element_type=F32) for p in prs]
    l_ak = [jnp.where(strict, pm[p][0:RW_C, LANES:], 0.0) for p in prs]
    m_rb = [jnp.where(incl, pm[p][RW_C:, 0:LANES], 0.0) for p in prs]
    m_rk = [jnp.where(incl, pm[p][RW_C:, LANES:], 0.0) for p in prs]
    x = [jnp.where(strict, pm[p][0:RW_C, 0:LANES], 0.0) for p in prs]
    t_inv = [eye + x[p] for p in prs]
    x = [mm(x[p], bd(x[p])) for p in prs]
    for _ in range(RW_C.bit_length() - 3):
        z = [mm(x[p], cat([bd(t_inv[p]), bd(x[p])], 1)) for p in prs]
        t_inv = [t_inv[p] + z[p][:, 0:LANES] for p in prs]
        x = [z[p][:, LANES:] for p in prs]
    t_inv = [t_inv[p] + mm(x[p], bd(t_inv[p])) for p in prs]
    lv = [mm(l_ak[p], v_bd[p]) for p in prs]
    w12 = [mm(t_inv[p], cat([bd(a[p]), bd(lv[p])], 1)) for p in prs]
    wv = [cat([w12[p], cat([jnp.zeros((RW_C, LANES), F32), vv[p]], 1)], 0) for p in prs]
    wv_bd = [cat([cat([bd(w12[p][:, 0:LANES]), bd(w12[p][:, LANES:])], 1), cat([zeros, v_bd[p]], 1)], 0)
             for p in prs]
    qy = [mm(cat([m_rb[p], m_rk[p]], 1), wv_bd[p]) for p in prs]
    gh = [jnp.where(same_head, mm(wv[p].T, cat([bh[p], kh[p]], 0)), 0.0) for p in prs]
    s = [s_ref[p // RW_PAIRS, p % RW_PAIRS] for p in prs]
    ys = [lax.dot_general((rt[p] + qy[p][:, 0:LANES]).astype(BF16), s[p].astype(BF16), nt,
                          preferred_element_type=F32) + qy[p][:, LANES:] for p in prs]
    sg = [jnp.dot(cat(_split3(s[p]), 1), cat([gh[p][0:LANES, :].astype(BF16)] * 3, 0), preferred_element_type=F32)
          for p in prs]
    for p in prs:
        s_ref[p // RW_PAIRS, p % RW_PAIRS] = s[p] * e_last[:, cols[p]] + sg[p] + gh[p][LANES:, :]

    for n in nbs:
        yn = jnp.concatenate(ys[n * RW_PAIRS:(n + 1) * RW_PAIRS], axis=1)
        mean = _head_sum(yn, ones_blk) * (1.0 / RW_N)
        d = yn - mean
        var = _head_sum(d * d, ones_blk) * (1.0 / RW_N)
        yn = d * lax.rsqrt(var + RW_GN_EPS) * lg_ref[...] + lb_ref[...]
        bonus = _head_sum(r[n] * k[n] * rk_ref[...], ones_blk) * v[n]
        o_ref[n] = ((yn + bonus) * g[n]).astype(BF16)


def rw_mixer(z, mu, w0, w2, a0, a2, g2, kkp, kap, lg, lb, rk, w_up, w_down, cast_layer):
    w = RW_W
    steps = LP // RW_C
    up_rows, dn_rows = D_MODEL // FF_CAST_STEPS_UP, D_FF // FF_CAST_STEPS_DOWN

    def cast_blk(rows, width, n_steps, lead):
        return pl.BlockSpec((None, rows, width), lambda b, c: (lead, jnp.minimum(b * steps + c, n_steps - 1), 0))
    c0 = GLA_MAIN // w
    lo0 = (GLA_MAIN + 3 * w) // RW_LORA
    z3 = z.reshape(BATCH, LP, CD_COLS)

    def zcol(width, k):
        return pl.BlockSpec((RW_NB, RW_C, width), lambda b, c, k=k: (b, c, k))

    def full(shape):
        return pl.BlockSpec(shape, lambda b, c: (0, 0))

    y, w_up_b, w_down_b = pl.pallas_call(
        _rw_mixer_kernel,
        grid=(BATCH // RW_NB, LP // RW_C),
        in_specs=[zcol(w, c0), zcol(w, c0 + 1), zcol(w, c0 + 2), zcol(RW_LORA, lo0),
                  full((3, w)), full((1, RW_LORA)),
                  full((1, w)), full((RW_DECAY_RANK, w)), full((1, w)), full((RW_A_RANK, w)), full((RW_G_RANK, w)),
                  full((1, w)), full((1, w)), full((1, w)), full((1, w)), full((1, w)),
                  cast_blk(up_rows, 2 * D_FF, FF_CAST_STEPS_UP, cast_layer),
                  cast_blk(dn_rows, D_MODEL, FF_CAST_STEPS_DOWN, cast_layer)],
        out_specs=[pl.BlockSpec((RW_NB, RW_C, w), lambda b, c: (b, c, 0)),
                   cast_blk(up_rows, 2 * D_FF, FF_CAST_STEPS_UP, 0),
                   cast_blk(dn_rows, D_MODEL, FF_CAST_STEPS_DOWN, 0)],
        out_shape=[jax.ShapeDtypeStruct((BATCH, LP, w), BF16),
                   jax.ShapeDtypeStruct((1, D_MODEL, 2 * D_FF), BF16),
                   jax.ShapeDtypeStruct((1, D_FF, D_MODEL), BF16)],
        scratch_shapes=[pltpu.VMEM((RW_NB, RW_PAIRS, LANES, LANES), F32),
                        pltpu.VMEM((RW_NB, SUBLANES, RW_COLS), F32)],
        compiler_params=_params(("parallel", "arbitrary")),
        name="rw_mixer",
    )(z3, z3, z3, z3, mu[:, :3 * w].reshape(3, w), mu[:, 3 * w:], w0, w2, a0, a2, g2, kkp, kap, lg, lb, rk,
      w_up, w_down)
    return y.reshape(MT, w), w_up_b, w_down_b


CD_WROWS = 256


def _cd_weight_kernel(wt_ref, o_ref):
    pad = jnp.zeros((GLR_PAD - GLA_RANK, CD_WROWS), F32)
    perm = jnp.concatenate([wt_ref[0:GLA_MAIN, :], wt_ref[GLA_COLS:GLA_COLS + RW_COLS, :],
                            wt_ref[GLA_MAIN:GLA_COLS, :], pad], axis=0)
    o_ref[...] = perm.T.astype(BF16)


def cd_weight(w, layer):
    wt = jnp.swapaxes(w, 1, 2)
    cols = wt.shape[1]
    return pl.pallas_call(
        _cd_weight_kernel,
        grid=(D_MODEL // CD_WROWS,),
        in_specs=[pl.BlockSpec((None, cols, CD_WROWS), lambda i: (layer, 0, i))],
        out_specs=pl.BlockSpec((CD_WROWS, CD_COLS), lambda i: (i, 0)),
        out_shape=jax.ShapeDtypeStruct((D_MODEL, CD_COLS), BF16),
        compiler_params=_params(("parallel",)),
        name="cd_weight",
    )(wt)


def kernel(x, meta, ab_w_in, ab_conf_dw, ab_conf_dw_b, ab_conf_ln_g, ab_conf_ln_b, ab_sc_dw, ab_w_out,
           cd_w_in, cd_gla_w2, cd_gla_b, cd_gla_norm_g, cd_rw_mu, cd_rw_w0, cd_rw_w2, cd_rw_a0, cd_rw_a2,
           cd_rw_g2, cd_rw_kk, cd_rw_ka, cd_rw_rk, cd_rw_ln_g, cd_rw_ln_b, cd_w_out, norm_mix, norm_ffn,
           ffn_w_up, ffn_dw, ffn_w_down, norm_final):
    bs = x.shape[0]
    h = jnp.concatenate([jnp.zeros((bs, PAD, D_MODEL), x.dtype),
                         jnp.broadcast_to(meta[None].astype(x.dtype), (bs, N_META, D_MODEL)), x], axis=1)
    h = h.reshape(MT, D_MODEL)
    row2 = lambda p: p.reshape(1, -1)
    g_final = row2(norm_final)

    z = norm_matmul(h, row2(norm_mix[0]), ab_w_in[0].astype(BF16), TN_AB)
    a_act, s_act = mix_ab(z, ab_conf_dw[0], row2(ab_conf_dw_b[0]), row2(ab_conf_ln_g[0]),
                          row2(ab_conf_ln_b[0]), ab_sc_dw[0])
    h = out_proj(h, a_act, s_act, ab_w_out[0].astype(BF16))
    h = conv_ffn(h, row2(norm_ffn[0]), ffn_w_up[:1].astype(BF16), ffn_dw, ffn_w_down[:1].astype(BF16), g_final,
                 0, False)

    z = norm_matmul(h, row2(norm_mix[1]), cd_weight(cd_w_in, 0), TN_CD)
    w2p = jnp.concatenate([cd_gla_w2[0], jnp.zeros((GLR_PAD - GLA_RANK, GLA_QK), F32)], axis=0)
    o_act = gla(z, w2p, row2(cd_gla_b[0]), row2(cd_gla_norm_g[0]))
    y_act, w_up, w_down = rw_mixer(z, row2(cd_rw_mu[0]), row2(cd_rw_w0[0]), cd_rw_w2[0], row2(cd_rw_a0[0]), cd_rw_a2[0],
                     cd_rw_g2[0], row2(cd_rw_kk[0]), row2(cd_rw_ka[0]), row2(cd_rw_ln_g[0]), row2(cd_rw_ln_b[0]),
                     row2(cd_rw_rk[0]), ffn_w_up, ffn_w_down, 1)
    h = out_proj(h, o_act, y_act, cd_w_out[0].astype(BF16))
    h = conv_ffn(h, row2(norm_ffn[1]), w_up, ffn_dw, w_down, g_final, 1, True, w_layer=0)
    return h.reshape(bs, SEQ, D_MODEL)
```

```python
import functools
import math

import jax
import jax.numpy as jnp
from jax import lax
from jax.experimental import pallas as pl
from jax.experimental.pallas import tpu as pltpu

D_MODEL = 2048
BATCH = 4
SEQ = 2048
N_META = 16
CONF_W = 1024
CONF_K = 31
SC_K = 3
GLA_HEADS = 4
GLA_DK = 128
GLA_DV = 256
GLA_QK = GLA_HEADS * GLA_DK
GLA_V = GLA_HEADS * GLA_DV
GLA_RANK = 16
GLA_GATE_NORM = 16.0
GLA_CHUNK = 64
GLA_COLS = 2 * GLA_QK + 2 * GLA_V + GLA_RANK
RW_HEADS = 16
RW_N = 64
RW_W = RW_HEADS * RW_N
RW_DECAY_RANK = 64
RW_A_RANK = 64
RW_G_RANK = 128
RW_LORA = RW_DECAY_RANK + RW_A_RANK + RW_G_RANK
RW_COLS = 3 * RW_W + RW_LORA
RW_GN_EPS = 64e-5
D_FF = 5632
EPS = 1e-6
LN_EPS = 1e-5

PAD = (-N_META) % GLA_CHUNK
LP = PAD + N_META + SEQ
MT = BATCH * LP
FIRST = PAD
TM = LP // 3
TS = LP // 6
LANES = 128
SUBLANES = 8
GLR_PAD = LANES
GLA_MAIN = 2 * GLA_QK + 2 * GLA_V
CD_COLS = GLA_MAIN + RW_COLS + GLR_PAD
TN_AB = 1280
TN_CD = 2176
TN_FF = 512
TM_LAST = 512
FF_LAST_CHUNK = 128
HALO_FF = 16
HALO_CONF = 32
HALO_SC = 8
GLA_TB = TM
GLA_NB = 2
VMEM_LIMIT = 56 * 1024 * 1024

F32 = jnp.float32
BF16 = jnp.bfloat16


def _params(sem):
    return pltpu.CompilerParams(dimension_semantics=sem, vmem_limit_bytes=VMEM_LIMIT)


def _sigmoid(x):
    return 1.0 / (1.0 + jnp.exp(-x))


def _silu(x):
    return x * _sigmoid(x)


def _softplus(x):
    return jnp.maximum(x, 0.0) + jnp.log(1.0 + jnp.exp(-jnp.abs(x)))


def _split3(x):
    hi = x.astype(BF16)
    r = x - hi.astype(F32)
    mid = r.astype(BF16)
    lo = (r - mid.astype(F32)).astype(BF16)
    return hi, mid, lo


def _dot_x_m01(x, m01):
    return jnp.dot(jnp.concatenate(_split3(x), axis=1), jnp.concatenate([m01] * 3, axis=0),
                   preferred_element_type=F32)


def _dot_m01_x(m01, x):
    return jnp.dot(jnp.concatenate([m01] * 3, axis=1), jnp.concatenate(_split3(x), axis=0),
                   preferred_element_type=F32)


def _rms_rows(x, g):
    ms = jnp.mean(x * x, axis=-1, keepdims=True)
    return x * lax.rsqrt(ms + EPS) * g


NORM_CHUNK = 176


def _norm_rows_into(dst_ref, dst_row0, src_ref, g_ref, chunk=NORM_CHUNK):
    assert src_ref.shape[0] % chunk == 0 and chunk % 16 == 0

    def body(c, carry):
        r = pl.multiple_of(c * chunk, 16)
        x = src_ref[pl.ds(r, chunk), :]
        dst_ref[pl.ds(pl.multiple_of(dst_row0 + r, 16), chunk), :] = _rms_rows(x, g_ref[...]).astype(BF16)
        return carry

    lax.fori_loop(0, src_ref.shape[0] // chunk, body, 0)


def _norm_matmul_kernel(h_ref, g_ref, w_ref, o_ref, hn_ref):
    @pl.when(pl.program_id(1) == 0)
    def _():
        _norm_rows_into(hn_ref, 0, h_ref, g_ref)

    o_ref[...] = jnp.dot(hn_ref[...], w_ref[...], preferred_element_type=F32)


def norm_matmul(h, g, w, tn):
    n = w.shape[1]
    return pl.pallas_call(
        _norm_matmul_kernel,
        grid=(MT // TM, n // tn),
        in_specs=[
            pl.BlockSpec((TM, D_MODEL), lambda i, j: (i, 0)),
            pl.BlockSpec((1, D_MODEL), lambda i, j: (0, 0)),
            pl.BlockSpec((D_MODEL, tn), lambda i, j: (0, j)),
        ],
        out_specs=pl.BlockSpec((TM, tn), lambda i, j: (i, j)),
        out_shape=jax.ShapeDtypeStruct((MT, n), F32),
        scratch_shapes=[pltpu.VMEM((TM, D_MODEL), BF16)],
        compiler_params=_params(("parallel", "arbitrary")),
        name="norm_matmul",
    )(h, g, w)


def _zero_pad_rows(o_ref, tiles_per_batch):
    @pl.when(pl.program_id(0) % tiles_per_batch == 0)
    def _():
        o_ref[0:FIRST, :] = jnp.zeros((FIRST, o_ref.shape[1]), o_ref.dtype)


def _out_proj_kernel(h_ref, a1_ref, a2_ref, w1_ref, w2_ref, o_ref):
    acc = jnp.dot(a1_ref[...], w1_ref[...], preferred_element_type=F32)
    acc = acc + jnp.dot(a2_ref[...], w2_ref[...], preferred_element_type=F32)
    o_ref[...] = h_ref[...] + acc
    _zero_pad_rows(o_ref, LP // TS)


def out_proj(h, a1, a2, w):
    k1, k2 = a1.shape[1], a2.shape[1]
    assert k1 == k2 and w.shape[0] == k1 + k2
    w1 = w2 = w
    return pl.pallas_call(
        _out_proj_kernel,
        grid=(MT // TS,),
        in_specs=[
            pl.BlockSpec((TS, D_MODEL), lambda i: (i, 0)),
            pl.BlockSpec((TS, k1), lambda i: (i, 0)),
            pl.BlockSpec((TS, k2), lambda i: (i, 0)),
            pl.BlockSpec((k1, D_MODEL), lambda i: (0, 0)),
            pl.BlockSpec((k2, D_MODEL), lambda i: (1, 0)),
        ],
        out_specs=pl.BlockSpec((TS, D_MODEL), lambda i: (i, 0)),
        out_shape=jax.ShapeDtypeStruct((MT, D_MODEL), F32),
        compiler_params=_params(("parallel",)),
        name="out_proj",
    )(h, a1, a2, w1, w2)


def _ffn_kernel(h_ref, halo_ref, g_ref, wv_ref, wg_ref, dwv_ref, dwg_ref, wd_ref, gf_ref,
                o_ref, hn_ref, *, last_layer):
    j = pl.program_id(1)
    tm = h_ref.shape[0]
    chunk = FF_LAST_CHUNK if last_layer else NORM_CHUNK

    @pl.when(j == 0)
    def _():
        hn_ref[0:HALO_FF, :] = _rms_rows(halo_ref[...], g_ref[...]).astype(BF16)
        _norm_rows_into(hn_ref, HALO_FF, h_ref, g_ref, chunk)
        o_ref[...] = h_ref[...]

    hn = hn_ref[...]

    def conv(w_ref, dw_ref):
        u = jnp.dot(hn, w_ref[...], preferred_element_type=F32)
        u1 = pltpu.roll(u, 1, 0)
        u2 = pltpu.roll(u, 2, 0)
        c = u * dw_ref[2:3, :] + u1 * dw_ref[1:2, :] + u2 * dw_ref[0:1, :]
        return c[HALO_FF:, :]

    act = (_silu(conv(wg_ref, dwg_ref)) * conv(wv_ref, dwv_ref)).astype(BF16)
    for n in range(D_MODEL // TN_FF):
        cols = slice(n * TN_FF, (n + 1) * TN_FF)
        o_ref[:, cols] += jnp.dot(act, wd_ref[:, cols], preferred_element_type=F32)

    @pl.when(j == pl.num_programs(1) - 1)
    def _():
        if last_layer:
            def body(c, carry):
                rows = pl.ds(pl.multiple_of(c * chunk, 16), chunk)
                o_ref[rows, :] = _rms_rows(o_ref[rows, :], gf_ref[...])
                return carry

            lax.fori_loop(0, tm // chunk, body, 0)
        else:
            _zero_pad_rows(o_ref, LP // TM)


def conv_ffn(h, g, w_up, dw, w_down, g_final, layer, last_layer, w_layer=None):
    nj = D_FF // TN_FF
    wl = layer if w_layer is None else w_layer
    if last_layer:
        tm, rows_out = TM_LAST, BATCH * SEQ
        per_batch = SEQ // tm

        def row0(i):
            return pl.multiple_of((i // per_batch) * LP + (PAD + N_META) + (i % per_batch) * tm, HALO_FF)

        h_spec = pl.BlockSpec((pl.Element(tm), pl.Element(D_MODEL)), lambda i, j: (row0(i), 0))
        halo_spec = pl.BlockSpec((pl.Element(HALO_FF), pl.Element(D_MODEL)),
                                 lambda i, j: (pl.multiple_of(row0(i) - HALO_FF, HALO_FF), 0))
    else:
        tm, rows_out = TM, MT
        halo_blocks = tm // HALO_FF
        h_spec = pl.BlockSpec((tm, D_MODEL), lambda i, j: (i, 0))
        halo_spec = pl.BlockSpec((HALO_FF, D_MODEL), lambda i, j: (jnp.maximum(i * halo_blocks - 1, 0), 0))
    return pl.pallas_call(
        functools.partial(_ffn_kernel, last_layer=last_layer),
        grid=(rows_out // tm, nj),
        in_specs=[
            h_spec,
            halo_spec,
            pl.BlockSpec((1, D_MODEL), lambda i, j: (0, 0)),
            pl.BlockSpec((None, D_MODEL, TN_FF), lambda i, j: (wl, 0, j)),
            pl.BlockSpec((None, D_MODEL, TN_FF), lambda i, j: (wl, 0, nj + j)),
            pl.BlockSpec((None, 3, TN_FF), lambda i, j: (layer, 0, j)),
            pl.BlockSpec((None, 3, TN_FF), lambda i, j: (layer, 0, nj + j)),
            pl.BlockSpec((None, TN_FF, D_MODEL), lambda i, j: (wl, j, 0)),
            pl.BlockSpec((1, D_MODEL), lambda i, j: (0, 0)),
        ],
        out_specs=pl.BlockSpec((tm, D_MODEL), lambda i, j: (i, 0)),
        out_shape=jax.ShapeDtypeStruct((rows_out, D_MODEL), F32),
        scratch_shapes=[pltpu.VMEM((HALO_FF + tm, D_MODEL), BF16)],
        compiler_params=_params(("parallel", "arbitrary")),
        name="conv_ffn",
    )(h, h, g, w_up, w_up, dw, dw, w_down, g_final)


CONF_RB = 88


def _mix_ab_kernel(av_ref, ag_ref, sb_ref, sc_ref, sx_ref, hav_ref, hag_ref, hsc_ref, hsx_ref,
                   cw_ref, cb_ref, lg_ref, lb_ref, sw_ref, a_ref, s_ref, xs_ref, sh_ref, ac_ref, cx_ref):
    xs_ref[0:HALO_CONF, :] = hav_ref[...] * _sigmoid(hag_ref[...])
    xs_ref[HALO_CONF:HALO_CONF + TS, :] = av_ref[...] * _sigmoid(ag_ref[...])
    base = HALO_CONF - (CONF_K - 1)

    sub = SUBLANES
    sh_rows = HALO_CONF + TS
    xs_ref[sh_rows:, :] = jnp.zeros((sub, CONF_W), F32)

    def lane_chunk(c, carry):
        cols = pl.ds(pl.multiple_of(c * LANES, LANES), LANES)
        for s in range(sub):
            sh_ref[s] = xs_ref[pl.ds(s, sh_rows), cols]
        for rb in range(TS // CONF_RB):
            r0 = rb * CONF_RB
            acc = jnp.broadcast_to(cb_ref[:, cols], (CONF_RB, LANES))
            for t in range(CONF_K):
                off = base + t
                acc = acc + sh_ref[off % sub, pl.ds(r0 + off - off % sub, CONF_RB), :] * cw_ref[pl.ds(t, 1), cols]
            ac_ref[pl.ds(r0, CONF_RB), cols] = acc
        return carry

    lax.fori_loop(0, CONF_W // LANES, lane_chunk, 0)

    def norm_rows(c, carry):
        rows = pl.ds(pl.multiple_of(c * NORM_CHUNK, 16), NORM_CHUNK)
        a = ac_ref[rows, :]
        mu = jnp.mean(a, axis=-1, keepdims=True)
        d = a - mu
        var = jnp.mean(d * d, axis=-1, keepdims=True)
        y = d * lax.rsqrt(var + LN_EPS) * lg_ref[...] + lb_ref[...]
        a_ref[rows, :] = _silu(y).astype(BF16)
        return carry

    lax.fori_loop(0, TS // NORM_CHUNK, norm_rows, 0)

    cx_ref[0:HALO_SC, :] = hsc_ref[...] * hsx_ref[...]
    cx_ref[HALO_SC:, :] = sc_ref[...] * sx_ref[...]
    cv = cx_ref[pl.ds(HALO_SC - 2, TS), :] * sw_ref[0:1, :]
    cv = cv + cx_ref[pl.ds(HALO_SC - 1, TS), :] * sw_ref[1:2, :]
    cv = cv + cx_ref[pl.ds(HALO_SC, TS), :] * sw_ref[2:3, :]
    s_ref[...] = (sb_ref[...] * cv).astype(BF16)


def mix_ab(z, conf_dw, conf_b, ln_g, ln_b, sc_dw):
    w = CONF_W
    hc, hs = TS // HALO_CONF, TS // HALO_SC

    def col(k):
        return pl.BlockSpec((TS, w), lambda i, k=k: (i, k))

    def halo(rows, per_tile, k):
        return pl.BlockSpec((rows, w), lambda i, k=k: (jnp.maximum(i * per_tile - 1, 0), k))

    def full(r):
        return pl.BlockSpec((r, w), lambda i: (0, 0))

    return pl.pallas_call(
        _mix_ab_kernel,
        grid=(MT // TS,),
        in_specs=[col(0), col(1), col(2), col(3), col(4),
                  halo(HALO_CONF, hc, 0), halo(HALO_CONF, hc, 1),
                  halo(HALO_SC, hs, 3), halo(HALO_SC, hs, 4),
                  full(CONF_K), full(1), full(1), full(1), full(SC_K)],
        out_specs=[pl.BlockSpec((TS, w), lambda i: (i, 0)), pl.BlockSpec((TS, w), lambda i: (i, 0))],
        out_shape=[jax.ShapeDtypeStruct((MT, w), BF16), jax.ShapeDtypeStruct((MT, w), BF16)],
        scratch_shapes=[pltpu.VMEM((HALO_CONF + TS + SUBLANES, w), F32),
                        pltpu.VMEM((SUBLANES, HALO_CONF + TS, LANES), F32),
                        pltpu.VMEM((TS, w), F32), pltpu.VMEM((HALO_SC + TS, w), F32)],
        compiler_params=_params(("parallel",)),
        name="mix_ab",
    )(z, z, z, z, z, z, z, z, z, conf_dw, conf_b, ln_g, ln_b, sc_dw)


def _gla_kernel(q_ref, k_ref, v_ref, go_ref, glr_ref, w2_ref, b_ref, gn_ref, o_ref, st_ref):
    c_len = GLA_CHUNK
    tb = pl.program_id(1)

    @pl.when(tb == 0)
    def _():
        st_ref[...] = jnp.zeros(st_ref.shape, F32)

    ri = lax.broadcasted_iota(jnp.int32, (c_len, c_len), 0)
    ci = lax.broadcasted_iota(jnp.int32, (c_len, c_len), 1)
    tril = ri >= ci
    tril_b = jnp.where(tril, 1.0, 0.0).astype(BF16)
    row = lax.broadcasted_iota(jnp.int32, (c_len, 1), 0)
    nt_dims = (((1,), (1,)), ((), ()))

    def chunk(c, carry):
        rows = pl.ds(pl.multiple_of(c * c_len, c_len), c_len)
        real = tb * GLA_TB + c * c_len + row >= FIRST
        nb, hs = range(GLA_NB), range(GLA_HEADS)
        ch = [(n, h) for n in nb for h in hs]
        lin = [jnp.dot(glr_ref[n, rows, :], w2_ref[...], preferred_element_type=F32) + b_ref[...] for n in nb]
        log_a = [jnp.where(real, -_softplus(-lin[n]) / GLA_GATE_NORM, 0.0) for n in nb]
        cum_all = [_dot_m01_x(tril_b, log_a[n]) for n in nb]
        q_all = [q_ref[n, rows, :] * (GLA_DK ** -0.5) for n in nb]
        k_all = [k_ref[n, rows, :] for n in nb]
        v_all = [v_ref[n, rows, :] for n in nb]
        kc = [slice(h * GLA_DK, (h + 1) * GLA_DK) for h in hs]
        vc = [slice(h * GLA_DV, (h + 1) * GLA_DV) for h in hs]
        cum = [cum_all[n][:, kc[h]] for n, h in ch]
        last = [x[c_len - 1:c_len, :] for x in cum]
        v = [v_all[n][:, vc[h]] for n, h in ch]
        q_dec = [q_all[n][:, kc[h]] * jnp.exp(cum[i]) for i, (n, h) in enumerate(ch)]
        k_dec = [k_all[n][:, kc[h]] * jnp.exp(-cum[i]) for i, (n, h) in enumerate(ch)]
        k_st = [k_all[n][:, kc[h]] * jnp.exp(last[i] - cum[i]) for i, (n, h) in enumerate(ch)]
        scores = [jnp.where(tril, lax.dot_general(q_dec[i], k_dec[i], nt_dims, preferred_element_type=F32), 0.0)
                  for i in range(len(ch))]
        st = [st_ref[n, h] for n, h in ch]
        o = [jnp.dot(scores[i], v[i], preferred_element_type=F32) for i in range(len(ch))]
        o = [o[i] + lax.dot_general(q_dec[i], st[i], nt_dims, preferred_element_type=F32) for i in range(len(ch))]
        upd = [jnp.dot(v[i].T, k_st[i], preferred_element_type=F32) for i in range(len(ch))]
        for i, (n, h) in enumerate(ch):
            st_ref[n, h] = st[i] * jnp.exp(last[i]) + upd[i]
        o = [x * lax.rsqrt(jnp.mean(x * x, axis=-1, keepdims=True) + EPS) for x in o]
        for n in nb:
            go = go_ref[n, rows, :]
            outs = [o[n * GLA_HEADS + h] * gn_ref[:, vc[h]] * _silu(go[:, vc[h]]) for h in hs]
            o_ref[n, rows, :] = jnp.concatenate(outs, axis=-1).astype(BF16)
        return carry

    lax.fori_loop(0, GLA_TB // c_len, chunk, 0)


def gla(z, w2p, b, gn):
    z3 = z.reshape(BATCH, LP, CD_COLS)

    def zcol(width, k):
        return pl.BlockSpec((GLA_NB, GLA_TB, width), lambda b_, t, k=k: (b_, t, k))

    def full(shape):
        return pl.BlockSpec(shape, lambda b_, t: (0, 0))

    return pl.pallas_call(
        _gla_kernel,
        grid=(BATCH // GLA_NB, LP // GLA_TB),
        in_specs=[zcol(GLA_QK, 0), zcol(GLA_QK, 1), zcol(GLA_V, 1), zcol(GLA_V, 2),
                  zcol(GLR_PAD, (GLA_MAIN + RW_COLS) // GLR_PAD),
                  full((GLR_PAD, GLA_QK)), full((1, GLA_QK)), full((1, GLA_V))],
        out_specs=pl.BlockSpec((GLA_NB, GLA_TB, GLA_V), lambda b_, t: (b_, t, 0)),
        out_shape=jax.ShapeDtypeStruct((BATCH, LP, GLA_V), BF16),
        scratch_shapes=[pltpu.VMEM((GLA_NB, GLA_HEADS, GLA_DV, GLA_DK), F32)],
        compiler_params=_params(("parallel", "arbitrary")),
        name="gla",
    )(z3, z3, z3, z3, z3, w2p, b, gn).reshape(MT, GLA_V)


def _head_sum(x, ones_blk):
    n = x.shape[0]
    stacked = jnp.concatenate([x[:, c * LANES:(c + 1) * LANES] for c in range(RW_W // LANES)], axis=0)
    out = _dot_x_m01(stacked, ones_blk)
    return jnp.concatenate([out[c * n:(c + 1) * n, :] for c in range(RW_W // LANES)], axis=-1)


def _head_ones():
    r = lax.broadcasted_iota(jnp.int32, (LANES, LANES), 0) // RW_N
    c = lax.broadcasted_iota(jnp.int32, (LANES, LANES), 1) // RW_N
    return jnp.where(r == c, 1.0, 0.0).astype(BF16)


RW_PAIRS = RW_HEADS // 2
RW_C = 64
RW_NB = 2
FF_CAST_STEPS_UP = 64
FF_CAST_STEPS_DOWN = 44


def _rw_mixer_kernel(r_ref, k_ref, v_ref, lo_ref, mu_ref, mulo_ref, w0_ref, w2_ref, a0_ref, a2_ref, g2_ref,
                     kkp_ref, kap_ref, lg_ref, lb_ref, rk_ref, wup_ref, wdn_ref, o_ref, wup_o_ref, wdn_o_ref,
                     s_ref, prev_ref):
    c = pl.program_id(1)

    wup_o_ref[...] = wup_ref[...].astype(BF16)
    wdn_o_ref[...] = wdn_ref[...].astype(BF16)

    @pl.when(c == 0)
    def _():
        s_ref[...] = jnp.zeros(s_ref.shape, F32)
        prev_ref[...] = jnp.zeros(prev_ref.shape, F32)

    row = lax.broadcasted_iota(jnp.int32, (RW_C, 1), 0)
    ones_blk = _head_ones()
    real = jnp.logical_or(c != 0, row >= FIRST)
    nbs = range(RW_NB)

    def shift_mix(x, n, col0, mu):
        cols = slice(col0, col0 + x.shape[1])
        prev = jnp.where(row == 0, prev_ref[n, 0:1, cols], pltpu.roll(x, 1, 0))
        prev_ref[n, 0:1, cols] = x[RW_C - 1:RW_C, :]
        return x + (prev - x) * mu

    r = [shift_mix(r_ref[n], n, 0, mu_ref[0:1, :]) for n in nbs]
    k = [shift_mix(k_ref[n], n, RW_W, mu_ref[1:2, :]) for n in nbs]
    v = [shift_mix(v_ref[n], n, 2 * RW_W, mu_ref[2:3, :]) for n in nbs]
    lo = [shift_mix(lo_ref[n], n, 3 * RW_W, mulo_ref[...]) for n in nbs]
    xa0, xg0 = RW_DECAY_RANK, RW_DECAY_RANK + RW_A_RANK
    w_lin = [jnp.dot(jnp.tanh(lo[n][:, 0:xa0]), w2_ref[...], preferred_element_type=F32) for n in nbs]
    a_lin = [jnp.dot(lo[n][:, xa0:xg0], a2_ref[...], preferred_element_type=F32) for n in nbs]
    g = [jnp.dot(_sigmoid(lo[n][:, xg0:]), g2_ref[...], preferred_element_type=F32) for n in nbs]
    lw = [_sigmoid(w0_ref[...] + w_lin[n]) * (-math.exp(-0.5)) for n in nbs]
    a = [_sigmoid(a0_ref[...] + a_lin[n]) for n in nbs]
    k = [jnp.where(real, k[n], 0.0) for n in nbs]
    v = [jnp.where(real, v[n], 0.0) for n in nbs]
    kk = [k[n] * kkp_ref[...] for n in nbs]
    kk = [kk[n] / jnp.maximum(jnp.sqrt(_head_sum(kk[n] * kk[n], ones_blk)), 1e-12) for n in nbs]
    bb = [kk[n] * a[n] for n in nbs]
    k = [k[n] * (1.0 + (a[n] - 1.0) * kap_ref[...]) for n in nbs]

    def wide(xs):
        return jnp.concatenate(xs, axis=1)

    ri = lax.broadcasted_iota(jnp.int32, (RW_C, RW_C), 0)
    ci = lax.broadcasted_iota(jnp.int32, (RW_C, RW_C), 1)
    tril_b = jnp.where(ri >= ci, 1.0, 0.0).astype(BF16)
    lw_w = wide(lw)
    cum = _dot_m01_x(tril_b, lw_w)
    e_in, e_out, e_prev = jnp.exp(cum), jnp.exp(-cum), jnp.exp(cum - lw_w)
    e_last = jnp.exp(cum[RW_C - 1:RW_C, :])
    a_all = -wide(kk) * e_prev
    bt_all = wide(bb) * e_out
    kt_all = wide(k) * e_out
    rt_all = wide(r) * e_in
    bh_all = bt_all * e_last
    kh_all = kt_all * e_last
    v_all = wide(v)

    lane = lax.broadcasted_iota(jnp.int32, (RW_C, LANES), 1)
    t_idx = lax.broadcasted_iota(jnp.int32, (RW_C, LANES), 0)
    first = lane < RW_N
    strict, incl = t_idx > lane % RW_C, t_idx >= lane % RW_C
    eye = jnp.where(t_idx == lane % RW_C, 1.0, 0.0)
    r2 = lax.broadcasted_iota(jnp.int32, (2 * LANES, LANES), 0)
    c2 = lax.broadcasted_iota(jnp.int32, (2 * LANES, LANES), 1)
    same_head = (r2 % LANES) // RW_N == c2 // RW_N
    zeros = jnp.zeros((LANES, LANES), F32)
    nt = (((1,), (1,)), ((), ()))

    def bd(x):
        return jnp.concatenate([jnp.where(first, x, 0.0), jnp.where(first, 0.0, x)], axis=0)

    def mm(x, y):
        return jnp.dot(x.astype(BF16), y.astype(BF16), preferred_element_type=F32)

    def cat(xs, axis):
        return jnp.concatenate(xs, axis=axis)

    prs = range(RW_NB * RW_PAIRS)
    cols = [slice(p * LANES, (p + 1) * LANES) for p in prs]
    a, bt, kt, rt, bh, kh, vv = ([x[:, cs] for cs in cols]
                                 for x in (a_all, bt_all, kt_all, rt_all, bh_all, kh_all, v_all))
    v_bd = [bd(vv[p]) for p in prs]
    pm = [lax.dot_general(cat([a[p], rt[p]], 0).astype(BF16), cat([bd(bt[p]), bd(kt[p])], 0).astype(BF16), nt,
                          preferred_element_type=F32) for p in prs]
    l_ak = [jnp.where(strict, pm[p][0:RW_C, LANES:], 0.0) for p in prs]
    m_rb = [jnp.where(incl, pm[p][RW_C:, 0:LANES], 0.0) for p in prs]
    m_rk = [jnp.where(incl, pm[p][RW_C:, LANES:], 0.0) for p in prs]
    x = [jnp.where(strict, pm[p][0:RW_C, 0:LANES], 0.0) for p in prs]
    t_inv = [eye + x[p] for p in prs]
    x = [mm(x[p], bd(x[p])) for p in prs]
    for _ in range(RW_C.bit_length() - 3):
        z = [mm(x[p], cat([bd(t_inv[p]), bd(x[p])], 1)) for p in prs]
        t_inv = [t_inv[p] + z[p][:, 0:LANES] for p in prs]
        x = [z[p][:, LANES:] for p in prs]
    t_inv = [t_inv[p] + mm(x[p], bd(t_inv[p])) for p in prs]
    lv = [mm(l_ak[p], v_bd[p]) for p in prs]
    w12 = [mm(t_inv[p], cat([bd(a[p]), bd(lv[p])], 1)) for p in prs]
    wv = [cat([w12[p], cat([jnp.zeros((RW_C, LANES), F32), vv[p]], 1)], 0) for p in prs]
    wv_bd = [cat([cat([bd(w12[p][:, 0:LANES]), bd(w12[p][:, LANES:])], 1), cat([zeros, v_bd[p]], 1)], 0)
             for p in prs]
    qy = [mm(cat([m_rb[p], m_rk[p]], 1), wv_bd[p]) for p in prs]
    gh = [jnp.where(same_head, mm(wv[p].T, cat([bh[p], kh[p]], 0)), 0.0) for p in prs]
    s = [s_ref[p // RW_PAIRS, p % RW_PAIRS] for p in prs]
    ys = [lax.dot_general((rt[p] + qy[p][:, 0:LANES]).astype(BF16), s[p].astype(BF16), nt,
                          preferred_element_type=F32) + qy[p][:, LANES:] for p in prs]
    sg = [jnp.dot(cat(_split3(s[p]), 1), cat([gh[p][0:LANES, :].astype(BF16)] * 3, 0), preferred_element_type=F32)
          for p in prs]
    for p in prs:
        s_ref[p // RW_PAIRS, p % RW_PAIRS] = s[p] * e_last[:, cols[p]] + sg[p] + gh[p][LANES:, :]

    for n in nbs:
        yn = jnp.concatenate(ys[n * RW_PAIRS:(n + 1) * RW_PAIRS], axis=1)
        mean = _head_sum(yn, ones_blk) * (1.0 / RW_N)
        d = yn - mean
        var = _head_sum(d * d, ones_blk) * (1.0 / RW_N)
        yn = d * lax.rsqrt(var + RW_GN_EPS) * lg_ref[...] + lb_ref[...]
        bonus = _head_sum(r[n] * k[n] * rk_ref[...], ones_blk) * v[n]
        o_ref[n] = ((yn + bonus) * g[n]).astype(BF16)


def rw_mixer(z, mu, w0, w2, a0, a2, g2, kkp, kap, lg, lb, rk, w_up, w_down, cast_layer):
    w = RW_W
    steps = LP // RW_C
    up_rows, dn_rows = D_MODEL // FF_CAST_STEPS_UP, D_FF // FF_CAST_STEPS_DOWN

    def cast_blk(rows, width, n_steps, lead):
        return pl.BlockSpec((None, rows, width), lambda b, c: (lead, jnp.minimum(b * steps + c, n_steps - 1), 0))
    c0 = GLA_MAIN // w
    lo0 = (GLA_MAIN + 3 * w) // RW_LORA
    z3 = z.reshape(BATCH, LP, CD_COLS)

    def zcol(width, k):
        return pl.BlockSpec((RW_NB, RW_C, width), lambda b, c, k=k: (b, c, k))

    def full(shape):
        return pl.BlockSpec(shape, lambda b, c: (0, 0))

    y, w_up_b, w_down_b = pl.pallas_call(
        _rw_mixer_kernel,
        grid=(BATCH // RW_NB, LP // RW_C),
        in_specs=[zcol(w, c0), zcol(w, c0 + 1), zcol(w, c0 + 2), zcol(RW_LORA, lo0),
                  full((3, w)), full((1, RW_LORA)),
                  full((1, w)), full((RW_DECAY_RANK, w)), full((1, w)), full((RW_A_RANK, w)), full((RW_G_RANK, w)),
                  full((1, w)), full((1, w)), full((1, w)), full((1, w)), full((1, w)),
                  cast_blk(up_rows, 2 * D_FF, FF_CAST_STEPS_UP, cast_layer),
                  cast_blk(dn_rows, D_MODEL, FF_CAST_STEPS_DOWN, cast_layer)],
        out_specs=[pl.BlockSpec((RW_NB, RW_C, w), lambda b, c: (b, c, 0)),
                   cast_blk(up_rows, 2 * D_FF, FF_CAST_STEPS_UP, 0),
                   cast_blk(dn_rows, D_MODEL, FF_CAST_STEPS_DOWN, 0)],
        out_shape=[jax.ShapeDtypeStruct((BATCH, LP, w), BF16),
                   jax.ShapeDtypeStruct((1, D_MODEL, 2 * D_FF), BF16),
                   jax.ShapeDtypeStruct((1, D_FF, D_MODEL), BF16)],
        scratch_shapes=[pltpu.VMEM((RW_NB, RW_PAIRS, LANES, LANES), F32),
                        pltpu.VMEM((RW_NB, SUBLANES, RW_COLS), F32)],
        compiler_params=_params(("parallel", "arbitrary")),
        name="rw_mixer",
    )(z3, z3, z3, z3, mu[:, :3 * w].reshape(3, w), mu[:, 3 * w:], w0, w2, a0, a2, g2, kkp, kap, lg, lb, rk,
      w_up, w_down)
    return y.reshape(MT, w), w_up_b, w_down_b


CD_WROWS = 256


def _cd_weight_kernel(wt_ref, o_ref):
    pad = jnp.zeros((GLR_PAD - GLA_RANK, CD_WROWS), F32)
    perm = jnp.concatenate([wt_ref[0:GLA_MAIN, :], wt_ref[GLA_COLS:GLA_COLS + RW_COLS, :],
                            wt_ref[GLA_MAIN:GLA_COLS, :], pad], axis=0)
    o_ref[...] = perm.T.astype(BF16)


def cd_weight(w, layer):
    wt = jnp.swapaxes(w, 1, 2)
    cols = wt.shape[1]
    return pl.pallas_call(
        _cd_weight_kernel,
        grid=(D_MODEL // CD_WROWS,),
        in_specs=[pl.BlockSpec((None, cols, CD_WROWS), lambda i: (layer, 0, i))],
        out_specs=pl.BlockSpec((CD_WROWS, CD_COLS), lambda i: (i, 0)),
        out_shape=jax.ShapeDtypeStruct((D_MODEL, CD_COLS), BF16),
        compiler_params=_params(("parallel",)),
        name="cd_weight",
    )(wt)


def kernel(x, meta, ab_w_in, ab_conf_dw, ab_conf_dw_b, ab_conf_ln_g, ab_conf_ln_b, ab_sc_dw, ab_w_out,
           cd_w_in, cd_gla_w2, cd_gla_b, cd_gla_norm_g, cd_rw_mu, cd_rw_w0, cd_rw_w2, cd_rw_a0, cd_rw_a2,
           cd_rw_g2, cd_rw_kk, cd_rw_ka, cd_rw_rk, cd_rw_ln_g, cd_rw_ln_b, cd_w_out, norm_mix, norm_ffn,
           ffn_w_up, ffn_dw, ffn_w_down, norm_final):
    bs = x.shape[0]
    h = jnp.concatenate([jnp.zeros((bs, PAD, D_MODEL), x.dtype),
                         jnp.broadcast_to(meta[None].astype(x.dtype), (bs, N_META, D_MODEL)), x], axis=1)
    h = h.reshape(MT, D_MODEL)
    row2 = lambda p: p.reshape(1, -1)
    g_final = row2(norm_final)

    z = norm_matmul(h, row2(norm_mix[0]), ab_w_in[0].astype(BF16), TN_AB)
    a_act, s_act = mix_ab(z, ab_conf_dw[0], row2(ab_conf_dw_b[0]), row2(ab_conf_ln_g[0]),
                          row2(ab_conf_ln_b[0]), ab_sc_dw[0])
    h = out_proj(h, a_act, s_act, ab_w_out[0].astype(BF16))
    h = conv_ffn(h, row2(norm_ffn[0]), ffn_w_up[:1].astype(BF16), ffn_dw, ffn_w_down[:1].astype(BF16), g_final,
                 0, False)

    z = norm_matmul(h, row2(norm_mix[1]), cd_weight(cd_w_in, 0), TN_CD)
    w2p = jnp.concatenate([cd_gla_w2[0], jnp.zeros((GLR_PAD - GLA_RANK, GLA_QK), F32)], axis=0)
    o_act = gla(z, w2p, row2(cd_gla_b[0]), row2(cd_gla_norm_g[0]))
    y_act, w_up, w_down = rw_mixer(z, row2(cd_rw_mu[0]), row2(cd_rw_w0[0]), cd_rw_w2[0], row2(cd_rw_a0[0]), cd_rw_a2[0],
                     cd_rw_g2[0], row2(cd_rw_kk[0]), row2(cd_rw_ka[0]), row2(cd_rw_ln_g[0]), row2(cd_rw_ln_b[0]),
                     row2(cd_rw_rk[0]), ffn_w_up, ffn_w_down, 1)
    h = out_proj(h, o_act, y_act, cd_w_out[0].astype(BF16))
    h = conv_ffn(h, row2(norm_ffn[1]), w_up, ffn_dw, w_down, g_final, 1, True, w_layer=0)
    return h.reshape(bs, SEQ, D_MODEL)
```

```python
import functools
import math

import jax
import jax.numpy as jnp
from jax import lax
from jax.experimental import pallas as pl
from jax.experimental.pallas import tpu as pltpu

D_MODEL = 2048
BATCH = 4
SEQ = 2048
N_META = 16
CONF_W = 1024
CONF_K = 31
SC_K = 3
GLA_HEADS = 4
GLA_DK = 128
GLA_DV = 256
GLA_QK = GLA_HEADS * GLA_DK
GLA_V = GLA_HEADS * GLA_DV
GLA_RANK = 16
GLA_GATE_NORM = 16.0
GLA_CHUNK = 64
GLA_COLS = 2 * GLA_QK + 2 * GLA_V + GLA_RANK
RW_HEADS = 16
RW_N = 64
RW_W = RW_HEADS * RW_N
RW_DECAY_RANK = 64
RW_A_RANK = 64
RW_G_RANK = 128
RW_LORA = RW_DECAY_RANK + RW_A_RANK + RW_G_RANK
RW_COLS = 3 * RW_W + RW_LORA
RW_GN_EPS = 64e-5
D_FF = 5632
EPS = 1e-6
LN_EPS = 1e-5

PAD = (-N_META) % GLA_CHUNK
LP = PAD + N_META + SEQ
MT = BATCH * LP
FIRST = PAD
TM = LP // 3
TS = LP // 6
LANES = 128
SUBLANES = 8
GLR_PAD = LANES
GLA_MAIN = 2 * GLA_QK + 2 * GLA_V
CD_COLS = GLA_MAIN + RW_COLS + GLR_PAD
TN_AB = 1280
TN_CD = 2176
TN_FF = 512
TM_LAST = 512
FF_LAST_CHUNK = 128
HALO_FF = 16
HALO_CONF = 32
HALO_SC = 8
GLA_TB = 3 * GLA_CHUNK
GLA_NB = 4
VMEM_LIMIT = 56 * 1024 * 1024

F32 = jnp.float32
BF16 = jnp.bfloat16


def _params(sem):
    return pltpu.CompilerParams(dimension_semantics=sem, vmem_limit_bytes=VMEM_LIMIT)


def _sigmoid(x):
    return 1.0 / (1.0 + jnp.exp(-x))


def _silu(x):
    return x * _sigmoid(x)


def _softplus(x):
    return jnp.maximum(x, 0.0) + jnp.log(1.0 + jnp.exp(-jnp.abs(x)))


def _split3(x):
    hi = x.astype(BF16)
    r = x - hi.astype(F32)
    mid = r.astype(BF16)
    lo = (r - mid.astype(F32)).astype(BF16)
    return hi, mid, lo


def _dot_x_m01(x, m01):
    return jnp.dot(jnp.concatenate(_split3(x), axis=1), jnp.concatenate([m01] * 3, axis=0),
                   preferred_element_type=F32)


def _dot_m01_x(m01, x):
    return jnp.dot(jnp.concatenate([m01] * 3, axis=1), jnp.concatenate(_split3(x), axis=0),
                   preferred_element_type=F32)


def _rms_rows(x, g):
    ms = jnp.mean(x * x, axis=-1, keepdims=True)
    return x * lax.rsqrt(ms + EPS) * g


NORM_CHUNK = 176


def _norm_rows_into(dst_ref, dst_row0, src_ref, g_ref, chunk=NORM_CHUNK):
    assert src_ref.shape[0] % chunk == 0 and chunk % 16 == 0

    def body(c, carry):
        r = pl.multiple_of(c * chunk, 16)
        x = src_ref[pl.ds(r, chunk), :]
        dst_ref[pl.ds(pl.multiple_of(dst_row0 + r, 16), chunk), :] = _rms_rows(x, g_ref[...]).astype(BF16)
        return carry

    lax.fori_loop(0, src_ref.shape[0] // chunk, body, 0)


def _norm_matmul_kernel(h_ref, g_ref, w_ref, o_ref, hn_ref):
    @pl.when(pl.program_id(1) == 0)
    def _():
        _norm_rows_into(hn_ref, 0, h_ref, g_ref)

    o_ref[...] = jnp.dot(hn_ref[...], w_ref[...], preferred_element_type=F32)


def norm_matmul(h, g, w, tn):
    n = w.shape[1]
    return pl.pallas_call(
        _norm_matmul_kernel,
        grid=(MT // TM, n // tn),
        in_specs=[
            pl.BlockSpec((TM, D_MODEL), lambda i, j: (i, 0)),
            pl.BlockSpec((1, D_MODEL), lambda i, j: (0, 0)),
            pl.BlockSpec((D_MODEL, tn), lambda i, j: (0, j)),
        ],
        out_specs=pl.BlockSpec((TM, tn), lambda i, j: (i, j)),
        out_shape=jax.ShapeDtypeStruct((MT, n), F32),
        scratch_shapes=[pltpu.VMEM((TM, D_MODEL), BF16)],
        compiler_params=_params(("parallel", "arbitrary")),
        name="norm_matmul",
    )(h, g, w)


def _zero_pad_rows(o_ref, tiles_per_batch):
    @pl.when(pl.program_id(0) % tiles_per_batch == 0)
    def _():
        o_ref[0:FIRST, :] = jnp.zeros((FIRST, o_ref.shape[1]), o_ref.dtype)


def _out_proj_kernel(h_ref, a1_ref, a2_ref, w1_ref, w2_ref, o_ref):
    acc = jnp.dot(a1_ref[...], w1_ref[...], preferred_element_type=F32)
    acc = acc + jnp.dot(a2_ref[...], w2_ref[...], preferred_element_type=F32)
    o_ref[...] = h_ref[...] + acc
    _zero_pad_rows(o_ref, LP // TS)


def out_proj(h, a1, a2, w):
    k1, k2 = a1.shape[1], a2.shape[1]
    assert k1 == k2 and w.shape[0] == k1 + k2
    w1 = w2 = w
    return pl.pallas_call(
        _out_proj_kernel,
        grid=(MT // TS,),
        in_specs=[
            pl.BlockSpec((TS, D_MODEL), lambda i: (i, 0)),
            pl.BlockSpec((TS, k1), lambda i: (i, 0)),
            pl.BlockSpec((TS, k2), lambda i: (i, 0)),
            pl.BlockSpec((k1, D_MODEL), lambda i: (0, 0)),
            pl.BlockSpec((k2, D_MODEL), lambda i: (1, 0)),
        ],
        out_specs=pl.BlockSpec((TS, D_MODEL), lambda i: (i, 0)),
        out_shape=jax.ShapeDtypeStruct((MT, D_MODEL), F32),
        compiler_params=_params(("parallel",)),
        name="out_proj",
    )(h, a1, a2, w1, w2)


def _ffn_kernel(h_ref, halo_ref, g_ref, wv_ref, wg_ref, dwv_ref, dwg_ref, wd_ref, gf_ref,
                o_ref, hn_ref, *, last_layer):
    j = pl.program_id(1)
    tm = h_ref.shape[0]
    chunk = FF_LAST_CHUNK if last_layer else NORM_CHUNK

    @pl.when(j == 0)
    def _():
        hn_ref[0:HALO_FF, :] = _rms_rows(halo_ref[...], g_ref[...]).astype(BF16)
        _norm_rows_into(hn_ref, HALO_FF, h_ref, g_ref, chunk)
        o_ref[...] = h_ref[...]

    hn = hn_ref[...]

    def conv(w_ref, dw_ref):
        u = jnp.dot(hn, w_ref[...], preferred_element_type=F32)
        u1 = pltpu.roll(u, 1, 0)
        u2 = pltpu.roll(u, 2, 0)
        c = u * dw_ref[2:3, :] + u1 * dw_ref[1:2, :] + u2 * dw_ref[0:1, :]
        return c[HALO_FF:, :]

    act = (_silu(conv(wg_ref, dwg_ref)) * conv(wv_ref, dwv_ref)).astype(BF16)
    for n in range(D_MODEL // TN_FF):
        cols = slice(n * TN_FF, (n + 1) * TN_FF)
        o_ref[:, cols] += jnp.dot(act, wd_ref[:, cols], preferred_element_type=F32)

    @pl.when(j == pl.num_programs(1) - 1)
    def _():
        if last_layer:
            def body(c, carry):
                rows = pl.ds(pl.multiple_of(c * chunk, 16), chunk)
                o_ref[rows, :] = _rms_rows(o_ref[rows, :], gf_ref[...])
                return carry

            lax.fori_loop(0, tm // chunk, body, 0)
        else:
            _zero_pad_rows(o_ref, LP // TM)


def conv_ffn(h, g, w_up, dw, w_down, g_final, layer, last_layer, w_layer=None):
    nj = D_FF // TN_FF
    wl = layer if w_layer is None else w_layer
    if last_layer:
        tm, rows_out = TM_LAST, BATCH * SEQ
        per_batch = SEQ // tm

        def row0(i):
            return pl.multiple_of((i // per_batch) * LP + (PAD + N_META) + (i % per_batch) * tm, HALO_FF)

        h_spec = pl.BlockSpec((pl.Element(tm), pl.Element(D_MODEL)), lambda i, j: (row0(i), 0))
        halo_spec = pl.BlockSpec((pl.Element(HALO_FF), pl.Element(D_MODEL)),
                                 lambda i, j: (pl.multiple_of(row0(i) - HALO_FF, HALO_FF), 0))
    else:
        tm, rows_out = TM, MT
        halo_blocks = tm // HALO_FF
        h_spec = pl.BlockSpec((tm, D_MODEL), lambda i, j: (i, 0))
        halo_spec = pl.BlockSpec((HALO_FF, D_MODEL), lambda i, j: (jnp.maximum(i * halo_blocks - 1, 0), 0))
    return pl.pallas_call(
        functools.partial(_ffn_kernel, last_layer=last_layer),
        grid=(rows_out // tm, nj),
        in_specs=[
            h_spec,
            halo_spec,
            pl.BlockSpec((1, D_MODEL), lambda i, j: (0, 0)),
            pl.BlockSpec((None, D_MODEL, TN_FF), lambda i, j: (wl, 0, j)),
            pl.BlockSpec((None, D_MODEL, TN_FF), lambda i, j: (wl, 0, nj + j)),
            pl.BlockSpec((None, 3, TN_FF), lambda i, j: (layer, 0, j)),
            pl.BlockSpec((None, 3, TN_FF), lambda i, j: (layer, 0, nj + j)),
            pl.BlockSpec((None, TN_FF, D_MODEL), lambda i, j: (wl, j, 0)),
            pl.BlockSpec((1, D_MODEL), lambda i, j: (0, 0)),
        ],
        out_specs=pl.BlockSpec((tm, D_MODEL), lambda i, j: (i, 0)),
        out_shape=jax.ShapeDtypeStruct((rows_out, D_MODEL), F32),
        scratch_shapes=[pltpu.VMEM((HALO_FF + tm, D_MODEL), BF16)],
        compiler_params=_params(("parallel", "arbitrary")),
        name="conv_ffn",
    )(h, h, g, w_up, w_up, dw, dw, w_down, g_final)


CONF_RB = 88


def _mix_ab_kernel(av_ref, ag_ref, sb_ref, sc_ref, sx_ref, hav_ref, hag_ref, hsc_ref, hsx_ref,
                   cw_ref, cb_ref, lg_ref, lb_ref, sw_ref, a_ref, s_ref, xs_ref, sh_ref, ac_ref, cx_ref):
    xs_ref[0:HALO_CONF, :] = hav_ref[...] * _sigmoid(hag_ref[...])
    xs_ref[HALO_CONF:HALO_CONF + TS, :] = av_ref[...] * _sigmoid(ag_ref[...])
    base = HALO_CONF - (CONF_K - 1)

    sub = SUBLANES
    sh_rows = HALO_CONF + TS
    xs_ref[sh_rows:, :] = jnp.zeros((sub, CONF_W), F32)

    def lane_chunk(c, carry):
        cols = pl.ds(pl.multiple_of(c * LANES, LANES), LANES)
        for s in range(sub):
            sh_ref[s] = xs_ref[pl.ds(s, sh_rows), cols]
        for rb in range(TS // CONF_RB):
            r0 = rb * CONF_RB
            acc = jnp.broadcast_to(cb_ref[:, cols], (CONF_RB, LANES))
            for t in range(CONF_K):
                off = base + t
                acc = acc + sh_ref[off % sub, pl.ds(r0 + off - off % sub, CONF_RB), :] * cw_ref[pl.ds(t, 1), cols]
            ac_ref[pl.ds(r0, CONF_RB), cols] = acc
        return carry

    lax.fori_loop(0, CONF_W // LANES, lane_chunk, 0)

    def norm_rows(c, carry):
        rows = pl.ds(pl.multiple_of(c * NORM_CHUNK, 16), NORM_CHUNK)
        a = ac_ref[rows, :]
        mu = jnp.mean(a, axis=-1, keepdims=True)
        d = a - mu
        var = jnp.mean(d * d, axis=-1, keepdims=True)
        y = d * lax.rsqrt(var + LN_EPS) * lg_ref[...] + lb_ref[...]
        a_ref[rows, :] = _silu(y).astype(BF16)
        return carry

    lax.fori_loop(0, TS // NORM_CHUNK, norm_rows, 0)

    cx_ref[0:HALO_SC, :] = hsc_ref[...] * hsx_ref[...]
    cx_ref[HALO_SC:, :] = sc_ref[...] * sx_ref[...]
    cv = cx_ref[pl.ds(HALO_SC - 2, TS), :] * sw_ref[0:1, :]
    cv = cv + cx_ref[pl.ds(HALO_SC - 1, TS), :] * sw_ref[1:2, :]
    cv = cv + cx_ref[pl.ds(HALO_SC, TS), :] * sw_ref[2:3, :]
    s_ref[...] = (sb_ref[...] * cv).astype(BF16)


def mix_ab(z, conf_dw, conf_b, ln_g, ln_b, sc_dw):
    w = CONF_W
    hc, hs = TS // HALO_CONF, TS // HALO_SC

    def col(k):
        return pl.BlockSpec((TS, w), lambda i, k=k: (i, k))

    def halo(rows, per_tile, k):
        return pl.BlockSpec((rows, w), lambda i, k=k: (jnp.maximum(i * per_tile - 1, 0), k))

    def full(r):
        return pl.BlockSpec((r, w), lambda i: (0, 0))

    return pl.pallas_call(
        _mix_ab_kernel,
        grid=(MT // TS,),
        in_specs=[col(0), col(1), col(2), col(3), col(4),
                  halo(HALO_CONF, hc, 0), halo(HALO_CONF, hc, 1),
                  halo(HALO_SC, hs, 3), halo(HALO_SC, hs, 4),
                  full(CONF_K), full(1), full(1), full(1), full(SC_K)],
        out_specs=[pl.BlockSpec((TS, w), lambda i: (i, 0)), pl.BlockSpec((TS, w), lambda i: (i, 0))],
        out_shape=[jax.ShapeDtypeStruct((MT, w), BF16), jax.ShapeDtypeStruct((MT, w), BF16)],
        scratch_shapes=[pltpu.VMEM((HALO_CONF + TS + SUBLANES, w), F32),
                        pltpu.VMEM((SUBLANES, HALO_CONF + TS, LANES), F32),
                        pltpu.VMEM((TS, w), F32), pltpu.VMEM((HALO_SC + TS, w), F32)],
        compiler_params=_params(("parallel",)),
        name="mix_ab",
    )(z, z, z, z, z, z, z, z, z, conf_dw, conf_b, ln_g, ln_b, sc_dw)


def _gla_kernel(q_ref, k_ref, v_ref, go_ref, glr_ref, w2_ref, b_ref, gn_ref, o_ref, st_ref):
    c_len = GLA_CHUNK
    tb = pl.program_id(1)

    @pl.when(tb == 0)
    def _():
        st_ref[...] = jnp.zeros(st_ref.shape, F32)

    ri = lax.broadcasted_iota(jnp.int32, (c_len, c_len), 0)
    ci = lax.broadcasted_iota(jnp.int32, (c_len, c_len), 1)
    tril = ri >= ci
    tril_b = jnp.where(tril, 1.0, 0.0).astype(BF16)
    row = lax.broadcasted_iota(jnp.int32, (c_len, 1), 0)
    nt_dims = (((1,), (1,)), ((), ()))

    def chunk(c, carry):
        rows = pl.ds(pl.multiple_of(c * c_len, c_len), c_len)
        real = tb * GLA_TB + c * c_len + row >= FIRST
        nb, hs = range(GLA_NB), range(GLA_HEADS)
        ch = [(n, h) for n in nb for h in hs]
        lin = [jnp.dot(glr_ref[n, rows, :], w2_ref[...], preferred_element_type=F32) + b_ref[...] for n in nb]
        log_a = [jnp.where(real, -_softplus(-lin[n]) / GLA_GATE_NORM, 0.0) for n in nb]
        cum_all = [_dot_m01_x(tril_b, log_a[n]) for n in nb]
        q_all = [q_ref[n, rows, :] * (GLA_DK ** -0.5) for n in nb]
        k_all = [k_ref[n, rows, :] for n in nb]
        v_all = [v_ref[n, rows, :] for n in nb]
        kc = [slice(h * GLA_DK, (h + 1) * GLA_DK) for h in hs]
        vc = [slice(h * GLA_DV, (h + 1) * GLA_DV) for h in hs]
        cum = [cum_all[n][:, kc[h]] for n, h in ch]
        last = [x[c_len - 1:c_len, :] for x in cum]
        v = [v_all[n][:, vc[h]] for n, h in ch]
        q_dec = [q_all[n][:, kc[h]] * jnp.exp(cum[i]) for i, (n, h) in enumerate(ch)]
        k_dec = [k_all[n][:, kc[h]] * jnp.exp(-cum[i]) for i, (n, h) in enumerate(ch)]
        k_st = [k_all[n][:, kc[h]] * jnp.exp(last[i] - cum[i]) for i, (n, h) in enumerate(ch)]
        scores = [jnp.where(tril, lax.dot_general(q_dec[i], k_dec[i], nt_dims, preferred_element_type=F32), 0.0)
                  for i in range(len(ch))]
        st = [st_ref[n, h] for n, h in ch]
        o = [jnp.dot(scores[i], v[i], preferred_element_type=F32) for i in range(len(ch))]
        o = [o[i] + lax.dot_general(q_dec[i], st[i], nt_dims, preferred_element_type=F32) for i in range(len(ch))]
        upd = [jnp.dot(v[i].T, k_st[i], preferred_element_type=F32) for i in range(len(ch))]
        for i, (n, h) in enumerate(ch):
            st_ref[n, h] = st[i] * jnp.exp(last[i]) + upd[i]
        o = [x * lax.rsqrt(jnp.mean(x * x, axis=-1, keepdims=True) + EPS) for x in o]
        for n in nb:
            go = go_ref[n, rows, :]
            outs = [o[n * GLA_HEADS + h] * gn_ref[:, vc[h]] * _silu(go[:, vc[h]]) for h in hs]
            o_ref[n, rows, :] = jnp.concatenate(outs, axis=-1).astype(BF16)
        return carry

    lax.fori_loop(0, GLA_TB // c_len, chunk, 0)


def gla(z, w2p, b, gn):
    z3 = z.reshape(BATCH, LP, CD_COLS)

    def zcol(width, k):
        return pl.BlockSpec((GLA_NB, GLA_TB, width), lambda b_, t, k=k: (b_, t, k))

    def full(shape):
        return pl.BlockSpec(shape, lambda b_, t: (0, 0))

    return pl.pallas_call(
        _gla_kernel,
        grid=(BATCH // GLA_NB, LP // GLA_TB),
        in_specs=[zcol(GLA_QK, 0), zcol(GLA_QK, 1), zcol(GLA_V, 1), zcol(GLA_V, 2),
                  zcol(GLR_PAD, (GLA_MAIN + RW_COLS) // GLR_PAD),
                  full((GLR_PAD, GLA_QK)), full((1, GLA_QK)), full((1, GLA_V))],
        out_specs=pl.BlockSpec((GLA_NB, GLA_TB, GLA_V), lambda b_, t: (b_, t, 0)),
        out_shape=jax.ShapeDtypeStruct((BATCH, LP, GLA_V), BF16),
        scratch_shapes=[pltpu.VMEM((GLA_NB, GLA_HEADS, GLA_DV, GLA_DK), F32)],
        compiler_params=_params(("parallel", "arbitrary")),
        name="gla",
    )(z3, z3, z3, z3, z3, w2p, b, gn).reshape(MT, GLA_V)


def _head_sum(x, ones_blk):
    n = x.shape[0]
    stacked = jnp.concatenate([x[:, c * LANES:(c + 1) * LANES] for c in range(RW_W // LANES)], axis=0)
    out = _dot_x_m01(stacked, ones_blk)
    return jnp.concatenate([out[c * n:(c + 1) * n, :] for c in range(RW_W // LANES)], axis=-1)


def _head_ones():
    r = lax.broadcasted_iota(jnp.int32, (LANES, LANES), 0) // RW_N
    c = lax.broadcasted_iota(jnp.int32, (LANES, LANES), 1) // RW_N
    return jnp.where(r == c, 1.0, 0.0).astype(BF16)


RW_PAIRS = RW_HEADS // 2
RW_C = 64
RW_NB = 2
FF_CAST_STEPS_UP = 64
FF_CAST_STEPS_DOWN = 44


def _rw_mixer_kernel(r_ref, k_ref, v_ref, lo_ref, mu_ref, mulo_ref, w0_ref, w2_ref, a0_ref, a2_ref, g2_ref,
                     kkp_ref, kap_ref, lg_ref, lb_ref, rk_ref, wup_ref, wdn_ref, o_ref, wup_o_ref, wdn_o_ref,
                     s_ref, prev_ref):
    c = pl.program_id(1)

    wup_o_ref[...] = wup_ref[...].astype(BF16)
    wdn_o_ref[...] = wdn_ref[...].astype(BF16)

    @pl.when(c == 0)
    def _():
        s_ref[...] = jnp.zeros(s_ref.shape, F32)
        prev_ref[...] = jnp.zeros(prev_ref.shape, F32)

    row = lax.broadcasted_iota(jnp.int32, (RW_C, 1), 0)
    ones_blk = _head_ones()
    real = jnp.logical_or(c != 0, row >= FIRST)
    nbs = range(RW_NB)

    def shift_mix(x, n, col0, mu):
        cols = slice(col0, col0 + x.shape[1])
        prev = jnp.where(row == 0, prev_ref[n, 0:1, cols], pltpu.roll(x, 1, 0))
        prev_ref[n, 0:1, cols] = x[RW_C - 1:RW_C, :]
        return x + (prev - x) * mu

    r = [shift_mix(r_ref[n], n, 0, mu_ref[0:1, :]) for n in nbs]
    k = [shift_mix(k_ref[n], n, RW_W, mu_ref[1:2, :]) for n in nbs]
    v = [shift_mix(v_ref[n], n, 2 * RW_W, mu_ref[2:3, :]) for n in nbs]
    lo = [shift_mix(lo_ref[n], n, 3 * RW_W, mulo_ref[...]) for n in nbs]
    xa0, xg0 = RW_DECAY_RANK, RW_DECAY_RANK + RW_A_RANK
    w_lin = [jnp.dot(jnp.tanh(lo[n][:, 0:xa0]), w2_ref[...], preferred_element_type=F32) for n in nbs]
    a_lin = [jnp.dot(lo[n][:, xa0:xg0], a2_ref[...], preferred_element_type=F32) for n in nbs]
    g = [jnp.dot(_sigmoid(lo[n][:, xg0:]), g2_ref[...], preferred_element_type=F32) for n in nbs]
    lw = [_sigmoid(w0_ref[...] + w_lin[n]) * (-math.exp(-0.5)) for n in nbs]
    a = [_sigmoid(a0_ref[...] + a_lin[n]) for n in nbs]
    k = [jnp.where(real, k[n], 0.0) for n in nbs]
    v = [jnp.where(real, v[n], 0.0) for n in nbs]
    kk = [k[n] * kkp_ref[...] for n in nbs]
    kk = [kk[n] / jnp.maximum(jnp.sqrt(_head_sum(kk[n] * kk[n], ones_blk)), 1e-12) for n in nbs]
    bb = [kk[n] * a[n] for n in nbs]
    k = [k[n] * (1.0 + (a[n] - 1.0) * kap_ref[...]) for n in nbs]

    def wide(xs):
        return jnp.concatenate(xs, axis=1)

    ri = lax.broadcasted_iota(jnp.int32, (RW_C, RW_C), 0)
    ci = lax.broadcasted_iota(jnp.int32, (RW_C, RW_C), 1)
    tril_b = jnp.where(ri >= ci, 1.0, 0.0).astype(BF16)
    lw_w = wide(lw)
    cum = _dot_m01_x(tril_b, lw_w)
    e_in, e_out, e_prev = jnp.exp(cum), jnp.exp(-cum), jnp.exp(cum - lw_w)
    e_last = jnp.exp(cum[RW_C - 1:RW_C, :])
    a_all = -wide(kk) * e_prev
    bt_all = wide(bb) * e_out
    kt_all = wide(k) * e_out
    rt_all = wide(r) * e_in
    bh_all = bt_all * e_last
    kh_all = kt_all * e_last
    v_all = wide(v)

    lane = lax.broadcasted_iota(jnp.int32, (RW_C, LANES), 1)
    t_idx = lax.broadcasted_iota(jnp.int32, (RW_C, LANES), 0)
    first = lane < RW_N
    strict, incl = t_idx > lane % RW_C, t_idx >= lane % RW_C
    eye = jnp.where(t_idx == lane % RW_C, 1.0, 0.0)
    r2 = lax.broadcasted_iota(jnp.int32, (2 * LANES, LANES), 0)
    c2 = lax.broadcasted_iota(jnp.int32, (2 * LANES, LANES), 1)
    same_head = (r2 % LANES) // RW_N == c2 // RW_N
    zeros = jnp.zeros((LANES, LANES), F32)
    nt = (((1,), (1,)), ((), ()))

    def bd(x):
        return jnp.concatenate([jnp.where(first, x, 0.0), jnp.where(first, 0.0, x)], axis=0)

    def mm(x, y):
        return jnp.dot(x.astype(BF16), y.astype(BF16), preferred_element_type=F32)

    def cat(xs, axis):
        return jnp.concatenate(xs, axis=axis)

    prs = range(RW_NB * RW_PAIRS)
    cols = [slice(p * LANES, (p + 1) * LANES) for p in prs]
    a, bt, kt, rt, bh, kh, vv = ([x[:, cs] for cs in cols]
                                 for x in (a_all, bt_all, kt_all, rt_all, bh_all, kh_all, v_all))
    v_bd = [bd(vv[p]) for p in prs]
    pm = [lax.dot_general(cat([a[p], rt[p]], 0).astype(BF16), cat([bd(bt[p]), bd(kt[p])], 0).astype(BF16), nt,
                          preferred_element_type=F32) for p in prs]
    l_ak = [jnp.where(strict, pm[p][0:RW_C, LANES:], 0.0) for p in prs]
    m_rb = [jnp.where(incl, pm[p][RW_C:, 0:LANES], 0.0) for p in prs]
    m_rk = [jnp.where(incl, pm[p][RW_C:, LANES:], 0.0) for p in prs]
    x = [jnp.where(strict, pm[p][0:RW_C, 0:LANES], 0.0) for p in prs]
    t_inv = [eye + x[p] for p in prs]
    x = [mm(x[p], bd(x[p])) for p in prs]
    for _ in range(RW_C.bit_length() - 3):
        z = [mm(x[p], cat([bd(t_inv[p]), bd(x[p])], 1)) for p in prs]
        t_inv = [t_inv[p] + z[p][:, 0:LANES] for p in prs]
        x = [z[p][:, LANES:] for p in prs]
    t_inv = [t_inv[p] + mm(x[p], bd(t_inv[p])) for p in prs]
    lv = [mm(l_ak[p], v_bd[p]) for p in prs]
    w12 = [mm(t_inv[p], cat([bd(a[p]), bd(lv[p])], 1)) for p in prs]
    wv = [cat([w12[p], cat([jnp.zeros((RW_C, LANES), F32), vv[p]], 1)], 0) for p in prs]
    wv_bd = [cat([cat([bd(w12[p][:, 0:LANES]), bd(w12[p][:, LANES:])], 1), cat([zeros, v_bd[p]], 1)], 0)
             for p in prs]
    qy = [mm(cat([m_rb[p], m_rk[p]], 1), wv_bd[p]) for p in prs]
    gh = [jnp.where(same_head, mm(wv[p].T, cat([bh[p], kh[p]], 0)), 0.0) for p in prs]
    s = [s_ref[p // RW_PAIRS, p % RW_PAIRS] for p in prs]
    ys = [lax.dot_general((rt[p] + qy[p][:, 0:LANES]).astype(BF16), s[p].astype(BF16), nt,
                          preferred_element_type=F32) + qy[p][:, LANES:] for p in prs]
    sg = [jnp.dot(cat(_split3(s[p]), 1), cat([gh[p][0:LANES, :].astype(BF16)] * 3, 0), preferred_element_type=F32)
          for p in prs]
    for p in prs:
        s_ref[p // RW_PAIRS, p % RW_PAIRS] = s[p] * e_last[:, cols[p]] + sg[p] + gh[p][LANES:, :]

    for n in nbs:
        yn = jnp.concatenate(ys[n * RW_PAIRS:(n + 1) * RW_PAIRS], axis=1)
        mean = _head_sum(yn, ones_blk) * (1.0 / RW_N)
        d = yn - mean
        var = _head_sum(d * d, ones_blk) * (1.0 / RW_N)
        yn = d * lax.rsqrt(var + RW_GN_EPS) * lg_ref[...] + lb_ref[...]
        bonus = _head_sum(r[n] * k[n] * rk_ref[...], ones_blk) * v[n]
        o_ref[n] = ((yn + bonus) * g[n]).astype(BF16)


def rw_mixer(z, mu, w0, w2, a0, a2, g2, kkp, kap, lg, lb, rk, w_up, w_down, cast_layer):
    w = RW_W
    steps = LP // RW_C
    up_rows, dn_rows = D_MODEL // FF_CAST_STEPS_UP, D_FF // FF_CAST_STEPS_DOWN

    def cast_blk(rows, width, n_steps, lead):
        return pl.BlockSpec((None, rows, width), lambda b, c: (lead, jnp.minimum(b * steps + c, n_steps - 1), 0))
    c0 = GLA_MAIN // w
    lo0 = (GLA_MAIN + 3 * w) // RW_LORA
    z3 = z.reshape(BATCH, LP, CD_COLS)

    def zcol(width, k):
        return pl.BlockSpec((RW_NB, RW_C, width), lambda b, c, k=k: (b, c, k))

    def full(shape):
        return pl.BlockSpec(shape, lambda b, c: (0, 0))

    y, w_up_b, w_down_b = pl.pallas_call(
        _rw_mixer_kernel,
        grid=(BATCH // RW_NB, LP // RW_C),
        in_specs=[zcol(w, c0), zcol(w, c0 + 1), zcol(w, c0 + 2), zcol(RW_LORA, lo0),
                  full((3, w)), full((1, RW_LORA)),
                  full((1, w)), full((RW_DECAY_RANK, w)), full((1, w)), full((RW_A_RANK, w)), full((RW_G_RANK, w)),
                  full((1, w)), full((1, w)), full((1, w)), full((1, w)), full((1, w)),
                  cast_blk(up_rows, 2 * D_FF, FF_CAST_STEPS_UP, cast_layer),
                  cast_blk(dn_rows, D_MODEL, FF_CAST_STEPS_DOWN, cast_layer)],
        out_specs=[pl.BlockSpec((RW_NB, RW_C, w), lambda b, c: (b, c, 0)),
                   cast_blk(up_rows, 2 * D_FF, FF_CAST_STEPS_UP, 0),
                   cast_blk(dn_rows, D_MODEL, FF_CAST_STEPS_DOWN, 0)],
        out_shape=[jax.ShapeDtypeStruct((BATCH, LP, w), BF16),
                   jax.ShapeDtypeStruct((1, D_MODEL, 2 * D_FF), BF16),
                   jax.ShapeDtypeStruct((1, D_FF, D_MODEL), BF16)],
        scratch_shapes=[pltpu.VMEM((RW_NB, RW_PAIRS, LANES, LANES), F32),
                        pltpu.VMEM((RW_NB, SUBLANES, RW_COLS), F32)],
        compiler_params=_params(("parallel", "arbitrary")),
        name="rw_mixer",
    )(z3, z3, z3, z3, mu[:, :3 * w].reshape(3, w), mu[:, 3 * w:], w0, w2, a0, a2, g2, kkp, kap, lg, lb, rk,
      w_up, w_down)
    return y.reshape(MT, w), w_up_b, w_down_b


CD_WROWS = 256


def _cd_weight_kernel(wt_ref, o_ref):
    pad = jnp.zeros((GLR_PAD - GLA_RANK, CD_WROWS), F32)
    perm = jnp.concatenate([wt_ref[0:GLA_MAIN, :], wt_ref[GLA_COLS:GLA_COLS + RW_COLS, :],
                            wt_ref[GLA_MAIN:GLA_COLS, :], pad], axis=0)
    o_ref[...] = perm.T.astype(BF16)


def cd_weight(w, layer):
    wt = jnp.swapaxes(w, 1, 2)
    cols = wt.shape[1]
    return pl.pallas_call(
        _cd_weight_kernel,
        grid=(D_MODEL // CD_WROWS,),
        in_specs=[pl.BlockSpec((None, cols, CD_WROWS), lambda i: (layer, 0, i))],
        out_specs=pl.BlockSpec((CD_WROWS, CD_COLS), lambda i: (i, 0)),
        out_shape=jax.ShapeDtypeStruct((D_MODEL, CD_COLS), BF16),
        compiler_params=_params(("parallel",)),
        name="cd_weight",
    )(wt)


def kernel(x, meta, ab_w_in, ab_conf_dw, ab_conf_dw_b, ab_conf_ln_g, ab_conf_ln_b, ab_sc_dw, ab_w_out,
           cd_w_in, cd_gla_w2, cd_gla_b, cd_gla_norm_g, cd_rw_mu, cd_rw_w0, cd_rw_w2, cd_rw_a0, cd_rw_a2,
           cd_rw_g2, cd_rw_kk, cd_rw_ka, cd_rw_rk, cd_rw_ln_g, cd_rw_ln_b, cd_w_out, norm_mix, norm_ffn,
           ffn_w_up, ffn_dw, ffn_w_down, norm_final):
    bs = x.shape[0]
    h = jnp.concatenate([jnp.zeros((bs, PAD, D_MODEL), x.dtype),
                         jnp.broadcast_to(meta[None].astype(x.dtype), (bs, N_META, D_MODEL)), x], axis=1)
    h = h.reshape(MT, D_MODEL)
    row2 = lambda p: p.reshape(1, -1)
    g_final = row2(norm_final)

    z = norm_matmul(h, row2(norm_mix[0]), ab_w_in[0].astype(BF16), TN_AB)
    a_act, s_act = mix_ab(z, ab_conf_dw[0], row2(ab_conf_dw_b[0]), row2(ab_conf_ln_g[0]),
                          row2(ab_conf_ln_b[0]), ab_sc_dw[0])
    h = out_proj(h, a_act, s_act, ab_w_out[0].astype(BF16))
    h = conv_ffn(h, row2(norm_ffn[0]), ffn_w_up[:1].astype(BF16), ffn_dw, ffn_w_down[:1].astype(BF16), g_final,
                 0, False)

    z = norm_matmul(h, row2(norm_mix[1]), cd_weight(cd_w_in, 0), TN_CD)
    w2p = jnp.concatenate([cd_gla_w2[0], jnp.zeros((GLR_PAD - GLA_RANK, GLA_QK), F32)], axis=0)
    o_act = gla(z, w2p, row2(cd_gla_b[0]), row2(cd_gla_norm_g[0]))
    y_act, w_up, w_down = rw_mixer(z, row2(cd_rw_mu[0]), row2(cd_rw_w0[0]), cd_rw_w2[0], row2(cd_rw_a0[0]), cd_rw_a2[0],
                     cd_rw_g2[0], row2(cd_rw_kk[0]), row2(cd_rw_ka[0]), row2(cd_rw_ln_g[0]), row2(cd_rw_ln_b[0]),
                     row2(cd_rw_rk[0]), ffn_w_up, ffn_w_down, 1)
    h = out_proj(h, o_act, y_act, cd_w_out[0].astype(BF16))
    h = conv_ffn(h, row2(norm_ffn[1]), w_up, ffn_dw, w_down, g_final, 1, True, w_layer=0)
    return h.reshape(bs, SEQ, D_MODEL)
```
